```python
import math
import jax, jax.numpy as jnp
from jax import lax
import numpy as np

D_MODEL = 2048
BATCH = 16
SEQ = 2048
DEPTH = 1
DEC_BATCH = 4
DEC_SEQ = 4096
PAST_LEN = 128

HEAD_DIM = 128
HEADS_PER_GROUP = 4
DILATED_GROUPS = ((128, 1), (512, 4), (2048, 16))
N_GROUPS = len(DILATED_GROUPS)
N_ATTN_HEADS = N_GROUPS * HEADS_PER_GROUP
ATTN_W = N_ATTN_HEADS * HEAD_DIM
ATTN_OUT_W = HEADS_PER_GROUP * HEAD_DIM
NUM_BUCKETS = 32
REL_MAX_DIST = 1024
LRU_W = 1536
LRU_BLOCKS = 12
LRU_BLOCK_W = LRU_W // LRU_BLOCKS
LRU_C = 8.0
CONV_W = 4
D_FF = 4 * D_MODEL
NORM_EPS = 1e-6
NEG_INF = -1e30
D_IN = 3 * ATTN_W + 2 * LRU_W + 2 * D_MODEL
SPLIT_POINTS = (ATTN_W, 2 * ATTN_W, 3 * ATTN_W, 3 * ATTN_W + LRU_W,
                3 * ATTN_W + 2 * LRU_W, 3 * ATTN_W + 2 * LRU_W + D_MODEL)

kernel_name = "hybrid_dilated_attn_rglru_encoder"


def _rmsnorm(x, g):
    xf = x.astype(jnp.float32)
    y = xf * lax.rsqrt(jnp.mean(xf * xf, axis=-1, keepdims=True) + NORM_EPS)
    return (y * g.astype(jnp.float32)).astype(x.dtype)


def _t5_bucket(rel):
    half = NUM_BUCKETS // 2
    max_exact = half // 2
    n = jnp.abs(rel)
    nf = jnp.maximum(n, 1).astype(jnp.float32)
    large = max_exact + (jnp.log(nf / max_exact) / math.log(REL_MAX_DIST / max_exact)
                         * (half - max_exact)).astype(jnp.int32)
    large = jnp.minimum(large, half - 1)
    return jnp.where(rel > 0, half, 0) + jnp.where(n < max_exact, n, large)


def _dilated_group(q, k, v, bias_tab, window, dil):
    B, S, H, hd = q.shape
    L = S // dil
    C = window // (2 * dil)
    N = B * dil

    def fold(t):
        return t.reshape(B, L, dil, H, hd).transpose(0, 2, 1, 3, 4).reshape(N, L, H, hd)

    q, k, v = fold(q), fold(k), fold(v)
    nb = -(-L // C)
    Lp = nb * C
    qb = jnp.pad(q, ((0, 0), (0, Lp - L), (0, 0), (0, 0))).reshape(N, nb, C, H, hd)

    def band(t):
        tp = jnp.pad(t, ((0, 0), (C, Lp - L + C), (0, 0), (0, 0))).reshape(N, nb + 2, C, H, hd)
        return jnp.concatenate([tp[:, :-2], tp[:, 1:-1], tp[:, 2:]], axis=2)

    kb, vb = band(k), band(v)
    s = jnp.einsum('nbqhd,nbkhd->nbhqk', qb, kb).astype(jnp.float32) * (HEAD_DIM ** -0.5)
    qi = jnp.arange(C, dtype=jnp.int32)[:, None]
    ki = jnp.arange(3 * C, dtype=jnp.int32)[None, :] - C
    rel = ki - qi
    key_idx = jnp.arange(nb, dtype=jnp.int32)[:, None, None] * C + ki[None]
    valid = (jnp.abs(rel) <= C)[None] & (key_idx >= 0) & (key_idx < L)
    bias = bias_tab.astype(jnp.float32)[_t5_bucket(rel * dil)].transpose(2, 0, 1)
    s = jnp.where(valid[None, :, None], s + bias[None, None], NEG_INF)
    m = jnp.max(s, axis=-1)
    p = jnp.exp(s - m[..., None])
    l = jnp.sum(p, axis=-1)
    o = jnp.einsum('nbhqk,nbkhd->nbqhd', p, vb.astype(jnp.float32))
    o = o / l.transpose(0, 1, 3, 2)[..., None]
    lse = (m + jnp.log(l)).transpose(0, 1, 3, 2)
    o = o.reshape(N, Lp, H, hd)[:, :L]
    lse = lse.reshape(N, Lp, H)[:, :L]
    o = o.reshape(B, dil, L, H, hd).transpose(0, 2, 1, 3, 4).reshape(B, S, H, hd)
    lse = lse.reshape(B, dil, L, H).transpose(0, 2, 1, 3).reshape(B, S, H)
    return o, lse


def _centred_conv(x, w, b):
    S = x.shape[1]
    left = CONV_W // 2
    xp = jnp.pad(x, ((0, 0), (left, CONV_W - 1 - left), (0, 0)))
    y = b
    for t in range(CONV_W):
        y = y + xp[:, t:t + S] * w[t]
    return y


def _block_diag(x, w, b):
    B, S, _ = x.shape
    xb = x.reshape(B, S, LRU_BLOCKS, LRU_BLOCK_W)
    return jnp.einsum('bsni,nij->bsnj', xb, w.astype(jnp.float32)).reshape(B, S, LRU_W) + b.astype(jnp.float32)


def _lin_combine(e1, e2):
    a1, b1 = e1
    a2, b2 = e2
    return a1 * a2, a2 * b1 + b2


def _rglru(x, wa, ba, wx, bx, lam):
    r = jax.nn.sigmoid(_block_diag(x, wa, ba))
    i = jax.nn.sigmoid(_block_diag(x, wx, bx))
    log_a = -LRU_C * r * jax.nn.softplus(-lam.astype(jnp.float32))
    a = jnp.exp(log_a)
    u = jnp.sqrt(-jnp.expm1(2.0 * log_a)) * (i * x)
    _, h = lax.associative_scan(_lin_combine, (a, u), axis=1)
    return h


def _mixer(u, rel_bias, w_in, conv_w, conv_b, lru_wa, lru_ba, lru_wx, lru_bx, lru_lambda,
           w_attn_o, w_rnn_o, w_out):
    B, S, _ = u.shape
    z = u @ w_in
    q, k, v, rx, ry, g_attn, g_rnn = jnp.split(z, SPLIT_POINTS, axis=-1)
    q = q.reshape(B, S, N_ATTN_HEADS, HEAD_DIM)
    k = k.reshape(B, S, N_ATTN_HEADS, HEAD_DIM)
    v = v.reshape(B, S, N_ATTN_HEADS, HEAD_DIM)
    outs, lses = [], []
    for g, (window, dil) in enumerate(DILATED_GROUPS):
        hs = slice(g * HEADS_PER_GROUP, (g + 1) * HEADS_PER_GROUP)
        o, lse = _dilated_group(q[:, :, hs], k[:, :, hs], v[:, :, hs], rel_bias[:, hs], window, dil)
        outs.append(o)
        lses.append(lse)
    wts = jax.nn.softmax(jnp.stack(lses), axis=0)
    attn = jnp.sum(wts[..., None] * jnp.stack(outs), axis=0).reshape(B, S, ATTN_OUT_W).astype(u.dtype)

    xc = _centred_conv(rx.astype(jnp.float32), conv_w.astype(jnp.float32), conv_b.astype(jnp.float32))
    h_fwd = _rglru(xc, lru_wa[0], lru_ba[0], lru_wx[0], lru_bx[0], lru_lambda[0])
    h_bwd = jnp.flip(_rglru(jnp.flip(xc, axis=1), lru_wa[1], lru_ba[1], lru_wx[1], lru_bx[1],
                            lru_lambda[1]), axis=1)
    rnn = ((h_fwd + h_bwd) * jax.nn.gelu(ry.astype(jnp.float32))).astype(u.dtype)

    merged = jax.nn.sigmoid(g_attn) * (attn @ w_attn_o) + jax.nn.sigmoid(g_rnn) * (rnn @ w_rnn_o)
    return merged @ w_out


def _mlp(u, w1, w2):
    return jnp.square(jax.nn.relu(u @ w1)) @ w2


def setup_inputs(seed: int = 0) -> dict:
    key = jax.random.key(seed)
    ks = jax.random.split(key, 24)
    f32 = jnp.float32
    nrm = lambda k, shape, scale: jax.random.normal(k, shape, f32) * scale
    u = jax.random.uniform(ks[12], (DEPTH, 2, LRU_W), f32, minval=0.9, maxval=0.999)
    a0 = u ** (1.0 / LRU_C)
    lru_lambda = jnp.log(a0) - jnp.log1p(-a0)
    return {
        "x_prompt": nrm(ks[0], (BATCH, SEQ, D_MODEL), 1.0),
        "x_sample": nrm(ks[1], (DEC_BATCH, DEC_SEQ, D_MODEL), 1.0),
        "rel_bias": nrm(ks[2], (NUM_BUCKETS, N_ATTN_HEADS), 0.5),
        "norm_mix_g": 1.0 + nrm(ks[3], (DEPTH, D_MODEL), 0.05),
        "w_in": nrm(ks[4], (DEPTH, D_MODEL, D_IN), D_MODEL ** -0.5),
        "conv_w": nrm(ks[5], (DEPTH, CONV_W, LRU_W), 0.5),
        "conv_b": nrm(ks[6], (DEPTH, LRU_W), 0.05),
        "lru_wa": nrm(ks[7], (DEPTH, 2, LRU_BLOCKS, LRU_BLOCK_W, LRU_BLOCK_W), LRU_BLOCK_W ** -0.5),
        "lru_ba": nrm(ks[8], (DEPTH, 2, LRU_W), 0.1),
        "lru_wx": nrm(ks[9], (DEPTH, 2, LRU_BLOCKS, LRU_BLOCK_W, LRU_BLOCK_W), LRU_BLOCK_W ** -0.5),
        "lru_bx": nrm(ks[10], (DEPTH, 2, LRU_W), 0.1),
        "lru_lambda": lru_lambda,
        "w_attn_o": nrm(ks[13], (DEPTH, ATTN_OUT_W, D_MODEL), ATTN_OUT_W ** -0.5),
        "w_rnn_o": nrm(ks[14], (DEPTH, LRU_W, D_MODEL), LRU_W ** -0.5),
        "w_out": nrm(ks[15], (DEPTH, D_MODEL, D_MODEL), D_MODEL ** -0.5),
        "norm_mlp_g": 1.0 + nrm(ks[16], (DEPTH, D_MODEL), 0.05),
        "w_mlp_in": nrm(ks[17], (DEPTH, D_MODEL, D_FF), D_MODEL ** -0.5),
        "w_mlp_out": nrm(ks[18], (DEPTH, D_FF, D_MODEL), D_FF ** -0.5),
        "norm_final_g": 1.0 + nrm(ks[19], (D_MODEL,), 0.05),
    }


def reference(x_prompt, x_sample, rel_bias, norm_mix_g, w_in, conv_w, conv_b, lru_wa, lru_ba,
              lru_wx, lru_bx, lru_lambda, w_attn_o, w_rnn_o, w_out, norm_mlp_g, w_mlp_in,
              w_mlp_out, norm_final_g):
    def trunk(x):
        for l in range(DEPTH):
            x = x + _mixer(_rmsnorm(x, norm_mix_g[l]), rel_bias, w_in[l], conv_w[l], conv_b[l],
                           lru_wa[l], lru_ba[l], lru_wx[l], lru_bx[l], lru_lambda[l],
                           w_attn_o[l], w_rnn_o[l], w_out[l])
            x = x + _mlp(_rmsnorm(x, norm_mlp_g[l]), w_mlp_in[l], w_mlp_out[l])
        return _rmsnorm(x, norm_final_g)

    y_prompt = trunk(x_prompt)
    y_sample = trunk(x_sample)
    return (y_prompt, y_sample)
```

```python
import functools
import math

import jax
import jax.numpy as jnp
from jax import lax
from jax.experimental import pallas as pl
from jax.experimental.pallas import tpu as pltpu

D_MODEL = 2048
HEAD_DIM = 128
HEADS_PER_GROUP = 4
DILATED_GROUPS = ((128, 1), (512, 4), (2048, 16))
N_GROUPS = len(DILATED_GROUPS)
ATTN_W = N_GROUPS * HEADS_PER_GROUP * HEAD_DIM
ATTN_OUT_W = HEADS_PER_GROUP * HEAD_DIM
NUM_BUCKETS = 32
REL_MAX_DIST = 1024
LRU_W = 1536
LRU_BLOCKS = 12
LRU_BLOCK_W = LRU_W // LRU_BLOCKS
LRU_C = 8.0
CONV_W = 4
D_FF = 4 * D_MODEL
NORM_EPS = 1e-6
NEG_INF = -1e30

_F32 = jnp.float32
_BF16 = jnp.bfloat16
_V7X_VMEM_LIMIT_BYTES = 56 * 1024 * 1024
_LANES = 128
_SUBLANES = 8

_SIDE = DILATED_GROUPS[0][0] // (2 * DILATED_GROUPS[0][1])
assert all(w // (2 * d) == _SIDE for w, d in DILATED_GROUPS)
_QBLK = 2 * _SIDE
_KWIN = 4 * _SIDE
assert _QBLK == _LANES

_HEADS_PER_STEP = 2
_HEAD_STEPS = HEADS_PER_GROUP // _HEADS_PER_STEP


def _params(semantics):
    return pltpu.CompilerParams(dimension_semantics=semantics,
                                vmem_limit_bytes=_V7X_VMEM_LIMIT_BYTES)


def _rms(x, g):
    ms = jnp.mean(x * x, axis=-1, keepdims=True)
    return x * lax.rsqrt(ms + NORM_EPS) * g


def _norm_proj_kernel(x_ref, g_ref, w_ref, o_ref, u_ref):
    @pl.when(pl.program_id(1) == 0)
    def _():
        u_ref[...] = _rms(x_ref[...], g_ref[...]).astype(u_ref.dtype)

    o_ref[...] = jnp.dot(u_ref[...], w_ref[...],
                         preferred_element_type=_F32).astype(o_ref.dtype)


def _norm_proj(x2d, g, w, *, tm, tn, out_dtype):
    T, D = x2d.shape
    N = w.shape[1]
    assert T % tm == 0 and N % tn == 0
    return pl.pallas_call(
        _norm_proj_kernel,
        grid=(T // tm, N // tn),
        in_specs=[pl.BlockSpec((tm, D), lambda i, j: (i, 0)),
                  pl.BlockSpec((1, D), lambda i, j: (0, 0)),
                  pl.BlockSpec((D, tn), lambda i, j: (0, j))],
        out_specs=pl.BlockSpec((tm, tn), lambda i, j: (i, j)),
        out_shape=jax.ShapeDtypeStruct((T, N), out_dtype),
        scratch_shapes=[pltpu.VMEM((tm, D), _BF16)],
        compiler_params=_params(("parallel", "arbitrary")),
        name="norm_proj",
    )(x2d, g, w)


def _attn_kernel(q_ref, k_ref, v_ref, b_ref, o_ref, lse_ref, kpad, vpad, *, L):
    W = _HEADS_PER_STEP * HEAD_DIM
    zeros = jnp.zeros((_SIDE, W), _BF16)
    kpad[0:_SIDE, :] = zeros
    kpad[_SIDE + L:2 * _SIDE + L, :] = zeros
    kpad[_SIDE:_SIDE + L, :] = k_ref[...]
    vpad[0:_SIDE, :] = zeros
    vpad[_SIDE + L:2 * _SIDE + L, :] = zeros
    vpad[_SIDE:_SIDE + L, :] = v_ref[...]

    scale = HEAD_DIM ** -0.5
    lane = lax.broadcasted_iota(jnp.int32, (_QBLK, _LANES), 1)
    kidx = lax.broadcasted_iota(jnp.int32, (1, _KWIN), 1)

    def body(i, carry):
        qs = pl.multiple_of(i * _QBLK, _QBLK)
        kpos = kidx + (qs - _SIDE)
        kvalid = jnp.logical_and(kpos >= 0, kpos < L)
        lse_tile = jnp.zeros((_QBLK, _LANES), _F32)
        for h in range(_HEADS_PER_STEP):
            cols = slice(h * HEAD_DIM, (h + 1) * HEAD_DIM)
            q = q_ref[pl.ds(qs, _QBLK), cols]
            kw = kpad[pl.ds(qs, _KWIN), cols]
            vw = vpad[pl.ds(qs, _KWIN), cols]
            s = lax.dot_general(q, kw, (((1,), (1,)), ((), ())),
                                preferred_element_type=_F32)
            s = s * scale + b_ref[h]
            s = jnp.where(kvalid, s, NEG_INF)
            m = jnp.max(s, axis=-1, keepdims=True)
            p = jnp.exp(s - m)
            l = jnp.sum(p, axis=-1, keepdims=True)
            o = jnp.dot(p.astype(_BF16), vw, preferred_element_type=_F32) / l
            o_ref[pl.ds(qs, _QBLK), cols] = o
            lse_tile = jnp.where(lane == h, m + jnp.log(l), lse_tile)
        lse_ref[pl.ds(qs, _QBLK), :] = lse_tile
        return carry

    lax.fori_loop(0, L // _QBLK, body, 0)


def _attn_group(zq, bias_g, group, dil):
    B, S, ZW = zq.shape
    L = S // dil
    assert S % dil == 0 and L % _QBLK == 0
    W = _HEADS_PER_STEP * HEAD_DIM
    per_res = ZW // W
    qb = (group * ATTN_OUT_W) // W
    kb = (ATTN_W + group * ATTN_OUT_W) // W
    vb = (2 * ATTN_W + group * ATTN_OUT_W) // W
    z3 = zq.reshape(B, L, dil * ZW)

    def col(base):
        return lambda b, r, hp: (b, 0, r * per_res + base + hp)

    o, lse = pl.pallas_call(
        functools.partial(_attn_kernel, L=L),
        grid=(B, dil, _HEAD_STEPS),
        in_specs=[pl.BlockSpec((None, L, W), col(qb)),
                  pl.BlockSpec((None, L, W), col(kb)),
                  pl.BlockSpec((None, L, W), col(vb)),
                  pl.BlockSpec((_HEADS_PER_STEP, _QBLK, _KWIN), lambda b, r, hp: (hp, 0, 0))],
        out_specs=[pl.BlockSpec((None, L, W), lambda b, r, hp: (b, 0, r * _HEAD_STEPS + hp)),
                   pl.BlockSpec((None, L, _LANES), lambda b, r, hp: (b, 0, r * _HEAD_STEPS + hp))],
        out_shape=[jax.ShapeDtypeStruct((B, L, dil * ATTN_OUT_W), _F32),
                   jax.ShapeDtypeStruct((B, L, dil * _HEAD_STEPS * _LANES), _F32)],
        scratch_shapes=[pltpu.VMEM((L + 2 * _SIDE, W), _BF16),
                        pltpu.VMEM((L + 2 * _SIDE, W), _BF16)],
        compiler_params=_params(("parallel", "parallel", "parallel")),
        name=f"dilated_attn_g{group}",
    )(z3, z3, z3, bias_g)
    return o.reshape(B * S, ATTN_OUT_W), lse.reshape(B * S, _HEAD_STEPS * _LANES)


def _rel_bucket(rel):
    half = NUM_BUCKETS // 2
    max_exact = half // 2
    n = jnp.abs(rel)
    nf = jnp.maximum(n, 1).astype(_F32)
    large = max_exact + (jnp.log(nf / max_exact) / math.log(REL_MAX_DIST / max_exact)
                         * (half - max_exact)).astype(jnp.int32)
    large = jnp.minimum(large, half - 1)
    return jnp.where(rel > 0, half, 0) + jnp.where(n < max_exact, n, large)


def _bias_tables(rel_bias):
    qi = jnp.arange(_QBLK, dtype=jnp.int32)[:, None]
    kj = jnp.arange(_KWIN, dtype=jnp.int32)[None, :]
    rel = kj - _SIDE - qi
    band = jnp.abs(rel) <= _SIDE
    tabs = []
    for g, (_, dil) in enumerate(DILATED_GROUPS):
        tab = rel_bias.astype(_F32)[:, g * HEADS_PER_GROUP:(g + 1) * HEADS_PER_GROUP]
        bias = tab[_rel_bucket(rel * dil)].transpose(2, 0, 1)
        tabs.append(jnp.where(band[None], bias, NEG_INF))
    return jnp.stack(tabs)


_GATE_ROWS = 256
_XPAD = _SUBLANES


def _rglru_kernel(rx_ref, ry_ref, cw_ref, cb_ref, wg_ref, bg_ref, lam_ref, o_ref,
                  xpad, a_f, u_f, a_b, u_b, *, S):
    TC = _GATE_ROWS
    BW = LRU_BLOCK_W
    xpad[0:_XPAD, :] = jnp.zeros((_XPAD, BW), _F32)
    xpad[_XPAD + S:2 * _XPAD + S, :] = jnp.zeros((_XPAD, BW), _F32)
    xpad[_XPAD:_XPAD + S, :] = rx_ref[...]

    cw = cw_ref[...]
    cb = cb_ref[...]
    decay = -LRU_C * jax.nn.softplus(-lam_ref[...])
    left = CONV_W // 2

    def gate_body(ci, carry):
        t0 = pl.multiple_of(ci * TC, TC)
        v = xpad[pl.ds(t0, TC + 2 * _XPAD), :]
        xc = cb
        for t in range(CONV_W):
            off = _XPAD - left + t
            xc = xc + v[off:off + TC] * cw[t:t + 1]
        g = jnp.dot(xc.astype(_BF16), wg_ref[...], preferred_element_type=_F32) + bg_ref[...]
        for d, (a_ref, u_ref) in enumerate(((a_f, u_f), (a_b, u_b))):
            r = jax.nn.sigmoid(g[:, (2 * d) * BW:(2 * d + 1) * BW])
            i = jax.nn.sigmoid(g[:, (2 * d + 1) * BW:(2 * d + 2) * BW])
            log_a = decay[d:d + 1] * r
            a = jnp.exp(log_a)
            one_minus_a2 = -jnp.tanh(log_a) * (a * a + 1.0)
            a_ref[pl.ds(t0, TC), :] = a
            u_ref[pl.ds(t0, TC), :] = jnp.sqrt(one_minus_a2) * (i * xc)
        return carry

    lax.fori_loop(0, S // TC, gate_body, 0)

    row = lax.broadcasted_iota(jnp.int32, (_SUBLANES, BW), 0)
    n8 = S // _SUBLANES

    def scan8(A, U, h_prev, reverse):
        P, Q = A, U
        for s in (1, 2, 4):
            if reverse:
                keep = row < _SUBLANES - s
                shift = _SUBLANES - s
            else:
                keep = row >= s
                shift = s
            Ps = jnp.where(keep, pltpu.roll(P, shift, axis=0), 1.0)
            Qs = jnp.where(keep, pltpu.roll(Q, shift, axis=0), 0.0)
            Q = P * Qs + Q
            P = Ps * P
        return P * h_prev + Q

    def scan_body(j, carry):
        hf_prev, hb_prev = carry
        tf = pl.multiple_of(j * _SUBLANES, _SUBLANES)
        tb = pl.multiple_of((n8 - 1 - j) * _SUBLANES, _SUBLANES)
        hf = scan8(a_f[pl.ds(tf, _SUBLANES), :], u_f[pl.ds(tf, _SUBLANES), :], hf_prev, False)
        u_f[pl.ds(tf, _SUBLANES), :] = hf
        hb = scan8(a_b[pl.ds(tb, _SUBLANES), :], u_b[pl.ds(tb, _SUBLANES), :], hb_prev, True)
        u_b[pl.ds(tb, _SUBLANES), :] = hb
        return (jnp.broadcast_to(hf[_SUBLANES - 1:_SUBLANES, :], (_SUBLANES, BW)),
                jnp.broadcast_to(hb[0:1, :], (_SUBLANES, BW)))

    zero = jnp.zeros((_SUBLANES, BW), _F32)
    lax.fori_loop(0, n8, scan_body, (zero, zero), unroll=2)

    def out_body(ci, carry):
        t0 = pl.multiple_of(ci * TC, TC)
        h = u_f[pl.ds(t0, TC), :] + u_b[pl.ds(t0, TC), :]
        o_ref[pl.ds(t0, TC), :] = (h * jax.nn.gelu(ry_ref[pl.ds(t0, TC), :])).astype(o_ref.dtype)
        return carry

    lax.fori_loop(0, S // TC, out_body, 0)


def _rglru(zr, rx_col, ry_col, conv_w, conv_b, wg, bg, lam):
    B, S, _ = zr.shape
    BW = LRU_BLOCK_W
    assert S % _GATE_ROWS == 0 and rx_col % BW == 0 and ry_col % BW == 0
    seq = pltpu.VMEM((S, BW), _F32)
    out = pl.pallas_call(
        functools.partial(_rglru_kernel, S=S),
        grid=(B, LRU_BLOCKS),
        in_specs=[pl.BlockSpec((None, S, BW), lambda b, n: (b, 0, rx_col // BW + n)),
                  pl.BlockSpec((None, S, BW), lambda b, n: (b, 0, ry_col // BW + n)),
                  pl.BlockSpec((CONV_W, BW), lambda b, n: (0, n)),
                  pl.BlockSpec((1, BW), lambda b, n: (0, n)),
                  pl.BlockSpec((None, BW, 4 * BW), lambda b, n: (n, 0, 0)),
                  pl.BlockSpec((None, 1, 4 * BW), lambda b, n: (n, 0, 0)),
                  pl.BlockSpec((2, BW), lambda b, n: (0, n))],
        out_specs=pl.BlockSpec((None, S, BW), lambda b, n: (b, 0, n)),
        out_shape=jax.ShapeDtypeStruct((B, S, LRU_W), _BF16),
        scratch_shapes=[pltpu.VMEM((S + 2 * _XPAD, BW), _F32), seq, seq, seq, seq],
        compiler_params=_params(("parallel", "parallel")),
        name="conv_rglru",
    )(zr, zr, conv_w, conv_b, wg, bg, lam)
    return out.reshape(B * S, LRU_W)


def _merge_kernel(o0_ref, o1_ref, o2_ref, l0_ref, l1_ref, l2_ref, rnn_ref, ga_ref, gr_ref,
                  x_ref, wa_ref, wr_ref, wo_ref, out_ref):
    l0, l1, l2 = l0_ref[...], l1_ref[...], l2_ref[...]
    mx = jnp.maximum(jnp.maximum(l0, l1), l2)
    e0, e1, e2 = jnp.exp(l0 - mx), jnp.exp(l1 - mx), jnp.exp(l2 - mx)
    inv = 1.0 / (e0 + e1 + e2)
    w0, w1, w2 = e0 * inv, e1 * inv, e2 * inv
    heads = []
    for h in range(HEADS_PER_GROUP):
        c = (h // _HEADS_PER_STEP) * _LANES + h % _HEADS_PER_STEP
        cols = slice(h * HEAD_DIM, (h + 1) * HEAD_DIM)
        heads.append(w0[:, c:c + 1] * o0_ref[:, cols] + w1[:, c:c + 1] * o1_ref[:, cols]
                     + w2[:, c:c + 1] * o2_ref[:, cols])
    attn = jnp.concatenate(heads, axis=-1).astype(_BF16)
    pa = jnp.dot(attn, wa_ref[...], preferred_element_type=_F32)
    pr = jnp.dot(rnn_ref[...], wr_ref[...], preferred_element_type=_F32)
    merged = jax.nn.sigmoid(ga_ref[...]) * pa + jax.nn.sigmoid(gr_ref[...]) * pr
    out_ref[...] = x_ref[...] + jnp.dot(merged.astype(_BF16), wo_ref[...],
                                        preferred_element_type=_F32)


def _merge(os_, lses, rnn, zr, ga_col, gr_col, x2d, wa, wr, wo, *, tm):
    T, D = x2d.shape
    assert T % tm == 0 and ga_col % D == 0 and gr_col % D == 0
    LW = _HEAD_STEPS * _LANES
    row = lambda w: pl.BlockSpec((tm, w), lambda i: (i, 0))
    const = lambda a: pl.BlockSpec(a.shape, lambda i: (0, 0), pipeline_mode=pl.Buffered(1))
    return pl.pallas_call(
        _merge_kernel,
        grid=(T // tm,),
        in_specs=[row(ATTN_OUT_W)] * 3 + [row(LW)] * 3 + [row(LRU_W)]
        + [pl.BlockSpec((tm, D), lambda i: (i, ga_col // D)),
           pl.BlockSpec((tm, D), lambda i: (i, gr_col // D)),
           row(D), const(wa), const(wr), const(wo)],
        out_specs=row(D),
        out_shape=jax.ShapeDtypeStruct((T, D), _F32),
        compiler_params=_params(("parallel",)),
        name="merge_proj",
    )(*os_, *lses, rnn, zr, zr, x2d, wa, wr, wo)


def _mlp_kernel(x_ref, g_ref, w1_ref, w2_ref, gf_ref, o_ref, u_ref):
    j = pl.program_id(1)

    @pl.when(j == 0)
    def _():
        x = x_ref[...]
        u_ref[...] = _rms(x, g_ref[...]).astype(u_ref.dtype)
        o_ref[...] = x

    h = jnp.dot(u_ref[...], w1_ref[...], preferred_element_type=_F32)
    h = jnp.square(jnp.maximum(h, 0.0)).astype(_BF16)
    o_ref[...] += jnp.dot(h, w2_ref[...], preferred_element_type=_F32)

    @pl.when(j == pl.num_programs(1) - 1)
    def _():
        o_ref[...] = _rms(o_ref[...], gf_ref[...])


def _mlp(x2d, g, w1, w2, gf, *, tm, tf):
    T, D = x2d.shape
    F = w1.shape[1]
    assert T % tm == 0 and F % tf == 0
    return pl.pallas_call(
        _mlp_kernel,
        grid=(T // tm, F // tf),
        in_specs=[pl.BlockSpec((tm, D), lambda i, j: (i, 0)),
                  pl.BlockSpec((1, D), lambda i, j: (0, 0)),
                  pl.BlockSpec((D, tf), lambda i, j: (0, j)),
                  pl.BlockSpec((tf, D), lambda i, j: (j, 0)),
                  pl.BlockSpec((1, D), lambda i, j: (0, 0))],
        out_specs=pl.BlockSpec((tm, D), lambda i, j: (i, 0)),
        out_shape=jax.ShapeDtypeStruct((T, D), _F32),
        scratch_shapes=[pltpu.VMEM((tm, D), _BF16)],
        compiler_params=_params(("parallel", "arbitrary")),
        name="mlp_final_norm",
    )(x2d, g, w1, w2, gf)


_REST_GA = 0
_REST_GR = D_MODEL
_REST_RX = 2 * D_MODEL
_REST_RY = 2 * D_MODEL + LRU_W


def _layer(x, bias_tabs, p):
    B, S, D = x.shape
    T = B * S
    x2d = x.reshape(T, D)
    zq = _norm_proj(x2d, p["g_mix"], p["w_qkv"], tm=1024, tn=ATTN_W, out_dtype=_BF16)
    zr = _norm_proj(x2d, p["g_mix"], p["w_rest"], tm=512, tn=1792, out_dtype=_F32)
    zq3 = zq.reshape(B, S, 3 * ATTN_W)
    os_, lses = [], []
    for g, (_, dil) in enumerate(DILATED_GROUPS):
        o, lse = _attn_group(zq3, bias_tabs[g], g, dil)
        os_.append(o)
        lses.append(lse)
    rnn = _rglru(zr.reshape(B, S, -1), _REST_RX, _REST_RY, p["conv_w"], p["conv_b"],
                 p["w_gate"], p["b_gate"], p["lam"])
    x2 = _merge(os_, lses, rnn, zr, _REST_GA, _REST_GR, x2d,
                p["w_attn_o"], p["w_rnn_o"], p["w_out"], tm=256)
    y = _mlp(x2, p["g_mlp"], p["w_mlp_in"], p["w_mlp_out"], p["g_final"], tm=512, tf=1024)
    return y.reshape(B, S, D)


def kernel(x_prompt, x_sample, rel_bias, norm_mix_g, w_in, conv_w, conv_b, lru_wa, lru_ba,
           lru_wx, lru_bx, lru_lambda, w_attn_o, w_rnn_o, w_out, norm_mlp_g, w_mlp_in,
           w_mlp_out, norm_final_g):
    depth = w_in.shape[0]
    bias_tabs = _bias_tables(rel_bias)
    q_end, rx_end, ry_end, ga_end = 3 * ATTN_W, 3 * ATTN_W + LRU_W, 3 * ATTN_W + 2 * LRU_W, \
        3 * ATTN_W + 2 * LRU_W + D_MODEL
    layers = []
    for l in range(depth):
        wl = w_in[l]
        w_gate = jnp.concatenate([lru_wa[l, 0], lru_wx[l, 0], lru_wa[l, 1], lru_wx[l, 1]],
                                 axis=-1).astype(_BF16)
        b_gate = jnp.concatenate(
            [b.reshape(LRU_BLOCKS, 1, LRU_BLOCK_W)
             for b in (lru_ba[l, 0], lru_bx[l, 0], lru_ba[l, 1], lru_bx[l, 1])], axis=-1)
        layers.append(dict(
            g_mix=norm_mix_g[l].reshape(1, D_MODEL),
            w_qkv=wl[:, :q_end].astype(_BF16),
            w_rest=jnp.concatenate([wl[:, ry_end:ga_end], wl[:, ga_end:], wl[:, q_end:rx_end],
                                    wl[:, rx_end:ry_end]], axis=-1).astype(_BF16),
            conv_w=conv_w[l], conv_b=conv_b[l].reshape(1, LRU_W),
            w_gate=w_gate, b_gate=b_gate.astype(_F32), lam=lru_lambda[l],
            w_attn_o=w_attn_o[l].astype(_BF16), w_rnn_o=w_rnn_o[l].astype(_BF16),
            w_out=w_out[l].astype(_BF16),
            g_mlp=norm_mlp_g[l].reshape(1, D_MODEL),
            w_mlp_in=w_mlp_in[l].astype(_BF16), w_mlp_out=w_mlp_out[l].astype(_BF16),
        ))
    g_final = norm_final_g.reshape(1, D_MODEL)

    def trunk(x):
        assert depth == 1, "final norm is fused into the single layer's MLP kernel"
        return _layer(x, bias_tabs, dict(layers[0], g_final=g_final))

    return (trunk(x_prompt), trunk(x_sample))
```

```python
import functools
import math

import jax
import jax.numpy as jnp
from jax import lax
from jax.experimental import pallas as pl
from jax.experimental.pallas import tpu as pltpu

D_MODEL = 2048
HEAD_DIM = 128
HEADS_PER_GROUP = 4
DILATED_GROUPS = ((128, 1), (512, 4), (2048, 16))
N_GROUPS = len(DILATED_GROUPS)
ATTN_W = N_GROUPS * HEADS_PER_GROUP * HEAD_DIM
ATTN_OUT_W = HEADS_PER_GROUP * HEAD_DIM
NUM_BUCKETS = 32
REL_MAX_DIST = 1024
LRU_W = 1536
LRU_BLOCKS = 12
LRU_BLOCK_W = LRU_W // LRU_BLOCKS
LRU_C = 8.0
CONV_W = 4
D_FF = 4 * D_MODEL
NORM_EPS = 1e-6
NEG_INF = -1e30

_F32 = jnp.float32
_BF16 = jnp.bfloat16
_V7X_VMEM_LIMIT_BYTES = 56 * 1024 * 1024
_LANES = 128
_SUBLANES = 8

_SIDE = DILATED_GROUPS[0][0] // (2 * DILATED_GROUPS[0][1])
assert all(w // (2 * d) == _SIDE for w, d in DILATED_GROUPS)
_QBLK = 2 * _SIDE
_KWIN = 4 * _SIDE
assert _QBLK == _LANES
_GROUP_QKV_W = 3 * ATTN_OUT_W


def _params(semantics):
    return pltpu.CompilerParams(dimension_semantics=semantics,
                                vmem_limit_bytes=_V7X_VMEM_LIMIT_BYTES)


def _rms(x, g):
    ms = jnp.mean(x * x, axis=-1, keepdims=True)
    return x * lax.rsqrt(ms + NORM_EPS) * g


def _lane_block(c):
    return slice(c * _LANES, (c + 1) * _LANES)


def _rel_bucket(rel):
    half = NUM_BUCKETS // 2
    max_exact = half // 2
    n = jnp.abs(rel)
    nf = jnp.maximum(n, 1).astype(_F32)
    large = max_exact + (jnp.log(nf / max_exact) / math.log(REL_MAX_DIST / max_exact)
                         * (half - max_exact)).astype(jnp.int32)
    large = jnp.minimum(large, half - 1)
    return jnp.where(rel > 0, half, 0) + jnp.where(n < max_exact, n, large)


def _bias_kernel(tab_ref, idx_ref, o_ref):
    head = pl.program_id(0)
    idx = idx_ref[...]
    acc = jnp.zeros(idx.shape, _F32)
    for b in range(NUM_BUCKETS):
        acc = jnp.where(idx == b, tab_ref[b, head], acc)
    o_ref[...] = jnp.where(idx >= 0, acc, NEG_INF)


def _bias_tables(rel_bias):
    qi = jnp.arange(_QBLK, dtype=jnp.int32)[:, None]
    kj = jnp.arange(_KWIN, dtype=jnp.int32)[None, :]
    rel = kj - _SIDE - qi
    band = jnp.abs(rel) <= _SIDE
    idx = jnp.stack([jnp.where(band, _rel_bucket(rel * dil), -1) for _, dil in DILATED_GROUPS])
    n_heads = N_GROUPS * HEADS_PER_GROUP
    return pl.pallas_call(
        _bias_kernel,
        grid=(n_heads,),
        in_specs=[pl.BlockSpec(memory_space=pltpu.SMEM),
                  pl.BlockSpec((None, _QBLK, _KWIN), lambda h: (h // HEADS_PER_GROUP, 0, 0))],
        out_specs=pl.BlockSpec((None, _QBLK, _KWIN), lambda h: (h, 0, 0)),
        out_shape=jax.ShapeDtypeStruct((n_heads, _QBLK, _KWIN), _F32),
        name="bias_tables",
    )(rel_bias.astype(_F32), idx)


def _qkv_proj_kernel(x_ref, g_ref, w_ref, *refs):
    out_refs = refs[:N_GROUPS]
    u_ref, stage_ref = refs[N_GROUPS:]
    j = pl.program_id(1)
    tm = x_ref.shape[0]
    n_slabs = _GROUP_QKV_W // _LANES

    @pl.when(j == 0)
    def _():
        u_ref[...] = _rms(x_ref[...], g_ref[...]).astype(u_ref.dtype)

    for g, (_, dil) in enumerate(DILATED_GROUPS):
        o_ref = out_refs[g]

        @pl.when(j == g)
        def _(o_ref=o_ref, dil=dil):
            res = jnp.dot(u_ref[...], w_ref[...], preferred_element_type=_F32)
            if dil == 1:
                o_ref[...] = res.astype(o_ref.dtype)
                return
            for c in range(n_slabs):
                stage_ref[c] = res[:, _lane_block(c)]
            rows = tm // dil
            for r in range(dil):
                for c in range(n_slabs):
                    o_ref[r, :, _lane_block(c)] = stage_ref[
                        c, pl.ds(r, rows, stride=dil), :].astype(o_ref.dtype)


def _qkv_proj(x2d, g, w, B, S, *, tm):
    T, D = x2d.shape
    assert T == B * S and S % tm == 0
    tiles_per_seq = S // tm
    out_shape, out_specs = [], []
    for _, dil in DILATED_GROUPS:
        assert tm % (dil * 16) == 0
        if dil == 1:
            out_shape.append(jax.ShapeDtypeStruct((T, _GROUP_QKV_W), _BF16))
            out_specs.append(pl.BlockSpec((tm, _GROUP_QKV_W), lambda i, j: (i, 0)))
        else:
            out_shape.append(jax.ShapeDtypeStruct((B, dil, S // dil, _GROUP_QKV_W), _BF16))
            out_specs.append(pl.BlockSpec(
                (None, dil, tm // dil, _GROUP_QKV_W),
                lambda i, j: (i // tiles_per_seq, 0, i % tiles_per_seq, 0)))
    return pl.pallas_call(
        _qkv_proj_kernel,
        grid=(T // tm, N_GROUPS),
        in_specs=[pl.BlockSpec((tm, D), lambda i, j: (i, 0)),
                  pl.BlockSpec((1, D), lambda i, j: (0, 0)),
                  pl.BlockSpec((D, _GROUP_QKV_W), lambda i, j: (0, j))],
        out_specs=out_specs,
        out_shape=out_shape,
        scratch_shapes=[pltpu.VMEM((tm, D), _BF16),
                        pltpu.VMEM((_GROUP_QKV_W // _LANES, tm, _LANES), _F32)],
        compiler_params=_params(("parallel", "arbitrary")),
        name="qkv_proj",
    )(x2d, g, w)


def _norm_proj_kernel(x_ref, g_ref, w_ref, o_ref, u_ref):
    @pl.when(pl.program_id(1) == 0)
    def _():
        u_ref[...] = _rms(x_ref[...], g_ref[...]).astype(u_ref.dtype)

    o_ref[...] = jnp.dot(u_ref[...], w_ref[...],
                         preferred_element_type=_F32).astype(o_ref.dtype)


def _norm_proj(x2d, g, w, *, tm, tn, out_dtype):
    T, D = x2d.shape
    N = w.shape[1]
    assert T % tm == 0 and N % tn == 0
    return pl.pallas_call(
        _norm_proj_kernel,
        grid=(T // tm, N // tn),
        in_specs=[pl.BlockSpec((tm, D), lambda i, j: (i, 0)),
                  pl.BlockSpec((1, D), lambda i, j: (0, 0)),
                  pl.BlockSpec((D, tn), lambda i, j: (0, j))],
        out_specs=pl.BlockSpec((tm, tn), lambda i, j: (i, j)),
        out_shape=jax.ShapeDtypeStruct((T, N), out_dtype),
        scratch_shapes=[pltpu.VMEM((tm, D), _BF16)],
        compiler_params=_params(("parallel", "arbitrary")),
        name="norm_proj",
    )(x2d, g, w)


def _attn_kernel(q_ref, k_ref, v_ref, b_ref, o_ref, lse_ref, kpad, vpad, *, L, dil, heads, unroll):
    W = heads * HEAD_DIM
    zeros = jnp.zeros((_SIDE, W), _BF16)
    for r in range(dil):
        for pad, src in ((kpad, k_ref), (vpad, v_ref)):
            pad[r, 0:_SIDE, :] = zeros
            pad[r, _SIDE + L:2 * _SIDE + L, :] = zeros
            pad[r, _SIDE:_SIDE + L, :] = src[r]

    scale = HEAD_DIM ** -0.5
    lane = lax.broadcasted_iota(jnp.int32, (_QBLK, _LANES), 1)
    kidx = lax.broadcasted_iota(jnp.int32, (1, _KWIN), 1)
    nblk = L // _QBLK
    assert nblk & (nblk - 1) == 0
    blk_shift = nblk.bit_length() - 1

    def body(n, carry):
        r = lax.shift_right_logical(n, blk_shift)
        qs = pl.multiple_of((n & (nblk - 1)) * _QBLK, _QBLK)
        kpos = kidx + (qs - _SIDE)
        kvalid = jnp.logical_and(kpos >= 0, kpos < L)
        lse_tile = jnp.zeros((_QBLK, _LANES), _F32)
        for h in range(heads):
            cols = _lane_block(h)
            q = q_ref[r, pl.ds(qs, _QBLK), cols]
            kw = kpad[r, pl.ds(qs, _KWIN), cols]
            vw = vpad[r, pl.ds(qs, _KWIN), cols]
            s = lax.dot_general(q, kw, (((1,), (1,)), ((), ())),
                                preferred_element_type=_F32)
            s = s * scale + b_ref[h]
            s = jnp.where(kvalid, s, NEG_INF)
            m = jnp.max(s, axis=-1, keepdims=True)
            p = jnp.exp(s - m)
            l = jnp.sum(p, axis=-1, keepdims=True)
            o = jnp.dot(p.astype(_BF16), vw, preferred_element_type=_F32) / l
            o_ref[r, pl.ds(qs, _QBLK), cols] = o
            lse_tile = jnp.where(lane == h, m + jnp.log(l), lse_tile)
        lse_ref[r, pl.ds(qs, _QBLK), :] = lse_tile
        return carry

    lax.fori_loop(0, dil * nblk, body, 0, unroll=unroll)


def _attn_heads_per_step(S):
    return HEADS_PER_GROUP if S <= 2048 else HEADS_PER_GROUP // 2


def _attn_group(zg, bias, group, dil, B, S):
    L = S // dil
    assert L % _QBLK == 0
    heads = _attn_heads_per_step(S)
    steps = HEADS_PER_GROUP // heads
    W = heads * HEAD_DIM
    zg = zg.reshape(B, dil, L, _GROUP_QKV_W)
    col = lambda part: (lambda b, hp: (b, 0, 0, part * steps + hp))
    return pl.pallas_call(
        functools.partial(_attn_kernel, L=L, dil=dil, heads=heads,
                          unroll=min(8 // heads, dil * L // _QBLK)),
        grid=(B, steps),
        in_specs=[pl.BlockSpec((None, dil, L, W), col(0)),
                  pl.BlockSpec((None, dil, L, W), col(1)),
                  pl.BlockSpec((None, dil, L, W), col(2)),
                  pl.BlockSpec((heads, _QBLK, _KWIN),
                               lambda b, hp: (group * steps + hp, 0, 0))],
        out_specs=[pl.BlockSpec((None, dil, L, W), lambda b, hp: (b, 0, 0, hp)),
                   pl.BlockSpec((None, dil, L, _LANES), lambda b, hp: (b, 0, 0, hp))],
        out_shape=[jax.ShapeDtypeStruct((B, dil, L, ATTN_OUT_W), _F32),
                   jax.ShapeDtypeStruct((B, dil, L, steps * _LANES), _F32)],
        scratch_shapes=[pltpu.VMEM((dil, L + 2 * _SIDE, W), _BF16),
                        pltpu.VMEM((dil, L + 2 * _SIDE, W), _BF16)],
        compiler_params=_params(("parallel", "parallel")),
        name=f"dilated_attn_g{group}",
    )(zg, zg, zg, bias)


_GATE_ROWS = 256
_XPAD = _SUBLANES


def _rglru_kernel(rx_ref, ry_ref, cw_ref, cb_ref, wg_ref, bg_ref, lam_ref, o_ref,
                  xpad, a_f, u_f, a_b, u_b, *, S):
    TC = _GATE_ROWS
    BW = LRU_BLOCK_W
    xpad[0:_XPAD, :] = jnp.zeros((_XPAD, BW), _F32)
    xpad[_XPAD + S:2 * _XPAD + S, :] = jnp.zeros((_XPAD, BW), _F32)
    xpad[_XPAD:_XPAD + S, :] = rx_ref[...]

    cw = cw_ref[...]
    cb = cb_ref[...]
    decay = -LRU_C * jax.nn.softplus(-lam_ref[...])
    left = CONV_W // 2

    def gate_body(ci, carry):
        t0 = pl.multiple_of(ci * TC, TC)
        v = xpad[pl.ds(t0, TC + 2 * _XPAD), :]
        xc = cb
        for t in range(CONV_W):
            off = _XPAD - left + t
            xc = xc + v[off:off + TC] * cw[t:t + 1]
        g = jnp.dot(xc.astype(_BF16), wg_ref[...], preferred_element_type=_F32) + bg_ref[...]
        for d, (a_ref, u_ref) in enumerate(((a_f, u_f), (a_b, u_b))):
            r = jax.nn.sigmoid(g[:, (2 * d) * BW:(2 * d + 1) * BW])
            i = jax.nn.sigmoid(g[:, (2 * d + 1) * BW:(2 * d + 2) * BW])
            log_a = decay[d:d + 1] * r
            a = jnp.exp(log_a)
            one_minus_a2 = -jnp.tanh(log_a) * (a * a + 1.0)
            a_ref[pl.ds(t0, TC), :] = a
            u_ref[pl.ds(t0, TC), :] = jnp.sqrt(one_minus_a2) * (i * xc)
        return carry

    lax.fori_loop(0, S // TC, gate_body, 0)

    row = lax.broadcasted_iota(jnp.int32, (_SUBLANES, BW), 0)
    n8 = S // _SUBLANES

    def scan8(A, U, h_prev, reverse):
        P, Q = A, U
        for s in (1, 2, 4):
            if reverse:
                keep = row < _SUBLANES - s
                shift = _SUBLANES - s
            else:
                keep = row >= s
                shift = s
            Ps = jnp.where(keep, pltpu.roll(P, shift, axis=0), 1.0)
            Qs = jnp.where(keep, pltpu.roll(Q, shift, axis=0), 0.0)
            Q = P * Qs + Q
            P = Ps * P
        return P * h_prev + Q

    def scan_body(j, carry):
        hf_prev, hb_prev = carry
        tf = pl.multiple_of(j * _SUBLANES, _SUBLANES)
        tb = pl.multiple_of((n8 - 1 - j) * _SUBLANES, _SUBLANES)
        hf = scan8(a_f[pl.ds(tf, _SUBLANES), :], u_f[pl.ds(tf, _SUBLANES), :], hf_prev, False)
        u_f[pl.ds(tf, _SUBLANES), :] = hf
        hb = scan8(a_b[pl.ds(tb, _SUBLANES), :], u_b[pl.ds(tb, _SUBLANES), :], hb_prev, True)
        u_b[pl.ds(tb, _SUBLANES), :] = hb
        return (jnp.broadcast_to(hf[_SUBLANES - 1:_SUBLANES, :], (_SUBLANES, BW)),
                jnp.broadcast_to(hb[0:1, :], (_SUBLANES, BW)))

    zero = jnp.zeros((_SUBLANES, BW), _F32)
    lax.fori_loop(0, n8, scan_body, (zero, zero), unroll=2)

    def out_body(ci, carry):
        t0 = pl.multiple_of(ci * TC, TC)
        h = u_f[pl.ds(t0, TC), :] + u_b[pl.ds(t0, TC), :]
        o_ref[pl.ds(t0, TC), :] = (h * jax.nn.gelu(ry_ref[pl.ds(t0, TC), :])).astype(o_ref.dtype)
        return carry

    lax.fori_loop(0, S // TC, out_body, 0)


def _rglru(zr, rx_col, ry_col, conv_w, conv_b, wg, bg, lam):
    B, S, _ = zr.shape
    BW = LRU_BLOCK_W
    assert S % _GATE_ROWS == 0 and rx_col % BW == 0 and ry_col % BW == 0
    seq = pltpu.VMEM((S, BW), _F32)
    out = pl.pallas_call(
        functools.partial(_rglru_kernel, S=S),
        grid=(B, LRU_BLOCKS),
        in_specs=[pl.BlockSpec((None, S, BW), lambda b, n: (b, 0, rx_col // BW + n)),
                  pl.BlockSpec((None, S, BW), lambda b, n: (b, 0, ry_col // BW + n)),
                  pl.BlockSpec((CONV_W, BW), lambda b, n: (0, n)),
                  pl.BlockSpec((1, BW), lambda b, n: (0, n)),
                  pl.BlockSpec((None, BW, 4 * BW), lambda b, n: (n, 0, 0)),
                  pl.BlockSpec((None, 1, 4 * BW), lambda b, n: (n, 0, 0)),
                  pl.BlockSpec((2, BW), lambda b, n: (0, n))],
        out_specs=pl.BlockSpec((None, S, BW), lambda b, n: (b, 0, n)),
        out_shape=jax.ShapeDtypeStruct((B, S, LRU_W), _BF16),
        scratch_shapes=[pltpu.VMEM((S + 2 * _XPAD, BW), _F32), seq, seq, seq, seq],
        compiler_params=_params(("parallel", "parallel")),
        name="conv_rglru",
    )(zr, zr, conv_w, conv_b, wg, bg, lam)
    return out.reshape(B * S, LRU_W)


def _unfold(src_ref, dst_ref, dil):
    rows = src_ref.shape[1]
    for r in range(dil):
        for c in range(dst_ref.shape[0]):
            dst_ref[c, pl.ds(r, rows, stride=dil), :] = src_ref[r, :, _lane_block(c)]


def _merge_kernel(*refs, heads):
    o_refs = refs[0:N_GROUPS]
    l_refs = refs[N_GROUPS:2 * N_GROUPS]
    rnn_ref, ga_ref, gr_ref, x_ref, wa_ref, wr_ref, wo_ref, out_ref = refs[2 * N_GROUPS:2 * N_GROUPS + 8]
    scratch = refs[2 * N_GROUPS + 8:]

    o_get, l_get, k = [], [], 0
    for g, (_, dil) in enumerate(DILATED_GROUPS):
        if dil == 1:
            o_get.append(lambda c, ref=o_refs[g]: ref[:, _lane_block(c)])
            l_get.append(lambda c, ref=l_refs[g]: ref[:, _lane_block(c)])
        else:
            o_scr, l_scr = scratch[k], scratch[k + 1]
            k += 2
            _unfold(o_refs[g], o_scr, dil)
            _unfold(l_refs[g], l_scr, dil)
            o_get.append(lambda c, ref=o_scr: ref[c])
            l_get.append(lambda c, ref=l_scr: ref[c])

    wts = []
    for step in range(HEADS_PER_GROUP // heads):
        ls = [get(step) for get in l_get]
        mx = functools.reduce(jnp.maximum, ls)
        es = [jnp.exp(l - mx) for l in ls]
        inv = 1.0 / functools.reduce(jnp.add, es)
        wts.append([e * inv for e in es])
    attn = []
    for h in range(HEADS_PER_GROUP):
        w = wts[h // heads]
        c = h % heads
        attn.append(functools.reduce(
            jnp.add, [w[g][:, c:c + 1] * o_get[g](h) for g in range(N_GROUPS)]))
    attn = jnp.concatenate(attn, axis=-1).astype(_BF16)
    pa = jnp.dot(attn, wa_ref[...], preferred_element_type=_F32)
    pr = jnp.dot(rnn_ref[...], wr_ref[...], preferred_element_type=_F32)
    merged = jax.nn.sigmoid(ga_ref[...]) * pa + jax.nn.sigmoid(gr_ref[...]) * pr
    out_ref[...] = x_ref[...] + jnp.dot(merged.astype(_BF16), wo_ref[...],
                                        preferred_element_type=_F32)


def _merge(os_, lses, rnn, zr, ga_col, gr_col, x2d, wa, wr, wo, B, S, *, tm):
    T, D = x2d.shape
    assert S % tm == 0 and ga_col % D == 0 and gr_col % D == 0
    heads = _attn_heads_per_step(S)
    LW = (HEADS_PER_GROUP // heads) * _LANES
    tiles_per_seq = S // tm
    row = lambda w: pl.BlockSpec((tm, w), lambda i: (i, 0))
    const = lambda a: pl.BlockSpec(a.shape, lambda i: (0, 0), pipeline_mode=pl.Buffered(1))

    def group_spec(dil, w):
        if dil == 1:
            return row(w)
        assert tm % (dil * _SUBLANES) == 0
        return pl.BlockSpec((None, dil, tm // dil, w),
                            lambda i: (i // tiles_per_seq, 0, i % tiles_per_seq, 0))

    scratch = []
    for _, dil in DILATED_GROUPS:
        if dil > 1:
            scratch += [pltpu.VMEM((ATTN_OUT_W // _LANES, tm, _LANES), _F32),
                        pltpu.VMEM((LW // _LANES, tm, _LANES), _F32)]
    os_ = [o.reshape(T, ATTN_OUT_W) if dil == 1 else o for o, (_, dil) in zip(os_, DILATED_GROUPS)]
    lses = [l.reshape(T, LW) if dil == 1 else l for l, (_, dil) in zip(lses, DILATED_GROUPS)]
    return pl.pallas_call(
        functools.partial(_merge_kernel, heads=heads),
        grid=(T // tm,),
        in_specs=[group_spec(dil, ATTN_OUT_W) for _, dil in DILATED_GROUPS]
        + [group_spec(dil, LW) for _, dil in DILATED_GROUPS]
        + [row(LRU_W),
           pl.BlockSpec((tm, D), lambda i: (i, ga_col // D)),
           pl.BlockSpec((tm, D), lambda i: (i, gr_col // D)),
           row(D), const(wa), const(wr), const(wo)],
        out_specs=row(D),
        out_shape=jax.ShapeDtypeStruct((T, D), _F32),
        scratch_shapes=scratch,
        compiler_params=_params(("parallel",)),
        name="merge_proj",
    )(*os_, *lses, rnn, zr, zr, x2d, wa, wr, wo)


def _mlp_kernel(x_ref, g_ref, w1_ref, w2_ref, gf_ref, o_ref, u_ref):
    j = pl.program_id(1)

    @pl.when(j == 0)
    def _():
        x = x_ref[...]
        u_ref[...] = _rms(x, g_ref[...]).astype(u_ref.dtype)
        o_ref[...] = x

    h = jnp.dot(u_ref[...], w1_ref[...], preferred_element_type=_F32)
    h = jnp.square(jnp.maximum(h, 0.0)).astype(_BF16)
    o_ref[...] += jnp.dot(h, w2_ref[...], preferred_element_type=_F32)

    @pl.when(j == pl.num_programs(1) - 1)
    def _():
        o_ref[...] = _rms(o_ref[...], gf_ref[...])


def _mlp(x2d, g, w1, w2, gf, *, tm, tf):
    T, D = x2d.shape
    F = w1.shape[1]
    assert T % tm == 0 and F % tf == 0
    return pl.pallas_call(
        _mlp_kernel,
        grid=(T // tm, F // tf),
        in_specs=[pl.BlockSpec((tm, D), lambda i, j: (i, 0)),
                  pl.BlockSpec((1, D), lambda i, j: (0, 0)),
                  pl.BlockSpec((D, tf), lambda i, j: (0, j)),
                  pl.BlockSpec((tf, D), lambda i, j: (j, 0)),
                  pl.BlockSpec((1, D), lambda i, j: (0, 0))],
        out_specs=pl.BlockSpec((tm, D), lambda i, j: (i, 0)),
        out_shape=jax.ShapeDtypeStruct((T, D), _F32),
        scratch_shapes=[pltpu.VMEM((tm, D), _BF16)],
        compiler_params=_params(("parallel", "arbitrary")),
        name="mlp_final_norm",
    )(x2d, g, w1, w2, gf)


_REST_GA = 0
_REST_GR = D_MODEL
_REST_RX = 2 * D_MODEL
_REST_RY = 2 * D_MODEL + LRU_W


def _layer(x, bias_tabs, p):
    B, S, D = x.shape
    T = B * S
    x2d = x.reshape(T, D)
    zgs = _qkv_proj(x2d, p["g_mix"], p["w_qkv"], B, S, tm=512)
    zr = _norm_proj(x2d, p["g_mix"], p["w_rest"], tm=512, tn=1792, out_dtype=_F32)
    os_, lses = [], []
    for g, (_, dil) in enumerate(DILATED_GROUPS):
        o, lse = _attn_group(zgs[g], bias_tabs, g, dil, B, S)
        os_.append(o)
        lses.append(lse)
    rnn = _rglru(zr.reshape(B, S, -1), _REST_RX, _REST_RY, p["conv_w"], p["conv_b"],
                 p["w_gate"], p["b_gate"], p["lam"])
    x2 = _merge(os_, lses, rnn, zr, _REST_GA, _REST_GR, x2d,
                p["w_attn_o"], p["w_rnn_o"], p["w_out"], B, S, tm=256)
    y = _mlp(x2, p["g_mlp"], p["w_mlp_in"], p["w_mlp_out"], p["g_final"], tm=512, tf=1024)
    return y.reshape(B, S, D)


def kernel(x_prompt, x_sample, rel_bias, norm_mix_g, w_in, conv_w, conv_b, lru_wa, lru_ba,
           lru_wx, lru_bx, lru_lambda, w_attn_o, w_rnn_o, w_out, norm_mlp_g, w_mlp_in,
           w_mlp_out, norm_final_g):
    depth = w_in.shape[0]
    assert depth == 1, "the final norm is fused into the single layer's MLP kernel"
    bias_tabs = _bias_tables(rel_bias)
    q_end, rx_end, ry_end, ga_end = 3 * ATTN_W, 3 * ATTN_W + LRU_W, 3 * ATTN_W + 2 * LRU_W, \
        3 * ATTN_W + 2 * LRU_W + D_MODEL
    wl = w_in[0]
    w_qkv = jnp.concatenate(
        [wl[:, part * ATTN_W + g * ATTN_OUT_W:part * ATTN_W + (g + 1) * ATTN_OUT_W]
         for g in range(N_GROUPS) for part in range(3)], axis=-1).astype(_BF16)
    w_gate = jnp.concatenate([lru_wa[0, 0], lru_wx[0, 0], lru_wa[0, 1], lru_wx[0, 1]],
                             axis=-1).astype(_BF16)
    b_gate = jnp.concatenate(
        [b.reshape(LRU_BLOCKS, 1, LRU_BLOCK_W)
         for b in (lru_ba[0, 0], lru_bx[0, 0], lru_ba[0, 1], lru_bx[0, 1])], axis=-1)
    p = dict(
        g_mix=norm_mix_g[0].reshape(1, D_MODEL),
        w_qkv=w_qkv,
        w_rest=jnp.concatenate([wl[:, ry_end:ga_end], wl[:, ga_end:], wl[:, q_end:rx_end],
                                wl[:, rx_end:ry_end]], axis=-1).astype(_BF16),
        conv_w=conv_w[0], conv_b=conv_b[0].reshape(1, LRU_W),
        w_gate=w_gate, b_gate=b_gate.astype(_F32), lam=lru_lambda[0],
        w_attn_o=w_attn_o[0].astype(_BF16), w_rnn_o=w_rnn_o[0].astype(_BF16),
        w_out=w_out[0].astype(_BF16),
        g_mlp=norm_mlp_g[0].reshape(1, D_MODEL),
        w_mlp_in=w_mlp_in[0].astype(_BF16), w_mlp_out=w_mlp_out[0].astype(_BF16),
        g_final=norm_final_g.reshape(1, D_MODEL),
    )
    return (_layer(x_prompt, bias_tabs, p), _layer(x_sample, bias_tabs, p))
```

```python
import functools
import math

import jax
import jax.numpy as jnp
from jax import lax
from jax.experimental import pallas as pl
from jax.experimental.pallas import tpu as pltpu

D_MODEL = 2048
HEAD_DIM = 128
HEADS_PER_GROUP = 4
DILATED_GROUPS = ((128, 1), (512, 4), (2048, 16))
N_GROUPS = len(DILATED_GROUPS)
ATTN_W = N_GROUPS * HEADS_PER_GROUP * HEAD_DIM
ATTN_OUT_W = HEADS_PER_GROUP * HEAD_DIM
NUM_BUCKETS = 32
REL_MAX_DIST = 1024
LRU_W = 1536
LRU_BLOCKS = 12
LRU_BLOCK_W = LRU_W // LRU_BLOCKS
LRU_C = 8.0
CONV_W = 4
D_FF = 4 * D_MODEL
NORM_EPS = 1e-6
NEG_INF = -1e30

_F32 = jnp.float32
_BF16 = jnp.bfloat16
_V7X_VMEM_LIMIT_BYTES = 56 * 1024 * 1024
_LANES = 128
_SUBLANES = 8

_SIDE = DILATED_GROUPS[0][0] // (2 * DILATED_GROUPS[0][1])
assert all(w // (2 * d) == _SIDE for w, d in DILATED_GROUPS)
_QBLK = 2 * _SIDE
_KWIN = 4 * _SIDE
assert _QBLK == _LANES
_GROUP_QKV_W = 3 * ATTN_OUT_W


def _params(semantics):
    return pltpu.CompilerParams(dimension_semantics=semantics,
                                vmem_limit_bytes=_V7X_VMEM_LIMIT_BYTES)


def _rms(x, g):
    ms = jnp.mean(x * x, axis=-1, keepdims=True)
    return x * lax.rsqrt(ms + NORM_EPS) * g


def _lane_block(c):
    return slice(c * _LANES, (c + 1) * _LANES)


def _rel_bucket(rel):
    half = NUM_BUCKETS // 2
    max_exact = half // 2
    n = jnp.abs(rel)
    nf = jnp.maximum(n, 1).astype(_F32)
    large = max_exact + (jnp.log(nf / max_exact) / math.log(REL_MAX_DIST / max_exact)
                         * (half - max_exact)).astype(jnp.int32)
    large = jnp.minimum(large, half - 1)
    return jnp.where(rel > 0, half, 0) + jnp.where(n < max_exact, n, large)


def _bias_kernel(tab_ref, idx_ref, o_ref):
    head = pl.program_id(0)
    idx = idx_ref[...]
    acc = jnp.zeros(idx.shape, _F32)
    for b in range(NUM_BUCKETS):
        acc = jnp.where(idx == b, tab_ref[b, head], acc)
    o_ref[...] = jnp.where(idx >= 0, acc, NEG_INF)


def _bias_tables(rel_bias):
    qi = jnp.arange(_QBLK, dtype=jnp.int32)[:, None]
    kj = jnp.arange(_KWIN, dtype=jnp.int32)[None, :]
    rel = kj - _SIDE - qi
    band = jnp.abs(rel) <= _SIDE
    idx = jnp.stack([jnp.where(band, _rel_bucket(rel * dil), -1) for _, dil in DILATED_GROUPS])
    n_heads = N_GROUPS * HEADS_PER_GROUP
    return pl.pallas_call(
        _bias_kernel,
        grid=(n_heads,),
        in_specs=[pl.BlockSpec(memory_space=pltpu.SMEM),
                  pl.BlockSpec((None, _QBLK, _KWIN), lambda h: (h // HEADS_PER_GROUP, 0, 0))],
        out_specs=pl.BlockSpec((None, _QBLK, _KWIN), lambda h: (h, 0, 0)),
        out_shape=jax.ShapeDtypeStruct((n_heads, _QBLK, _KWIN), _F32),
        name="bias_tables",
    )(rel_bias.astype(_F32), idx)


def _qkv_proj_kernel(x_ref, g_ref, w_ref, *refs):
    out_refs = refs[:N_GROUPS]
    u_ref, stage_ref = refs[N_GROUPS:]
    tm = x_ref.shape[0]
    n_slabs = _GROUP_QKV_W // _LANES
    u_ref[...] = _rms(x_ref[...], g_ref[...]).astype(u_ref.dtype)
    for g, (_, dil) in enumerate(DILATED_GROUPS):
        o_ref = out_refs[g]
        res = jnp.dot(u_ref[...], w_ref[:, g * _GROUP_QKV_W:(g + 1) * _GROUP_QKV_W],
                      preferred_element_type=_F32)
        if dil == 1:
            o_ref[...] = res.astype(o_ref.dtype)
            continue
        for c in range(n_slabs):
            stage_ref[c] = res[:, _lane_block(c)]
        rows = tm // dil
        for r in range(dil):
            for c in range(n_slabs):
                o_ref[r, :, _lane_block(c)] = stage_ref[
                    c, pl.ds(r, rows, stride=dil), :].astype(o_ref.dtype)


def _qkv_proj(x2d, g, w, B, S, *, tm):
    T, D = x2d.shape
    assert T == B * S and S % tm == 0
    tiles_per_seq = S // tm
    out_shape, out_specs = [], []
    for _, dil in DILATED_GROUPS:
        assert tm % (dil * 16) == 0
        if dil == 1:
            out_shape.append(jax.ShapeDtypeStruct((T, _GROUP_QKV_W), _BF16))
            out_specs.append(pl.BlockSpec((tm, _GROUP_QKV_W), lambda i: (i, 0)))
        else:
            out_shape.append(jax.ShapeDtypeStruct((B, dil, S // dil, _GROUP_QKV_W), _BF16))
            out_specs.append(pl.BlockSpec(
                (None, dil, tm // dil, _GROUP_QKV_W),
                lambda i: (i // tiles_per_seq, 0, i % tiles_per_seq, 0)))
    return pl.pallas_call(
        _qkv_proj_kernel,
        grid=(T // tm,),
        in_specs=[pl.BlockSpec((tm, D), lambda i: (i, 0)),
                  pl.BlockSpec((1, D), lambda i: (0, 0)),
                  pl.BlockSpec(w.shape, lambda i: (0, 0), pipeline_mode=pl.Buffered(1))],
        out_specs=out_specs,
        out_shape=out_shape,
        scratch_shapes=[pltpu.VMEM((tm, D), _BF16),
                        pltpu.VMEM((_GROUP_QKV_W // _LANES, tm, _LANES), _F32)],
        compiler_params=_params(("parallel",)),
        name="qkv_proj",
    )(x2d, g, w)


def _norm_proj_kernel(x_ref, g_ref, w_ref, o_ref, u_ref, *, tn):
    u_ref[...] = _rms(x_ref[...], g_ref[...]).astype(u_ref.dtype)
    for c in range(w_ref.shape[1] // tn):
        cols = slice(c * tn, (c + 1) * tn)
        o_ref[:, cols] = jnp.dot(u_ref[...], w_ref[:, cols],
                                 preferred_element_type=_F32).astype(o_ref.dtype)


def _norm_proj(x2d, g, w, *, tm, tn, out_dtype):
    T, D = x2d.shape
    N = w.shape[1]
    assert T % tm == 0 and N % tn == 0
    return pl.pallas_call(
        functools.partial(_norm_proj_kernel, tn=tn),
        grid=(T // tm,),
        in_specs=[pl.BlockSpec((tm, D), lambda i: (i, 0)),
                  pl.BlockSpec((1, D), lambda i: (0, 0)),
                  pl.BlockSpec(w.shape, lambda i: (0, 0), pipeline_mode=pl.Buffered(1))],
        out_specs=pl.BlockSpec((tm, N), lambda i: (i, 0)),
        out_shape=jax.ShapeDtypeStruct((T, N), out_dtype),
        scratch_shapes=[pltpu.VMEM((tm, D), _BF16)],
        compiler_params=_params(("parallel",)),
        name="norm_proj",
    )(x2d, g, w)


def _attn_kernel(q_ref, k_ref, v_ref, b_ref, o_ref, lse_ref, kpad, vpad, *, L, dil, heads, unroll):
    W = heads * HEAD_DIM
    zeros = jnp.zeros((_SIDE, W), _BF16)
    for r in range(dil):
        for pad, src in ((kpad, k_ref), (vpad, v_ref)):
            pad[r, 0:_SIDE, :] = zeros
            pad[r, _SIDE + L:2 * _SIDE + L, :] = zeros
            pad[r, _SIDE:_SIDE + L, :] = src[r]

    scale = HEAD_DIM ** -0.5
    lane = lax.broadcasted_iota(jnp.int32, (_QBLK, _LANES), 1)
    kidx = lax.broadcasted_iota(jnp.int32, (1, _KWIN), 1)
    nblk = L // _QBLK
    assert nblk & (nblk - 1) == 0
    blk_shift = nblk.bit_length() - 1

    def body(n, carry):
        r = lax.shift_right_logical(n, blk_shift)
        qs = pl.multiple_of((n & (nblk - 1)) * _QBLK, _QBLK)
        kpos = kidx + (qs - _SIDE)
        kvalid = jnp.logical_and(kpos >= 0, kpos < L)
        lse_tile = jnp.zeros((_QBLK, _LANES), _F32)
        for h in range(heads):
            cols = _lane_block(h)
            q = q_ref[r, pl.ds(qs, _QBLK), cols]
            kw = kpad[r, pl.ds(qs, _KWIN), cols]
            vw = vpad[r, pl.ds(qs, _KWIN), cols]
            s = lax.dot_general(q, kw, (((1,), (1,)), ((), ())),
                                preferred_element_type=_F32)
            s = s * scale + b_ref[h]
            s = jnp.where(kvalid, s, NEG_INF)
            m = jnp.max(s, axis=-1, keepdims=True)
            p = jnp.exp(s - m)
            l = jnp.sum(p, axis=-1, keepdims=True)
            o = jnp.dot(p.astype(_BF16), vw, preferred_element_type=_F32) / l
            o_ref[r, pl.ds(qs, _QBLK), cols] = o
            lse_tile = jnp.where(lane == h, m + jnp.log(l), lse_tile)
        lse_ref[r, pl.ds(qs, _QBLK), :] = lse_tile
        return carry

    lax.fori_loop(0, dil * nblk, body, 0, unroll=unroll)


def _attn_heads_per_step(S):
    return HEADS_PER_GROUP if S <= 2048 else HEADS_PER_GROUP // 2


def _attn_group(zg, bias, group, dil, B, S):
    L = S // dil
    assert L % _QBLK == 0
    heads = _attn_heads_per_step(S)
    steps = HEADS_PER_GROUP // heads
    W = heads * HEAD_DIM
    zg = zg.reshape(B, dil, L, _GROUP_QKV_W)
    col = lambda part: (lambda b, hp: (b, 0, 0, part * steps + hp))
    return pl.pallas_call(
        functools.partial(_attn_kernel, L=L, dil=dil, heads=heads,
                          unroll=min(8 // heads, dil * L // _QBLK)),
        grid=(B, steps),
        in_specs=[pl.BlockSpec((None, dil, L, W), col(0)),
                  pl.BlockSpec((None, dil, L, W), col(1)),
                  pl.BlockSpec((None, dil, L, W), col(2)),
                  pl.BlockSpec((heads, _QBLK, _KWIN),
                               lambda b, hp: (group * steps + hp, 0, 0))],
        out_specs=[pl.BlockSpec((None, dil, L, W), lambda b, hp: (b, 0, 0, hp)),
                   pl.BlockSpec((None, dil, L, _LANES), lambda b, hp: (b, 0, 0, hp))],
        out_shape=[jax.ShapeDtypeStruct((B, dil, L, ATTN_OUT_W), _F32),
                   jax.ShapeDtypeStruct((B, dil, L, steps * _LANES), _F32)],
        scratch_shapes=[pltpu.VMEM((dil, L + 2 * _SIDE, W), _BF16),
                        pltpu.VMEM((dil, L + 2 * _SIDE, W), _BF16)],
        compiler_params=_params(("parallel", "parallel")),
        name=f"dilated_attn_g{group}",
    )(zg, zg, zg, bias)


_CHUNKS = 2 * _SUBLANES
_PITCH_PAD = 4
_GATE_TAUS = 16
_SCAN_UNROLL = 8


def _sigmoid(x):
    return 0.5 * jnp.tanh(0.5 * x) + 0.5


def _chunk_rows(ref, start, pitch):
    return jnp.concatenate(
        [ref[pl.ds(start + half * _SUBLANES * pitch, _SUBLANES, stride=pitch), :]
         for half in range(_CHUNKS // _SUBLANES)], axis=0)


def _rglru_kernel(rx_ref, ry_ref, cw_ref, cb_ref, wg_ref, bg_ref, lam_ref, o_ref,
                  xs, a_f, h_f, a_b, h_b, *, S):
    BW = LRU_BLOCK_W
    Lc = S // _CHUNKS
    P = Lc + _PITCH_PAD
    left = CONV_W // 2

    for c in range(_CHUNKS):
        lo = c * Lc - left
        hi = lo + P
        s_lo, s_hi = max(lo, 0), min(hi, S)
        xs[c * P + (s_lo - lo):c * P + (s_hi - lo), :] = rx_ref[s_lo:s_hi, :]
        if lo < 0:
            xs[c * P:c * P - lo, :] = jnp.zeros((-lo, BW), _F32)
        if hi > S:
            xs[c * P + (S - lo):(c + 1) * P, :] = jnp.zeros((hi - S, BW), _F32)

    cw = cw_ref[...]
    cb = cb_ref[...]
    half_decay = (-0.5 * LRU_C) * jax.nn.softplus(-lam_ref[...])
    TB = _GATE_TAUS

    def gate_body(bi, carry):
        t0 = bi * TB
        taps = [_chunk_rows(xs, t0 + j, P) for j in range(TB + CONV_W - 1)]
        rows = []
        for tau in range(TB):
            xc = cb
            for t in range(CONV_W):
                xc = xc + taps[tau + t] * cw[t:t + 1]
            rows.append(xc)
        xc = jnp.concatenate(rows, axis=0)
        g = jnp.tanh(jnp.dot(xc.astype(_BF16), wg_ref[...], preferred_element_type=_F32)
                     + bg_ref[...])
        half_x = 0.5 * xc
        r0 = pl.multiple_of(t0 * _CHUNKS, TB * _CHUNKS)
        for d, (a_ref, u_ref) in enumerate(((a_f, h_f), (a_b, h_b))):
            log_a = half_decay[d:d + 1] * g[:, (2 * d) * BW:(2 * d + 1) * BW] + half_decay[d:d + 1]
            ix = (g[:, (2 * d + 1) * BW:(2 * d + 2) * BW] + 1.0) * half_x
            a = jnp.exp(log_a)
            om = jnp.tanh(log_a) * (-1.0 - a * a)
            root = jnp.where(om > 0.0, om * lax.rsqrt(om), 0.0)
            a_ref[pl.ds(r0, TB * _CHUNKS), :] = a
            u_ref[pl.ds(r0, TB * _CHUNKS), :] = root * ix
        return carry

    lax.fori_loop(0, Lc // TB, gate_body, 0, unroll=2)

    def scan_body(tau, carry):
        hf, pf, hb, pb = carry
        rf = pl.multiple_of(tau * _CHUNKS, _CHUNKS)
        rb = pl.multiple_of((Lc - 1 - tau) * _CHUNKS, _CHUNKS)
        af = a_f[pl.ds(rf, _CHUNKS), :]
        hf = af * hf + h_f[pl.ds(rf, _CHUNKS), :]
        pf = af * pf
        h_f[pl.ds(rf, _CHUNKS), :] = hf
        a_f[pl.ds(rf, _CHUNKS), :] = pf
        ab = a_b[pl.ds(rb, _CHUNKS), :]
        hb = ab * hb + h_b[pl.ds(rb, _CHUNKS), :]
        pb = ab * pb
        h_b[pl.ds(rb, _CHUNKS), :] = hb
        a_b[pl.ds(rb, _CHUNKS), :] = pb
        return hf, pf, hb, pb

    zero = jnp.zeros((_CHUNKS, BW), _F32)
    one = jnp.ones((_CHUNKS, BW), _F32)
    h_end, p_end, h_beg, p_beg = lax.fori_loop(0, Lc, scan_body, (zero, one, zero, one),
                                               unroll=_SCAN_UNROLL)

    row = lax.broadcasted_iota(jnp.int32, (_CHUNKS, BW), 0)
    in_f, in_b = zero, zero
    prev_f = jnp.zeros((1, BW), _F32)
    prev_b = jnp.zeros((1, BW), _F32)
    for c in range(1, _CHUNKS):
        prev_f = h_end[c - 1:c] + p_end[c - 1:c] * prev_f
        in_f = jnp.where(row == c, prev_f, in_f)
        cb_ = _CHUNKS - 1 - c
        prev_b = h_beg[cb_ + 1:cb_ + 2] + p_beg[cb_ + 1:cb_ + 2] * prev_b
        in_b = jnp.where(row == cb_, prev_b, in_b)

    def fix_body(tau, carry):
        r = pl.multiple_of(tau * _CHUNKS, _CHUNKS)
        h = (h_f[pl.ds(r, _CHUNKS), :] + a_f[pl.ds(r, _CHUNKS), :] * in_f
             + h_b[pl.ds(r, _CHUNKS), :] + a_b[pl.ds(r, _CHUNKS), :] * in_b)
        for half in range(_CHUNKS // _SUBLANES):
            xs[pl.ds(tau + half * _SUBLANES * P, _SUBLANES, stride=P), :] = \
                h[half * _SUBLANES:(half + 1) * _SUBLANES]
        return carry

    lax.fori_loop(0, Lc, fix_body, 0, unroll=_SCAN_UNROLL)

    for c in range(_CHUNKS):
        t = slice(c * Lc, (c + 1) * Lc)
        o_ref[t, :] = (xs[c * P:c * P + Lc, :] * jax.nn.gelu(ry_ref[t, :])).astype(o_ref.dtype)


def _rglru(zr, rx_col, ry_col, conv_w, conv_b, wg, bg, lam):
    B, S, _ = zr.shape
    BW = LRU_BLOCK_W
    Lc = S // _CHUNKS
    assert S % (_CHUNKS * _GATE_TAUS) == 0 and Lc % _SCAN_UNROLL == 0
    assert rx_col % BW == 0 and ry_col % BW == 0
    seq = pltpu.VMEM((S, BW), _F32)
    out = pl.pallas_call(
        functools.partial(_rglru_kernel, S=S),
        grid=(B, LRU_BLOCKS),
        in_specs=[pl.BlockSpec((None, S, BW), lambda b, n: (b, 0, rx_col // BW + n)),
                  pl.BlockSpec((None, S, BW), lambda b, n: (b, 0, ry_col // BW + n)),
                  pl.BlockSpec((CONV_W, BW), lambda b, n: (0, n)),
                  pl.BlockSpec((1, BW), lambda b, n: (0, n)),
                  pl.BlockSpec((None, BW, 4 * BW), lambda b, n: (n, 0, 0)),
                  pl.BlockSpec((None, 1, 4 * BW), lambda b, n: (n, 0, 0)),
                  pl.BlockSpec((2, BW), lambda b, n: (0, n))],
        out_specs=pl.BlockSpec((None, S, BW), lambda b, n: (b, 0, n)),
        out_shape=jax.ShapeDtypeStruct((B, S, LRU_W), _BF16),
        scratch_shapes=[pltpu.VMEM((_CHUNKS * (Lc + _PITCH_PAD), BW), _F32), seq, seq, seq, seq],
        compiler_params=_params(("parallel", "parallel")),
        name="conv_rglru",
    )(zr, zr, conv_w, conv_b, wg, bg, lam)
    return out.reshape(B * S, LRU_W)


def _unfold(src_ref, dst_ref, dil):
    rows = src_ref.shape[1]
    for r in range(dil):
        for c in range(dst_ref.shape[0]):
            dst_ref[c, pl.ds(r, rows, stride=dil), :] = src_ref[r, :, _lane_block(c)]


def _merge_kernel(*refs, heads):
    o_refs = refs[0:N_GROUPS]
    l_refs = refs[N_GROUPS:2 * N_GROUPS]
    rnn_ref, ga_ref, gr_ref, x_ref, wa_ref, wr_ref, wo_ref, out_ref = refs[2 * N_GROUPS:2 * N_GROUPS + 8]
    scratch = refs[2 * N_GROUPS + 8:]

    o_get, l_get, k = [], [], 0
    for g, (_, dil) in enumerate(DILATED_GROUPS):
        if dil == 1:
            o_get.append(lambda c, ref=o_refs[g]: ref[:, _lane_block(c)])
            l_get.append(lambda c, ref=l_refs[g]: ref[:, _lane_block(c)])
        else:
            o_scr, l_scr = scratch[k], scratch[k + 1]
            k += 2
            _unfold(o_refs[g], o_scr, dil)
            _unfold(l_refs[g], l_scr, dil)
            o_get.append(lambda c, ref=o_scr: ref[c])
            l_get.append(lambda c, ref=l_scr: ref[c])

    wts = []
    for step in range(HEADS_PER_GROUP // heads):
        ls = [get(step) for get in l_get]
        mx = functools.reduce(jnp.maximum, ls)
        es = [jnp.exp(l - mx) for l in ls]
        inv = 1.0 / functools.reduce(jnp.add, es)
        wts.append([e * inv for e in es])
    attn = []
    for h in range(HEADS_PER_GROUP):
        w = wts[h // heads]
        c = h % heads
        attn.append(functools.reduce(
            jnp.add, [w[g][:, c:c + 1] * o_get[g](h) for g in range(N_GROUPS)]))
    attn = jnp.concatenate(attn, axis=-1).astype(_BF16)
    pa = jnp.dot(attn, wa_ref[...], preferred_element_type=_F32)
    pr = jnp.dot(rnn_ref[...], wr_ref[...], preferred_element_type=_F32)
    merged = _sigmoid(ga_ref[...]) * pa + _sigmoid(gr_ref[...]) * pr
    out_ref[...] = x_ref[...] + jnp.dot(merged.astype(_BF16), wo_ref[...],
                                        preferred_element_type=_F32)


def _merge(os_, lses, rnn, zr, ga_col, gr_col, x2d, wa, wr, wo, B, S, *, tm):
    T, D = x2d.shape
    assert S % tm == 0 and ga_col % D == 0 and gr_col % D == 0
    heads = _attn_heads_per_step(S)
    LW = (HEADS_PER_GROUP // heads) * _LANES
    tiles_per_seq = S // tm
    row = lambda w: pl.BlockSpec((tm, w), lambda i: (i, 0))
    const = lambda a: pl.BlockSpec(a.shape, lambda i: (0, 0), pipeline_mode=pl.Buffered(1))

    def group_spec(dil, w):
        if dil == 1:
            return row(w)
        assert tm % (dil * _SUBLANES) == 0
        return pl.BlockSpec((None, dil, tm // dil, w),
                            lambda i: (i // tiles_per_seq, 0, i % tiles_per_seq, 0))

    scratch = []
    for _, dil in DILATED_GROUPS:
        if dil > 1:
            scratch += [pltpu.VMEM((ATTN_OUT_W // _LANES, tm, _LANES), _F32),
                        pltpu.VMEM((LW // _LANES, tm, _LANES), _F32)]
    os_ = [o.reshape(T, ATTN_OUT_W) if dil == 1 else o for o, (_, dil) in zip(os_, DILATED_GROUPS)]
    lses = [l.reshape(T, LW) if dil == 1 else l for l, (_, dil) in zip(lses, DILATED_GROUPS)]
    return pl.pallas_call(
        functools.partial(_merge_kernel, heads=heads),
        grid=(T // tm,),
        in_specs=[group_spec(dil, ATTN_OUT_W) for _, dil in DILATED_GROUPS]
        + [group_spec(dil, LW) for _, dil in DILATED_GROUPS]
        + [row(LRU_W),
           pl.BlockSpec((tm, D), lambda i: (i, ga_col // D)),
           pl.BlockSpec((tm, D), lambda i: (i, gr_col // D)),
           row(D), const(wa), const(wr), const(wo)],
        out_specs=row(D),
        out_shape=jax.ShapeDtypeStruct((T, D), _F32),
        scratch_shapes=scratch,
        compiler_params=_params(("parallel",)),
        name="merge_proj",
    )(*os_, *lses, rnn, zr, zr, x2d, wa, wr, wo)


def _mlp_kernel(x_ref, g_ref, w1_ref, w2_ref, gf_ref, o_ref, u_ref):
    j = pl.program_id(1)

    @pl.when(j == 0)
    def _():
        x = x_ref[...]
        u_ref[...] = _rms(x, g_ref[...]).astype(u_ref.dtype)
        o_ref[...] = x

    h = jnp.dot(u_ref[...], w1_ref[...], preferred_element_type=_F32)
    h = jnp.square(jnp.maximum(h, 0.0)).astype(_BF16)
    o_ref[...] += jnp.dot(h, w2_ref[...], preferred_element_type=_F32)

    @pl.when(j == pl.num_programs(1) - 1)
    def _():
        o_ref[...] = _rms(o_ref[...], gf_ref[...])


def _mlp(x2d, g, w1, w2, gf, *, tm, tf):
    T, D = x2d.shape
    F = w1.shape[1]
    assert T % tm == 0 and F % tf == 0
    return pl.pallas_call(
        _mlp_kernel,
        grid=(T // tm, F // tf),
        in_specs=[pl.BlockSpec((tm, D), lambda i, j: (i, 0)),
                  pl.BlockSpec((1, D), lambda i, j: (0, 0)),
                  pl.BlockSpec((D, tf), lambda i, j: (0, j)),
                  pl.BlockSpec((tf, D), lambda i, j: (j, 0)),
                  pl.BlockSpec((1, D), lambda i, j: (0, 0))],
        out_specs=pl.BlockSpec((tm, D), lambda i, j: (i, 0)),
        out_shape=jax.ShapeDtypeStruct((T, D), _F32),
        scratch_shapes=[pltpu.VMEM((tm, D), _BF16)],
        compiler_params=_params(("parallel", "arbitrary")),
        name="mlp_final_norm",
    )(x2d, g, w1, w2, gf)


_REST_GA = 0
_REST_GR = D_MODEL
_REST_RX = 2 * D_MODEL
_REST_RY = 2 * D_MODEL + LRU_W


def _layer(x, bias_tabs, p):
    B, S, D = x.shape
    T = B * S
    x2d = x.reshape(T, D)
    zgs = _qkv_proj(x2d, p["g_mix"], p["w_qkv"], B, S, tm=512)
    zr = _norm_proj(x2d, p["g_mix"], p["w_rest"], tm=256, tn=1792, out_dtype=_F32)
    os_, lses = [], []
    for g, (_, dil) in enumerate(DILATED_GROUPS):
        o, lse = _attn_group(zgs[g], bias_tabs, g, dil, B, S)
        os_.append(o)
        lses.append(lse)
    rnn = _rglru(zr.reshape(B, S, -1), _REST_RX, _REST_RY, p["conv_w"], p["conv_b"],
                 p["w_gate"], p["b_gate"], p["lam"])
    x2 = _merge(os_, lses, rnn, zr, _REST_GA, _REST_GR, x2d,
                p["w_attn_o"], p["w_rnn_o"], p["w_out"], B, S, tm=256)
    y = _mlp(x2, p["g_mlp"], p["w_mlp_in"], p["w_mlp_out"], p["g_final"], tm=1024, tf=512)
    return y.reshape(B, S, D)


def kernel(x_prompt, x_sample, rel_bias, norm_mix_g, w_in, conv_w, conv_b, lru_wa, lru_ba,
           lru_wx, lru_bx, lru_lambda, w_attn_o, w_rnn_o, w_out, norm_mlp_g, w_mlp_in,
           w_mlp_out, norm_final_g):
    depth = w_in.shape[0]
    assert depth == 1, "the final norm is fused into the single layer's MLP kernel"
    bias_tabs = _bias_tables(rel_bias)
    q_end, rx_end, ry_end, ga_end = 3 * ATTN_W, 3 * ATTN_W + LRU_W, 3 * ATTN_W + 2 * LRU_W, \
        3 * ATTN_W + 2 * LRU_W + D_MODEL
    wl = w_in[0]
    w_qkv = jnp.concatenate(
        [wl[:, part * ATTN_W + g * ATTN_OUT_W:part * ATTN_W + (g + 1) * ATTN_OUT_W]
         for g in range(N_GROUPS) for part in range(3)], axis=-1).astype(_BF16)
    w_gate = (0.5 * jnp.concatenate([lru_wa[0, 0], lru_wx[0, 0], lru_wa[0, 1], lru_wx[0, 1]],
                                    axis=-1)).astype(_BF16)
    b_gate = jnp.concatenate(
        [b.reshape(LRU_BLOCKS, 1, LRU_BLOCK_W)
         for b in (lru_ba[0, 0], lru_bx[0, 0], lru_ba[0, 1], lru_bx[0, 1])], axis=-1)
    p = dict(
        g_mix=norm_mix_g[0].reshape(1, D_MODEL),
        w_qkv=w_qkv,
        w_rest=jnp.concatenate([wl[:, ry_end:ga_end], wl[:, ga_end:], wl[:, q_end:rx_end],
                                wl[:, rx_end:ry_end]], axis=-1).astype(_BF16),
        conv_w=conv_w[0], conv_b=conv_b[0].reshape(1, LRU_W),
        w_gate=w_gate, b_gate=0.5 * b_gate.astype(_F32), lam=lru_lambda[0],
        w_attn_o=w_attn_o[0].astype(_BF16), w_rnn_o=w_rnn_o[0].astype(_BF16),
        w_out=w_out[0].astype(_BF16),
        g_mlp=norm_mlp_g[0].reshape(1, D_MODEL),
        w_mlp_in=w_mlp_in[0].astype(_BF16), w_mlp_out=w_mlp_out[0].astype(_BF16),
        g_final=norm_final_g.reshape(1, D_MODEL),
    )
    return (_layer(x_prompt, bias_tabs, p), _layer(x_sample, bias_tabs, p))
```

```python
import functools
import math

import jax
import jax.numpy as jnp
import numpy as np
from jax import lax
from jax.experimental import pallas as pl
from jax.experimental.pallas import tpu as pltpu

D_MODEL = 2048
HEAD_DIM = 128
HEADS_PER_GROUP = 4
DILATED_GROUPS = ((128, 1), (512, 4), (2048, 16))
N_GROUPS = len(DILATED_GROUPS)
ATTN_W = N_GROUPS * HEADS_PER_GROUP * HEAD_DIM
ATTN_OUT_W = HEADS_PER_GROUP * HEAD_DIM
NUM_BUCKETS = 32
REL_MAX_DIST = 1024
LRU_W = 1536
LRU_BLOCKS = 12
LRU_BLOCK_W = LRU_W // LRU_BLOCKS
LRU_C = 8.0
CONV_W = 4
D_FF = 4 * D_MODEL
NORM_EPS = 1e-6
NEG_INF = -1e30

_F32 = jnp.float32
_BF16 = jnp.bfloat16
_V7X_VMEM_LIMIT_BYTES = 56 * 1024 * 1024
_LANES = 128
_SUBLANES = 8

_SIDE = DILATED_GROUPS[0][0] // (2 * DILATED_GROUPS[0][1])
assert all(w // (2 * d) == _SIDE for w, d in DILATED_GROUPS)
_QBLK = 2 * _SIDE
_KWIN = 4 * _SIDE
assert _QBLK == _LANES
_GROUP_QKV_W = 3 * ATTN_OUT_W


def _params(semantics):
    return pltpu.CompilerParams(dimension_semantics=semantics,
                                vmem_limit_bytes=_V7X_VMEM_LIMIT_BYTES)


def _rms(x, g):
    ms = jnp.mean(x * x, axis=-1, keepdims=True)
    return x * lax.rsqrt(ms + NORM_EPS) * g


def _lane_block(c):
    return slice(c * _LANES, (c + 1) * _LANES)


_LOG2E = math.log2(math.e)
_EDGE_VARIANTS = 4


def _rel_bucket_np(rel):
    half = NUM_BUCKETS // 2
    max_exact = half // 2
    n = np.abs(rel)
    nf = np.maximum(n, 1).astype(np.float32)
    ratio = np.log(nf / np.float32(max_exact)) / np.float32(math.log(REL_MAX_DIST / max_exact))
    large = max_exact + (ratio * np.float32(half - max_exact)).astype(np.int32)
    large = np.minimum(large, half - 1)
    return (np.where(rel > 0, half, 0) + np.where(n < max_exact, n, large)).astype(np.int32)


def _bias_kernel(tab_ref, idx_ref, o_ref):
    head = pl.program_id(0)
    idx = idx_ref[...]
    acc = jnp.zeros(idx.shape, _F32)
    for b in range(NUM_BUCKETS):
        acc = jnp.where(idx == b, tab_ref[b, head], acc)
    acc = acc * _LOG2E
    kj = lax.broadcasted_iota(jnp.int32, idx.shape, 1)
    in_band = idx >= 0
    after_start = kj >= _SIDE
    before_end = kj < _KWIN - _SIDE
    o_ref[0] = jnp.where(in_band, acc, NEG_INF)
    o_ref[1] = jnp.where(jnp.logical_and(in_band, after_start), acc, NEG_INF)
    o_ref[2] = jnp.where(jnp.logical_and(in_band, before_end), acc, NEG_INF)
    o_ref[3] = jnp.where(jnp.logical_and(in_band, jnp.logical_and(after_start, before_end)),
                         acc, NEG_INF)


def _bias_tables(rel_bias):
    qi = np.arange(_QBLK, dtype=np.int32)[:, None]
    kj = np.arange(_KWIN, dtype=np.int32)[None, :]
    rel = kj - _SIDE - qi
    band = np.abs(rel) <= _SIDE
    idx = np.stack([np.where(band, _rel_bucket_np(rel * dil), -1) for _, dil in DILATED_GROUPS])
    n_heads = N_GROUPS * HEADS_PER_GROUP
    return pl.pallas_call(
        _bias_kernel,
        grid=(n_heads,),
        in_specs=[pl.BlockSpec(memory_space=pltpu.SMEM),
                  pl.BlockSpec((None, _QBLK, _KWIN), lambda h: (h // HEADS_PER_GROUP, 0, 0))],
        out_specs=pl.BlockSpec((_EDGE_VARIANTS, None, _QBLK, _KWIN), lambda h: (0, h, 0, 0)),
        out_shape=jax.ShapeDtypeStruct((_EDGE_VARIANTS, n_heads, _QBLK, _KWIN), _F32),
        name="bias_tables",
    )(rel_bias.astype(_F32), jnp.asarray(idx.astype(np.int32)))


_Q_SCALE = HEAD_DIM ** -0.5 * _LOG2E


def _qkv_proj_kernel(x_ref, g_ref, w_ref, *refs):
    out_refs = refs[:N_GROUPS]
    u_ref, stage_ref = refs[N_GROUPS:]
    tm = x_ref.shape[0]
    n_slabs = _GROUP_QKV_W // _LANES
    u_ref[...] = _rms(x_ref[...], g_ref[...]).astype(u_ref.dtype)
    for g, (_, dil) in enumerate(DILATED_GROUPS):
        o_ref = out_refs[g]
        res = jnp.dot(u_ref[...], w_ref[:, g * _GROUP_QKV_W:(g + 1) * _GROUP_QKV_W],
                      preferred_element_type=_F32)
        if dil == 1:
            o_ref[:, :ATTN_OUT_W] = (res[:, :ATTN_OUT_W] * _Q_SCALE).astype(o_ref.dtype)
            o_ref[:, ATTN_OUT_W:] = res[:, ATTN_OUT_W:].astype(o_ref.dtype)
            continue
        for c in range(n_slabs):
            slab = res[:, _lane_block(c)]
            stage_ref[c] = slab * _Q_SCALE if c < ATTN_OUT_W // _LANES else slab
        rows = tm // dil
        for r in range(dil):
            for c in range(n_slabs):
                o_ref[r, :, _lane_block(c)] = stage_ref[
                    c, pl.ds(r, rows, stride=dil), :].astype(o_ref.dtype)


def _qkv_proj(x2d, g, w, B, S, *, tm):
    T, D = x2d.shape
    assert T == B * S and S % tm == 0
    tiles_per_seq = S // tm
    out_shape, out_specs = [], []
    for _, dil in DILATED_GROUPS:
        assert tm % (dil * 16) == 0
        if dil == 1:
            out_shape.append(jax.ShapeDtypeStruct((T, _GROUP_QKV_W), _BF16))
            out_specs.append(pl.BlockSpec((tm, _GROUP_QKV_W), lambda i: (i, 0)))
        else:
            out_shape.append(jax.ShapeDtypeStruct((B, dil, S // dil, _GROUP_QKV_W), _BF16))
            out_specs.append(pl.BlockSpec(
                (None, dil, tm // dil, _GROUP_QKV_W),
                lambda i: (i // tiles_per_seq, 0, i % tiles_per_seq, 0)))
    return pl.pallas_call(
        _qkv_proj_kernel,
        grid=(T // tm,),
        in_specs=[pl.BlockSpec((tm, D), lambda i: (i, 0)),
                  pl.BlockSpec((1, D), lambda i: (0, 0)),
                  pl.BlockSpec(w.shape, lambda i: (0, 0), pipeline_mode=pl.Buffered(1))],
        out_specs=out_specs,
        out_shape=out_shape,
        scratch_shapes=[pltpu.VMEM((tm, D), _BF16),
                        pltpu.VMEM((_GROUP_QKV_W // _LANES, tm, _LANES), _F32)],
        compiler_params=_params(("parallel",)),
        name="qkv_proj",
    )(x2d, g, w)


def _norm_proj_kernel(x_ref, g_ref, w_ref, o_ref, u_ref, *, tn):
    u_ref[...] = _rms(x_ref[...], g_ref[...]).astype(u_ref.dtype)
    for c in range(w_ref.shape[1] // tn):
        cols = slice(c * tn, (c + 1) * tn)
        o_ref[:, cols] = jnp.dot(u_ref[...], w_ref[:, cols],
                                 preferred_element_type=_F32).astype(o_ref.dtype)


def _norm_proj(x2d, g, w, *, tm, tn, out_dtype):
    T, D = x2d.shape
    N = w.shape[1]
    assert T % tm == 0 and N % tn == 0
    return pl.pallas_call(
        functools.partial(_norm_proj_kernel, tn=tn),
        grid=(T // tm,),
        in_specs=[pl.BlockSpec((tm, D), lambda i: (i, 0)),
                  pl.BlockSpec((1, D), lambda i: (0, 0)),
                  pl.BlockSpec(w.shape, lambda i: (0, 0), pipeline_mode=pl.Buffered(1))],
        out_specs=pl.BlockSpec((tm, N), lambda i: (i, 0)),
        out_shape=jax.ShapeDtypeStruct((T, N), out_dtype),
        scratch_shapes=[pltpu.VMEM((tm, D), _BF16)],
        compiler_params=_params(("parallel",)),
        name="norm_proj",
    )(x2d, g, w)


_ML_SHIFT = _LANES // 2


def _attn_kernel(q_ref, k_ref, v_ref, b_ref, o_ref, ml_ref, kpad, vpad, *, L, dil, heads, unroll):
    W = heads * HEAD_DIM
    zeros = jnp.zeros((_SIDE, W), _BF16)
    for r in range(dil):
        for pad, src in ((kpad, k_ref), (vpad, v_ref)):
            pad[r, 0:_SIDE, :] = zeros
            pad[r, _SIDE + L:2 * _SIDE + L, :] = zeros
            pad[r, _SIDE:_SIDE + L, :] = src[r]

    lane = lax.broadcasted_iota(jnp.int32, (_QBLK, _LANES), 1)
    nblk = L // _QBLK
    assert nblk & (nblk - 1) == 0
    blk_shift = nblk.bit_length() - 1

    def body(n, carry):
        r = lax.shift_right_logical(n, blk_shift)
        blk = n & (nblk - 1)
        qs = pl.multiple_of(blk * _QBLK, _QBLK)
        edge = (blk == 0).astype(jnp.int32) + 2 * (blk == nblk - 1).astype(jnp.int32)
        ml_tile = jnp.zeros((_QBLK, _LANES), _F32)
        for h in range(heads):
            cols = _lane_block(h)
            q = q_ref[r, pl.ds(qs, _QBLK), cols]
            kw = kpad[r, pl.ds(qs, _KWIN), cols]
            vw = vpad[r, pl.ds(qs, _KWIN), cols]
            s = lax.dot_general(q, kw, (((1,), (1,)), ((), ())),
                                preferred_element_type=_F32) + b_ref[edge, h]
            m = jnp.max(s, axis=-1, keepdims=True)
            p = jnp.exp2(s - m)
            l = jnp.sum(p, axis=-1, keepdims=True)
            o_ref[r, pl.ds(qs, _QBLK), cols] = jnp.dot(p.astype(_BF16), vw,
                                                       preferred_element_type=_F32)
            ml_tile = jnp.where(lane == h, m, jnp.where(lane == _ML_SHIFT + h, l, ml_tile))
        ml_ref[r, pl.ds(qs, _QBLK), :] = ml_tile
        return carry

    lax.fori_loop(0, dil * nblk, body, 0, unroll=unroll)


def _attn_heads_per_step(S):
    return HEADS_PER_GROUP if S <= 2048 else HEADS_PER_GROUP // 2


def _attn_group(zg, bias, group, dil, B, S):
    L = S // dil
    assert L % _QBLK == 0
    heads = _attn_heads_per_step(S)
    steps = HEADS_PER_GROUP // heads
    W = heads * HEAD_DIM
    zg = zg.reshape(B, dil, L, _GROUP_QKV_W)
    col = lambda part: (lambda b, hp: (b, 0, 0, part * steps + hp))
    return pl.pallas_call(
        functools.partial(_attn_kernel, L=L, dil=dil, heads=heads,
                          unroll=min(64 // heads, dil * L // _QBLK)),
        grid=(B, steps),
        in_specs=[pl.BlockSpec((None, dil, L, W), col(0)),
                  pl.BlockSpec((None, dil, L, W), col(1)),
                  pl.BlockSpec((None, dil, L, W), col(2)),
                  pl.BlockSpec((_EDGE_VARIANTS, heads, _QBLK, _KWIN),
                               lambda b, hp: (0, group * steps + hp, 0, 0))],
        out_specs=[pl.BlockSpec((None, dil, L, W), lambda b, hp: (b, 0, 0, hp)),
                   pl.BlockSpec((None, dil, L, _LANES), lambda b, hp: (b, 0, 0, hp))],
        out_shape=[jax.ShapeDtypeStruct((B, dil, L, ATTN_OUT_W), _F32),
                   jax.ShapeDtypeStruct((B, dil, L, steps * _LANES), _F32)],
        scratch_shapes=[pltpu.VMEM((dil, L + 2 * _SIDE, W), _BF16),
                        pltpu.VMEM((dil, L + 2 * _SIDE, W), _BF16)],
        compiler_params=_params(("parallel", "parallel")),
        name=f"dilated_attn_g{group}",
    )(zg, zg, zg, bias)


_CHUNKS = 4 * _SUBLANES
_PITCH_PAD = 4
_GATE_TAUS = 8
_SCAN_UNROLL = 16


def _sigmoid(x):
    return 0.5 * jnp.tanh(0.5 * x) + 0.5


def _gelu_tanh(x):
    k = math.sqrt(2.0 / math.pi)
    half = 0.5 * x
    return half * jnp.tanh(x * (k + (k * 0.044715) * (x * x))) + half


def _chunk_rows(ref, start, pitch):
    return jnp.concatenate(
        [ref[pl.ds(start + half * _SUBLANES * pitch, _SUBLANES, stride=pitch), :]
         for half in range(_CHUNKS // _SUBLANES)], axis=0)


def _rglru_kernel(rx_ref, ry_ref, cw_ref, cb_ref, wg_ref, bg_ref, lam_ref, o_ref,
                  xs, a_f, h_f, a_b, h_b, *, S):
    BW = LRU_BLOCK_W
    Lc = S // _CHUNKS
    P = Lc + _PITCH_PAD
    left = CONV_W // 2

    for c in range(_CHUNKS):
        lo = c * Lc - left
        hi = lo + P
        s_lo, s_hi = max(lo, 0), min(hi, S)
        xs[c * P + (s_lo - lo):c * P + (s_hi - lo), :] = rx_ref[s_lo:s_hi, :]
        if lo < 0:
            xs[c * P:c * P - lo, :] = jnp.zeros((-lo, BW), _F32)
        if hi > S:
            xs[c * P + (S - lo):(c + 1) * P, :] = jnp.zeros((hi - S, BW), _F32)

    cw = 0.5 * cw_ref[...]
    cb = 0.5 * cb_ref[...]
    half_decay = (-0.5 * LRU_C) * jax.nn.softplus(-lam_ref[...])
    TB = _GATE_TAUS

    def gate_body(bi, carry):
        t0 = bi * TB
        taps = [_chunk_rows(xs, t0 + j, P) for j in range(TB + CONV_W - 1)]
        rows = []
        for tau in range(TB):
            xc = cb
            for t in range(CONV_W):
                xc = xc + taps[tau + t] * cw[t:t + 1]
            rows.append(xc)
        half_x = jnp.concatenate(rows, axis=0)
        g = jnp.tanh(jnp.dot(half_x.astype(_BF16), wg_ref[...], preferred_element_type=_F32)
                     + bg_ref[...])
        r0 = pl.multiple_of(t0 * _CHUNKS, TB * _CHUNKS)
        for d, (a_ref, u_ref) in enumerate(((a_f, h_f), (a_b, h_b))):
            log_a = half_decay[d:d + 1] * g[:, (2 * d) * BW:(2 * d + 1) * BW] + half_decay[d:d + 1]
            ix = (g[:, (2 * d + 1) * BW:(2 * d + 2) * BW] + 1.0) * half_x
            a = jnp.exp(log_a)
            om = jnp.tanh(log_a) * (-1.0 - a * a)
            root = jnp.where(om > 0.0, om * lax.rsqrt(om), 0.0)
            a_ref[pl.ds(r0, TB * _CHUNKS), :] = a
            u_ref[pl.ds(r0, TB * _CHUNKS), :] = root * ix
        return carry

    lax.fori_loop(0, Lc // TB, gate_body, 0, unroll=4)

    def scan_body(tau, carry):
        hf, pf, hb, pb = carry
        rf = pl.multiple_of(tau * _CHUNKS, _CHUNKS)
        rb = pl.multiple_of((Lc - 1 - tau) * _CHUNKS, _CHUNKS)
        af = a_f[pl.ds(rf, _CHUNKS), :]
        hf = af * hf + h_f[pl.ds(rf, _CHUNKS), :]
        pf = af * pf
        h_f[pl.ds(rf, _CHUNKS), :] = hf
        a_f[pl.ds(rf, _CHUNKS), :] = pf
        ab = a_b[pl.ds(rb, _CHUNKS), :]
        hb = ab * hb + h_b[pl.ds(rb, _CHUNKS), :]
        pb = ab * pb
        h_b[pl.ds(rb, _CHUNKS), :] = hb
        a_b[pl.ds(rb, _CHUNKS), :] = pb
        return hf, pf, hb, pb

    zero = jnp.zeros((_CHUNKS, BW), _F32)
    one = jnp.ones((_CHUNKS, BW), _F32)
    h_end, p_end, h_beg, p_beg = lax.fori_loop(0, Lc, scan_body, (zero, one, zero, one),
                                               unroll=_SCAN_UNROLL)

    row = lax.broadcasted_iota(jnp.int32, (_CHUNKS, BW), 0)
    in_f, in_b = zero, zero
    prev_f = jnp.zeros((1, BW), _F32)
    prev_b = jnp.zeros((1, BW), _F32)
    for c in range(1, _CHUNKS):
        prev_f = h_end[c - 1:c] + p_end[c - 1:c] * prev_f
        in_f = jnp.where(row == c, prev_f, in_f)
        cb_ = _CHUNKS - 1 - c
        prev_b = h_beg[cb_ + 1:cb_ + 2] + p_beg[cb_ + 1:cb_ + 2] * prev_b
        in_b = jnp.where(row == cb_, prev_b, in_b)

    def fix_body(tau, carry):
        r = pl.multiple_of(tau * _CHUNKS, _CHUNKS)
        h = (h_f[pl.ds(r, _CHUNKS), :] + a_f[pl.ds(r, _CHUNKS), :] * in_f
             + h_b[pl.ds(r, _CHUNKS), :] + a_b[pl.ds(r, _CHUNKS), :] * in_b)
        for half in range(_CHUNKS // _SUBLANES):
            xs[pl.ds(tau + half * _SUBLANES * P, _SUBLANES, stride=P), :] = \
                h[half * _SUBLANES:(half + 1) * _SUBLANES]
        return carry

    lax.fori_loop(0, Lc, fix_body, 0, unroll=_SCAN_UNROLL)

    for c in range(_CHUNKS):
        t = slice(c * Lc, (c + 1) * Lc)
        o_ref[t, :] = (xs[c * P:c * P + Lc, :] * _gelu_tanh(ry_ref[t, :])).astype(o_ref.dtype)


def _rglru(zr, rx_col, ry_col, conv_w, conv_b, wg, bg, lam):
    B, S, _ = zr.shape
    BW = LRU_BLOCK_W
    Lc = S // _CHUNKS
    assert S % (_CHUNKS * _GATE_TAUS) == 0 and Lc % _SCAN_UNROLL == 0
    assert rx_col % BW == 0 and ry_col % BW == 0
    seq = pltpu.VMEM((S, BW), _F32)
    out = pl.pallas_call(
        functools.partial(_rglru_kernel, S=S),
        grid=(B, LRU_BLOCKS),
        in_specs=[pl.BlockSpec((None, S, BW), lambda b, n: (b, 0, rx_col // BW + n)),
                  pl.BlockSpec((None, S, BW), lambda b, n: (b, 0, ry_col // BW + n)),
                  pl.BlockSpec((CONV_W, BW), lambda b, n: (0, n)),
                  pl.BlockSpec((1, BW), lambda b, n: (0, n)),
                  pl.BlockSpec((None, BW, 4 * BW), lambda b, n: (n, 0, 0)),
                  pl.BlockSpec((None, 1, 4 * BW), lambda b, n: (n, 0, 0)),
                  pl.BlockSpec((2, BW), lambda b, n: (0, n))],
        out_specs=pl.BlockSpec((None, S, BW), lambda b, n: (b, 0, n)),
        out_shape=jax.ShapeDtypeStruct((B, S, LRU_W), _BF16),
        scratch_shapes=[pltpu.VMEM((_CHUNKS * (Lc + _PITCH_PAD), BW), _F32), seq, seq, seq, seq],
        compiler_params=_params(("parallel", "parallel")),
        name="conv_rglru",
    )(zr, zr, conv_w, conv_b, wg, bg, lam)
    return out.reshape(B * S, LRU_W)


def _unfold(src_ref, dst_ref, dil):
    rows = src_ref.shape[1]
    for r in range(dil):
        for c in range(dst_ref.shape[0]):
            dst_ref[c, pl.ds(r, rows, stride=dil), :] = src_ref[r, :, _lane_block(c)]


def _merge_kernel(*refs, heads):
    o_refs = refs[0:N_GROUPS]
    l_refs = refs[N_GROUPS:2 * N_GROUPS]
    rnn_ref, ga_ref, gr_ref, x_ref, wa_ref, wr_ref, wo_ref, out_ref = refs[2 * N_GROUPS:2 * N_GROUPS + 8]
    scratch = refs[2 * N_GROUPS + 8:]

    pr = jnp.dot(rnn_ref[...], wr_ref[...], preferred_element_type=_F32)

    o_get, l_get, k = [], [], 0
    for g, (_, dil) in enumerate(DILATED_GROUPS):
        if dil == 1:
            o_get.append(lambda c, ref=o_refs[g]: ref[:, _lane_block(c)])
            l_get.append(lambda c, ref=l_refs[g]: ref[:, _lane_block(c)])
        else:
            o_scr, l_scr = scratch[k], scratch[k + 1]
            k += 2
            _unfold(o_refs[g], o_scr, dil)
            _unfold(l_refs[g], l_scr, dil)
            o_get.append(lambda c, ref=o_scr: ref[c])
            l_get.append(lambda c, ref=l_scr: ref[c])

    wts = []
    for step in range(HEADS_PER_GROUP // heads):
        ml = [get(step) for get in l_get]
        mx = functools.reduce(jnp.maximum, ml)
        es = [jnp.exp2(t - mx) for t in ml]
        den = functools.reduce(jnp.add, [e * pltpu.roll(t, _ML_SHIFT, axis=1)
                                         for e, t in zip(es, ml)])
        inv = 1.0 / den
        wts.append([e * inv for e in es])
    attn = []
    for h in range(HEADS_PER_GROUP):
        w = wts[h // heads]
        c = h % heads
        attn.append(functools.reduce(
            jnp.add, [w[g][:, c:c + 1] * o_get[g](h) for g in range(N_GROUPS)]))
    attn = jnp.concatenate(attn, axis=-1).astype(_BF16)
    pa = jnp.dot(attn, wa_ref[...], preferred_element_type=_F32)
    merged = _sigmoid(ga_ref[...]) * pa + _sigmoid(gr_ref[...]) * pr
    out_ref[...] = x_ref[...] + jnp.dot(merged.astype(_BF16), wo_ref[...],
                                        preferred_element_type=_F32)


def _merge(os_, lses, rnn, zr, ga_col, gr_col, x2d, wa, wr, wo, B, S, *, tm):
    T, D = x2d.shape
    assert S % tm == 0 and ga_col % D == 0 and gr_col % D == 0
    heads = _attn_heads_per_step(S)
    LW = (HEADS_PER_GROUP // heads) * _LANES
    tiles_per_seq = S // tm
    row = lambda w: pl.BlockSpec((tm, w), lambda i: (i, 0))
    const = lambda a: pl.BlockSpec(a.shape, lambda i: (0, 0), pipeline_mode=pl.Buffered(1))

    def group_spec(dil, w):
        if dil == 1:
            return row(w)
        assert tm % (dil * _SUBLANES) == 0
        return pl.BlockSpec((None, dil, tm // dil, w),
                            lambda i: (i // tiles_per_seq, 0, i % tiles_per_seq, 0))

    scratch = []
    for _, dil in DILATED_GROUPS:
        if dil > 1:
            scratch += [pltpu.VMEM((ATTN_OUT_W // _LANES, tm, _LANES), _F32),
                        pltpu.VMEM((LW // _LANES, tm, _LANES), _F32)]
    os_ = [o.reshape(T, ATTN_OUT_W) if dil == 1 else o for o, (_, dil) in zip(os_, DILATED_GROUPS)]
    lses = [l.reshape(T, LW) if dil == 1 else l for l, (_, dil) in zip(lses, DILATED_GROUPS)]
    return pl.pallas_call(
        functools.partial(_merge_kernel, heads=heads),
        grid=(T // tm,),
        in_specs=[group_spec(dil, ATTN_OUT_W) for _, dil in DILATED_GROUPS]
        + [group_spec(dil, LW) for _, dil in DILATED_GROUPS]
        + [row(LRU_W),
           pl.BlockSpec((tm, D), lambda i: (i, ga_col // D)),
           pl.BlockSpec((tm, D), lambda i: (i, gr_col // D)),
           row(D), const(wa), const(wr), const(wo)],
        out_specs=row(D),
        out_shape=jax.ShapeDtypeStruct((T, D), _F32),
        scratch_shapes=scratch,
        compiler_params=_params(("parallel",)),
        name="merge_proj",
    )(*os_, *lses, rnn, zr, zr, x2d, wa, wr, wo)


def _mlp_kernel(x_ref, g_ref, w1_ref, w2_ref, gf_ref, o_ref, u_ref):
    j = pl.program_id(1)

    @pl.when(j == 0)
    def _():
        x = x_ref[...]
        u_ref[...] = _rms(x, g_ref[...]).astype(u_ref.dtype)
        o_ref[...] = x

    h = jnp.dot(u_ref[...], w1_ref[...], preferred_element_type=_F32)
    h = jnp.square(jnp.maximum(h, 0.0)).astype(_BF16)
    o_ref[...] += jnp.dot(h, w2_ref[...], preferred_element_type=_F32)

    @pl.when(j == pl.num_programs(1) - 1)
    def _():
        o_ref[...] = _rms(o_ref[...], gf_ref[...])


def _mlp(x2d, g, w1, w2, gf, *, tm, tf):
    T, D = x2d.shape
    F = w1.shape[1]
    assert T % tm == 0 and F % tf == 0
    return pl.pallas_call(
        _mlp_kernel,
        grid=(T // tm, F // tf),
        in_specs=[pl.BlockSpec((tm, D), lambda i, j: (i, 0)),
                  pl.BlockSpec((1, D), lambda i, j: (0, 0)),
                  pl.BlockSpec((D, tf), lambda i, j: (0, j)),
                  pl.BlockSpec((tf, D), lambda i, j: (j, 0)),
                  pl.BlockSpec((1, D), lambda i, j: (0, 0))],
        out_specs=pl.BlockSpec((tm, D), lambda i, j: (i, 0)),
        out_shape=jax.ShapeDtypeStruct((T, D), _F32),
        scratch_shapes=[pltpu.VMEM((tm, D), _BF16)],
        compiler_params=_params(("parallel", "arbitrary")),
        name="mlp_final_norm",
    )(x2d, g, w1, w2, gf)


_REST_GA = 0
_REST_GR = D_MODEL
_REST_RX = 2 * D_MODEL
_REST_RY = 2 * D_MODEL + LRU_W


def _layer(x, bias_tabs, p):
    B, S, D = x.shape
    T = B * S
    x2d = x.reshape(T, D)
    zgs = _qkv_proj(x2d, p["g_mix"], p["w_qkv"], B, S, tm=512)
    zr = _norm_proj(x2d, p["g_mix"], p["w_rest"], tm=256, tn=1792, out_dtype=_F32)
    os_, lses = [], []
    for g, (_, dil) in enumerate(DILATED_GROUPS):
        o, lse = _attn_group(zgs[g], bias_tabs, g, dil, B, S)
        os_.append(o)
        lses.append(lse)
    rnn = _rglru(zr.reshape(B, S, -1), _REST_RX, _REST_RY, p["conv_w"], p["conv_b"],
                 p["w_gate"], p["b_gate"], p["lam"])
    x2 = _merge(os_, lses, rnn, zr, _REST_GA, _REST_GR, x2d,
                p["w_attn_o"], p["w_rnn_o"], p["w_out"], B, S, tm=256)
    y = _mlp(x2, p["g_mlp"], p["w_mlp_in"], p["w_mlp_out"], p["g_final"], tm=1024, tf=512)
    return y.reshape(B, S, D)


def kernel(x_prompt, x_sample, rel_bias, norm_mix_g, w_in, conv_w, conv_b, lru_wa, lru_ba,
           lru_wx, lru_bx, lru_lambda, w_attn_o, w_rnn_o, w_out, norm_mlp_g, w_mlp_in,
           w_mlp_out, norm_final_g):
    depth = w_in.shape[0]
    assert depth == 1, "the final norm is fused into the single layer's MLP kernel"
    bias_tabs = _bias_tables(rel_bias)
    q_end, rx_end, ry_end, ga_end = 3 * ATTN_W, 3 * ATTN_W + LRU_W, 3 * ATTN_W + 2 * LRU_W, \
        3 * ATTN_W + 2 * LRU_W + D_MODEL
    wl = w_in[0]
    w_qkv = jnp.concatenate(
        [wl[:, part * ATTN_W + g * ATTN_OUT_W:part * ATTN_W + (g + 1) * ATTN_OUT_W]
         for g in range(N_GROUPS) for part in range(3)], axis=-1).astype(_BF16)
    w_gate = jnp.concatenate([lru_wa[0, 0], lru_wx[0, 0], lru_wa[0, 1], lru_wx[0, 1]],
                             axis=-1).astype(_BF16)
    b_gate = jnp.concatenate(
        [b.reshape(LRU_BLOCKS, 1, LRU_BLOCK_W)
         for b in (lru_ba[0, 0], lru_bx[0, 0], lru_ba[0, 1], lru_bx[0, 1])], axis=-1)
    p = dict(
        g_mix=norm_mix_g[0].reshape(1, D_MODEL),
        w_qkv=w_qkv,
        w_rest=jnp.concatenate([wl[:, ry_end:ga_end], wl[:, ga_end:], wl[:, q_end:rx_end],
                                wl[:, rx_end:ry_end]], axis=-1).astype(_BF16),
        conv_w=conv_w[0], conv_b=conv_b[0].reshape(1, LRU_W),
        w_gate=w_gate, b_gate=0.5 * b_gate.astype(_F32), lam=lru_lambda[0],
        w_attn_o=w_attn_o[0].astype(_BF16), w_rnn_o=w_rnn_o[0].astype(_BF16),
        w_out=w_out[0].astype(_BF16),
        g_mlp=norm_mlp_g[0].reshape(1, D_MODEL),
        w_mlp_in=w_mlp_in[0].astype(_BF16), w_mlp_out=w_mlp_out[0].astype(_BF16),
        g_final=norm_final_g.reshape(1, D_MODEL),
    )
    return (_layer(x_prompt, bias_tabs, p), _layer(x_sample, bias_tabs, p))
```

```python
import functools
import math

import jax
import jax.numpy as jnp
import numpy as np
from jax import lax
from jax.experimental import pallas as pl
from jax.experimental.pallas import tpu as pltpu

D_MODEL = 2048
HEAD_DIM = 128
HEADS_PER_GROUP = 4
DILATED_GROUPS = ((128, 1), (512, 4), (2048, 16))
N_GROUPS = len(DILATED_GROUPS)
ATTN_W = N_GROUPS * HEADS_PER_GROUP * HEAD_DIM
ATTN_OUT_W = HEADS_PER_GROUP * HEAD_DIM
NUM_BUCKETS = 32
REL_MAX_DIST = 1024
LRU_W = 1536
LRU_BLOCKS = 12
LRU_BLOCK_W = LRU_W // LRU_BLOCKS
LRU_C = 8.0
CONV_W = 4
D_FF = 4 * D_MODEL
NORM_EPS = 1e-6
NEG_INF = -1e30

_F32 = jnp.float32
_BF16 = jnp.bfloat16
_V7X_VMEM_LIMIT_BYTES = 56 * 1024 * 1024
_LANES = 128
_SUBLANES = 8

_SIDE = DILATED_GROUPS[0][0] // (2 * DILATED_GROUPS[0][1])
assert all(w // (2 * d) == _SIDE for w, d in DILATED_GROUPS)
_QBLK = 2 * _SIDE
_KWIN = 4 * _SIDE
assert _QBLK == _LANES
_GROUP_QKV_W = 3 * ATTN_OUT_W


def _params(semantics):
    return pltpu.CompilerParams(dimension_semantics=semantics,
                                vmem_limit_bytes=_V7X_VMEM_LIMIT_BYTES)


def _rms(x, g):
    ms = jnp.mean(x * x, axis=-1, keepdims=True)
    return x * lax.rsqrt(ms + NORM_EPS) * g


def _lane_block(c):
    return slice(c * _LANES, (c + 1) * _LANES)


_LOG2E = math.log2(math.e)
_EDGE_VARIANTS = 4


def _rel_bucket_np(rel):
    half = NUM_BUCKETS // 2
    max_exact = half // 2
    n = np.abs(rel)
    nf = np.maximum(n, 1).astype(np.float32)
    ratio = np.log(nf / np.float32(max_exact)) / np.float32(math.log(REL_MAX_DIST / max_exact))
    large = max_exact + (ratio * np.float32(half - max_exact)).astype(np.int32)
    large = np.minimum(large, half - 1)
    return (np.where(rel > 0, half, 0) + np.where(n < max_exact, n, large)).astype(np.int32)


def _bias_kernel(tab_ref, idx_ref, o_ref):
    head = pl.program_id(0)
    idx = idx_ref[...]
    acc = jnp.zeros(idx.shape, _F32)
    for b in range(NUM_BUCKETS):
        acc = jnp.where(idx == b, tab_ref[b, head], acc)
    acc = acc * _LOG2E
    kj = lax.broadcasted_iota(jnp.int32, idx.shape, 1)
    in_band = idx >= 0
    after_start = kj >= _SIDE
    before_end = kj < _KWIN - _SIDE
    o_ref[0] = jnp.where(in_band, acc, NEG_INF)
    o_ref[1] = jnp.where(jnp.logical_and(in_band, after_start), acc, NEG_INF)
    o_ref[2] = jnp.where(jnp.logical_and(in_band, before_end), acc, NEG_INF)
    o_ref[3] = jnp.where(jnp.logical_and(in_band, jnp.logical_and(after_start, before_end)),
                         acc, NEG_INF)


def _bias_tables(rel_bias):
    qi = np.arange(_QBLK, dtype=np.int32)[:, None]
    kj = np.arange(_KWIN, dtype=np.int32)[None, :]
    rel = kj - _SIDE - qi
    band = np.abs(rel) <= _SIDE
    idx = np.stack([np.where(band, _rel_bucket_np(rel * dil), -1) for _, dil in DILATED_GROUPS])
    n_heads = N_GROUPS * HEADS_PER_GROUP
    return pl.pallas_call(
        _bias_kernel,
        grid=(n_heads,),
        in_specs=[pl.BlockSpec(memory_space=pltpu.SMEM),
                  pl.BlockSpec((None, _QBLK, _KWIN), lambda h: (h // HEADS_PER_GROUP, 0, 0))],
        out_specs=pl.BlockSpec((_EDGE_VARIANTS, None, _QBLK, _KWIN), lambda h: (0, h, 0, 0)),
        out_shape=jax.ShapeDtypeStruct((_EDGE_VARIANTS, n_heads, _QBLK, _KWIN), _F32),
        name="bias_tables",
    )(rel_bias.astype(_F32), jnp.asarray(idx.astype(np.int32)))


_Q_SCALE = HEAD_DIM ** -0.5 * _LOG2E


def _qkv_proj_kernel(x_ref, g_ref, w_ref, *refs):
    out_refs = refs[:N_GROUPS]
    u_ref, stage_ref = refs[N_GROUPS:]
    tm = x_ref.shape[0]
    n_slabs = _GROUP_QKV_W // _LANES
    u_ref[...] = _rms(x_ref[...], g_ref[...]).astype(u_ref.dtype)
    for g, (_, dil) in enumerate(DILATED_GROUPS):
        o_ref = out_refs[g]
        res = jnp.dot(u_ref[...], w_ref[:, g * _GROUP_QKV_W:(g + 1) * _GROUP_QKV_W],
                      preferred_element_type=_F32)
        if dil == 1:
            o_ref[:, :ATTN_OUT_W] = (res[:, :ATTN_OUT_W] * _Q_SCALE).astype(o_ref.dtype)
            o_ref[:, ATTN_OUT_W:] = res[:, ATTN_OUT_W:].astype(o_ref.dtype)
            continue
        for c in range(n_slabs):
            slab = res[:, _lane_block(c)]
            stage_ref[c] = slab * _Q_SCALE if c < ATTN_OUT_W // _LANES else slab
        rows = tm // dil
        for r in range(dil):
            for c in range(n_slabs):
                o_ref[r, :, _lane_block(c)] = stage_ref[
                    c, pl.ds(r, rows, stride=dil), :].astype(o_ref.dtype)


def _qkv_proj(x2d, g, w, B, S, *, tm):
    T, D = x2d.shape
    assert T == B * S and S % tm == 0
    tiles_per_seq = S // tm
    out_shape, out_specs = [], []
    for _, dil in DILATED_GROUPS:
        assert tm % (dil * 16) == 0
        if dil == 1:
            out_shape.append(jax.ShapeDtypeStruct((T, _GROUP_QKV_W), _BF16))
            out_specs.append(pl.BlockSpec((tm, _GROUP_QKV_W), lambda i: (i, 0)))
        else:
            out_shape.append(jax.ShapeDtypeStruct((B, dil, S // dil, _GROUP_QKV_W), _BF16))
            out_specs.append(pl.BlockSpec(
                (None, dil, tm // dil, _GROUP_QKV_W),
                lambda i: (i // tiles_per_seq, 0, i % tiles_per_seq, 0)))
    return pl.pallas_call(
        _qkv_proj_kernel,
        grid=(T // tm,),
        in_specs=[pl.BlockSpec((tm, D), lambda i: (i, 0)),
                  pl.BlockSpec((1, D), lambda i: (0, 0)),
                  pl.BlockSpec(w.shape, lambda i: (0, 0), pipeline_mode=pl.Buffered(1))],
        out_specs=out_specs,
        out_shape=out_shape,
        scratch_shapes=[pltpu.VMEM((tm, D), _BF16),
                        pltpu.VMEM((_GROUP_QKV_W // _LANES, tm, _LANES), _F32)],
        compiler_params=_params(("parallel",)),
        name="qkv_proj",
    )(x2d, g, w)


def _norm_proj_kernel(x_ref, g_ref, w_ref, o_ref, u_ref, *, tn):
    u_ref[...] = _rms(x_ref[...], g_ref[...]).astype(u_ref.dtype)
    for c in range(w_ref.shape[1] // tn):
        cols = slice(c * tn, (c + 1) * tn)
        o_ref[:, cols] = jnp.dot(u_ref[...], w_ref[:, cols],
                                 preferred_element_type=_F32).astype(o_ref.dtype)


def _norm_proj(x2d, g, w, *, tm, tn, out_dtype):
    T, D = x2d.shape
    N = w.shape[1]
    assert T % tm == 0 and N % tn == 0
    return pl.pallas_call(
        functools.partial(_norm_proj_kernel, tn=tn),
        grid=(T // tm,),
        in_specs=[pl.BlockSpec((tm, D), lambda i: (i, 0)),
                  pl.BlockSpec((1, D), lambda i: (0, 0)),
                  pl.BlockSpec(w.shape, lambda i: (0, 0), pipeline_mode=pl.Buffered(1))],
        out_specs=pl.BlockSpec((tm, N), lambda i: (i, 0)),
        out_shape=jax.ShapeDtypeStruct((T, N), out_dtype),
        scratch_shapes=[pltpu.VMEM((tm, D), _BF16)],
        compiler_params=_params(("parallel",)),
        name="norm_proj",
    )(x2d, g, w)


_ML_SHIFT = _LANES // 2


def _attn_kernel(q_ref, k_ref, v_ref, b_ref, o_ref, ml_ref, kpad, vpad, *, L, dil, heads, unroll):
    W = heads * HEAD_DIM
    zeros = jnp.zeros((_SIDE, W), _BF16)
    for r in range(dil):
        for pad, src in ((kpad, k_ref), (vpad, v_ref)):
            pad[r, 0:_SIDE, :] = zeros
            pad[r, _SIDE + L:2 * _SIDE + L, :] = zeros
            pad[r, _SIDE:_SIDE + L, :] = src[r]

    lane = lax.broadcasted_iota(jnp.int32, (_QBLK, _LANES), 1)
    nblk = L // _QBLK
    assert nblk & (nblk - 1) == 0
    blk_shift = nblk.bit_length() - 1

    def body(n, carry):
        r = lax.shift_right_logical(n, blk_shift)
        blk = n & (nblk - 1)
        qs = pl.multiple_of(blk * _QBLK, _QBLK)
        edge = jnp.where(blk == 0, 1, 0) + jnp.where(blk == nblk - 1, 2, 0)
        ml_tile = jnp.zeros((_QBLK, _LANES), _F32)
        for h in range(heads):
            cols = _lane_block(h)
            q = q_ref[r, pl.ds(qs, _QBLK), cols]
            kw = kpad[r, pl.ds(qs, _KWIN), cols]
            vw = vpad[r, pl.ds(qs, _KWIN), cols]
            s = lax.dot_general(q, kw, (((1,), (1,)), ((), ())),
                                preferred_element_type=_F32) + b_ref[edge, h]
            m = jnp.max(s, axis=-1, keepdims=True)
            p = jnp.exp2(s - m)
            l = jnp.sum(p, axis=-1, keepdims=True)
            o_ref[r, pl.ds(qs, _QBLK), cols] = jnp.dot(p.astype(_BF16), vw,
                                                       preferred_element_type=_F32)
            ml_tile = jnp.where(lane == h, m, jnp.where(lane == _ML_SHIFT + h, l, ml_tile))
        ml_ref[r, pl.ds(qs, _QBLK), :] = ml_tile
        return carry

    lax.fori_loop(0, dil * nblk, body, 0, unroll=unroll)


def _attn_heads_per_step(S):
    return HEADS_PER_GROUP if S <= 2048 else HEADS_PER_GROUP // 2


def _attn_group(zg, bias, group, dil, B, S):
    L = S // dil
    assert L % _QBLK == 0
    heads = _attn_heads_per_step(S)
    steps = HEADS_PER_GROUP // heads
    W = heads * HEAD_DIM
    zg = zg.reshape(B, dil, L, _GROUP_QKV_W)
    col = lambda part: (lambda b, hp: (b, 0, 0, part * steps + hp))
    return pl.pallas_call(
        functools.partial(_attn_kernel, L=L, dil=dil, heads=heads,
                          unroll=min(64 // heads, dil * L // _QBLK)),
        grid=(B, steps),
        in_specs=[pl.BlockSpec((None, dil, L, W), col(0)),
                  pl.BlockSpec((None, dil, L, W), col(1)),
                  pl.BlockSpec((None, dil, L, W), col(2)),
                  pl.BlockSpec((_EDGE_VARIANTS, heads, _QBLK, _KWIN),
                               lambda b, hp: (0, group * steps + hp, 0, 0))],
        out_specs=[pl.BlockSpec((None, dil, L, W), lambda b, hp: (b, 0, 0, hp)),
                   pl.BlockSpec((None, dil, L, _LANES), lambda b, hp: (b, 0, 0, hp))],
        out_shape=[jax.ShapeDtypeStruct((B, dil, L, ATTN_OUT_W), _F32),
                   jax.ShapeDtypeStruct((B, dil, L, steps * _LANES), _F32)],
        scratch_shapes=[pltpu.VMEM((dil, L + 2 * _SIDE, W), _BF16),
                        pltpu.VMEM((dil, L + 2 * _SIDE, W), _BF16)],
        compiler_params=_params(("parallel", "parallel")),
        name=f"dilated_attn_g{group}",
    )(zg, zg, zg, bias)


_CHUNKS = 4 * _SUBLANES
_PITCH_PAD = 4
_GATE_TAUS = 8
_SCAN_UNROLL = 16
_BF16_ROWS = 2 * _SUBLANES


def _sigmoid(x):
    return 0.5 * jnp.tanh(0.5 * x) + 0.5


def _gelu_tanh(x):
    k = math.sqrt(2.0 / math.pi)
    half = 0.5 * x
    return half * jnp.tanh(x * (k + (k * 0.044715) * (x * x))) + half


def _chunk_rows(ref, start, pitch):
    return jnp.concatenate(
        [ref[pl.ds(start + half * _SUBLANES * pitch, _SUBLANES, stride=pitch), :]
         for half in range(_CHUNKS // _SUBLANES)], axis=0)


def _rglru_kernel(rx_ref, ry_ref, cw_ref, cb_ref, wg_ref, bg_ref, lam_ref, o_ref,
                  xs, a_f, h_f, a_b, h_b, *, S):
    BW = LRU_BLOCK_W
    Lc = S // _CHUNKS
    P = Lc + _PITCH_PAD
    left = CONV_W // 2

    for c in range(_CHUNKS):
        lo = c * Lc - left
        hi = lo + P
        s_lo, s_hi = max(lo, 0), min(hi, S)
        a_lo = s_lo // _BF16_ROWS * _BF16_ROWS
        a_hi = -(-s_hi // _BF16_ROWS) * _BF16_ROWS
        v = rx_ref[a_lo:a_hi, :].astype(_F32)
        xs[c * P + (s_lo - lo):c * P + (s_hi - lo), :] = v[s_lo - a_lo:s_hi - a_lo]
        if lo < 0:
            xs[c * P:c * P - lo, :] = jnp.zeros((-lo, BW), _F32)
        if hi > S:
            xs[c * P + (S - lo):(c + 1) * P, :] = jnp.zeros((hi - S, BW), _F32)

    cw = 0.5 * cw_ref[...]
    cb = 0.5 * cb_ref[...]
    half_decay = (-0.5 * LRU_C) * jax.nn.softplus(-lam_ref[...])
    TB = _GATE_TAUS

    def gate_body(bi, carry):
        t0 = bi * TB
        taps = [_chunk_rows(xs, t0 + j, P) for j in range(TB + CONV_W - 1)]
        rows = []
        for tau in range(TB):
            xc = cb
            for t in range(CONV_W):
                xc = xc + taps[tau + t] * cw[t:t + 1]
            rows.append(xc)
        half_x = jnp.concatenate(rows, axis=0)
        g = jnp.tanh(jnp.dot(half_x.astype(_BF16), wg_ref[...], preferred_element_type=_F32)
                     + bg_ref[...])
        r0 = pl.multiple_of(t0 * _CHUNKS, TB * _CHUNKS)
        for d, (a_ref, u_ref) in enumerate(((a_f, h_f), (a_b, h_b))):
            log_a = half_decay[d:d + 1] * g[:, (2 * d) * BW:(2 * d + 1) * BW] + half_decay[d:d + 1]
            ix = (g[:, (2 * d + 1) * BW:(2 * d + 2) * BW] + 1.0) * half_x
            a = jnp.exp(log_a)
            om = jnp.tanh(log_a) * (-1.0 - a * a)
            root = jnp.where(om > 0.0, om * lax.rsqrt(om), 0.0)
            a_ref[pl.ds(r0, TB * _CHUNKS), :] = a
            u_ref[pl.ds(r0, TB * _CHUNKS), :] = root * ix
        return carry

    lax.fori_loop(0, Lc // TB, gate_body, 0, unroll=4)

    def scan_body(tau, carry):
        hf, pf, hb, pb = carry
        rf = pl.multiple_of(tau * _CHUNKS, _CHUNKS)
        rb = pl.multiple_of((Lc - 1 - tau) * _CHUNKS, _CHUNKS)
        af = a_f[pl.ds(rf, _CHUNKS), :]
        hf = af * hf + h_f[pl.ds(rf, _CHUNKS), :]
        pf = af * pf
        h_f[pl.ds(rf, _CHUNKS), :] = hf
        a_f[pl.ds(rf, _CHUNKS), :] = pf
        ab = a_b[pl.ds(rb, _CHUNKS), :]
        hb = ab * hb + h_b[pl.ds(rb, _CHUNKS), :]
        pb = ab * pb
        h_b[pl.ds(rb, _CHUNKS), :] = hb
        a_b[pl.ds(rb, _CHUNKS), :] = pb
        return hf, pf, hb, pb

    zero = jnp.zeros((_CHUNKS, BW), _F32)
    one = jnp.ones((_CHUNKS, BW), _F32)
    h_end, p_end, h_beg, p_beg = lax.fori_loop(0, Lc, scan_body, (zero, one, zero, one),
                                               unroll=_SCAN_UNROLL)

    row = lax.broadcasted_iota(jnp.int32, (_CHUNKS, BW), 0)
    in_f, in_b = zero, zero
    prev_f = jnp.zeros((1, BW), _F32)
    prev_b = jnp.zeros((1, BW), _F32)
    for c in range(1, _CHUNKS):
        prev_f = h_end[c - 1:c] + p_end[c - 1:c] * prev_f
        in_f = jnp.where(row == c, prev_f, in_f)
        cb_ = _CHUNKS - 1 - c
        prev_b = h_beg[cb_ + 1:cb_ + 2] + p_beg[cb_ + 1:cb_ + 2] * prev_b
        in_b = jnp.where(row == cb_, prev_b, in_b)

    def fix_body(tau, carry):
        r = pl.multiple_of(tau * _CHUNKS, _CHUNKS)
        h = (h_f[pl.ds(r, _CHUNKS), :] + a_f[pl.ds(r, _CHUNKS), :] * in_f
             + h_b[pl.ds(r, _CHUNKS), :] + a_b[pl.ds(r, _CHUNKS), :] * in_b)
        for half in range(_CHUNKS // _SUBLANES):
            xs[pl.ds(tau + half * _SUBLANES * P, _SUBLANES, stride=P), :] = \
                h[half * _SUBLANES:(half + 1) * _SUBLANES]
        return carry

    lax.fori_loop(0, Lc, fix_body, 0, unroll=_SCAN_UNROLL)

    for c in range(_CHUNKS):
        t = slice(c * Lc, (c + 1) * Lc)
        o_ref[t, :] = (xs[c * P:c * P + Lc, :] * _gelu_tanh(ry_ref[t, :].astype(_F32))).astype(o_ref.dtype)


def _rglru(zr, rx_col, ry_col, conv_w, conv_b, wg, bg, lam):
    B, S, _ = zr.shape
    BW = LRU_BLOCK_W
    Lc = S // _CHUNKS
    assert S % (_CHUNKS * _GATE_TAUS) == 0 and Lc % _SCAN_UNROLL == 0
    assert rx_col % BW == 0 and ry_col % BW == 0
    seq = pltpu.VMEM((S, BW), _F32)
    out = pl.pallas_call(
        functools.partial(_rglru_kernel, S=S),
        grid=(B, LRU_BLOCKS),
        in_specs=[pl.BlockSpec((None, S, BW), lambda b, n: (b, 0, rx_col // BW + n)),
                  pl.BlockSpec((None, S, BW), lambda b, n: (b, 0, ry_col // BW + n)),
                  pl.BlockSpec((CONV_W, BW), lambda b, n: (0, n)),
                  pl.BlockSpec((1, BW), lambda b, n: (0, n)),
                  pl.BlockSpec((None, BW, 4 * BW), lambda b, n: (n, 0, 0)),
                  pl.BlockSpec((None, 1, 4 * BW), lambda b, n: (n, 0, 0)),
                  pl.BlockSpec((2, BW), lambda b, n: (0, n))],
        out_specs=pl.BlockSpec((None, S, BW), lambda b, n: (b, 0, n)),
        out_shape=jax.ShapeDtypeStruct((B, S, LRU_W), _BF16),
        scratch_shapes=[pltpu.VMEM((_CHUNKS * (Lc + _PITCH_PAD), BW), _F32), seq, seq, seq, seq],
        compiler_params=_params(("parallel", "parallel")),
        name="conv_rglru",
    )(zr, zr, conv_w, conv_b, wg, bg, lam)
    return out.reshape(B * S, LRU_W)


def _unfold(src_ref, dst_ref, dil):
    rows = src_ref.shape[1]
    for r in range(dil):
        for c in range(dst_ref.shape[0]):
            dst_ref[c, pl.ds(r, rows, stride=dil), :] = src_ref[r, :, _lane_block(c)]


def _merge_kernel(*refs, heads):
    o_refs = refs[0:N_GROUPS]
    l_refs = refs[N_GROUPS:2 * N_GROUPS]
    rnn_ref, ga_ref, gr_ref, x_ref, wa_ref, wr_ref, wo_ref, out_ref = refs[2 * N_GROUPS:2 * N_GROUPS + 8]
    scratch = refs[2 * N_GROUPS + 8:]

    pr = jnp.dot(rnn_ref[...], wr_ref[...], preferred_element_type=_F32)

    o_get, l_get, k = [], [], 0
    for g, (_, dil) in enumerate(DILATED_GROUPS):
        if dil == 1:
            o_get.append(lambda c, ref=o_refs[g]: ref[:, _lane_block(c)])
            l_get.append(lambda c, ref=l_refs[g]: ref[:, _lane_block(c)])
        else:
            o_scr, l_scr = scratch[k], scratch[k + 1]
            k += 2
            _unfold(o_refs[g], o_scr, dil)
            _unfold(l_refs[g], l_scr, dil)
            o_get.append(lambda c, ref=o_scr: ref[c])
            l_get.append(lambda c, ref=l_scr: ref[c])

    wts = []
    for step in range(HEADS_PER_GROUP // heads):
        ml = [get(step) for get in l_get]
        mx = functools.reduce(jnp.maximum, ml)
        es = [jnp.exp2(t - mx) for t in ml]
        den = functools.reduce(jnp.add, [e * pltpu.roll(t, _ML_SHIFT, axis=1)
                                         for e, t in zip(es, ml)])
        inv = 1.0 / den
        wts.append([e * inv for e in es])
    attn = []
    for h in range(HEADS_PER_GROUP):
        w = wts[h // heads]
        c = h % heads
        attn.append(functools.reduce(
            jnp.add, [w[g][:, c:c + 1] * o_get[g](h) for g in range(N_GROUPS)]))
    attn = jnp.concatenate(attn, axis=-1).astype(_BF16)
    pa = jnp.dot(attn, wa_ref[...], preferred_element_type=_F32)
    merged = (_sigmoid(ga_ref[...].astype(_F32)) * pa
              + _sigmoid(gr_ref[...].astype(_F32)) * pr)
    out_ref[...] = x_ref[...] + jnp.dot(merged.astype(_BF16), wo_ref[...],
                                        preferred_element_type=_F32)


def _merge(os_, lses, rnn, zr, ga_col, gr_col, x2d, wa, wr, wo, B, S, *, tm):
    T, D = x2d.shape
    assert S % tm == 0 and ga_col % D == 0 and gr_col % D == 0
    heads = _attn_heads_per_step(S)
    LW = (HEADS_PER_GROUP // heads) * _LANES
    tiles_per_seq = S // tm
    row = lambda w: pl.BlockSpec((tm, w), lambda i: (i, 0))
    const = lambda a: pl.BlockSpec(a.shape, lambda i: (0, 0), pipeline_mode=pl.Buffered(1))

    def group_spec(dil, w):
        if dil == 1:
            return row(w)
        assert tm % (dil * _SUBLANES) == 0
        return pl.BlockSpec((None, dil, tm // dil, w),
                            lambda i: (i // tiles_per_seq, 0, i % tiles_per_seq, 0))

    scratch = []
    for _, dil in DILATED_GROUPS:
        if dil > 1:
            scratch += [pltpu.VMEM((ATTN_OUT_W // _LANES, tm, _LANES), _F32),
                        pltpu.VMEM((LW // _LANES, tm, _LANES), _F32)]
    os_ = [o.reshape(T, ATTN_OUT_W) if dil == 1 else o for o, (_, dil) in zip(os_, DILATED_GROUPS)]
    lses = [l.reshape(T, LW) if dil == 1 else l for l, (_, dil) in zip(lses, DILATED_GROUPS)]
    return pl.pallas_call(
        functools.partial(_merge_kernel, heads=heads),
        grid=(T // tm,),
        in_specs=[group_spec(dil, ATTN_OUT_W) for _, dil in DILATED_GROUPS]
        + [group_spec(dil, LW) for _, dil in DILATED_GROUPS]
        + [row(LRU_W),
           pl.BlockSpec((tm, D), lambda i: (i, ga_col // D)),
           pl.BlockSpec((tm, D), lambda i: (i, gr_col // D)),
           row(D), const(wa), const(wr), const(wo)],
        out_specs=row(D),
        out_shape=jax.ShapeDtypeStruct((T, D), _F32),
        scratch_shapes=scratch,
        compiler_params=_params(("parallel",)),
        name="merge_proj",
    )(*os_, *lses, rnn, zr, zr, x2d, wa, wr, wo)


def _mlp_kernel(x_ref, g_ref, w1_ref, w2_ref, gf_ref, o_ref, u_ref):
    j = pl.program_id(1)

    @pl.when(j == 0)
    def _():
        x = x_ref[...]
        u_ref[...] = _rms(x, g_ref[...]).astype(u_ref.dtype)
        o_ref[...] = x

    h = jnp.dot(u_ref[...], w1_ref[...], preferred_element_type=_F32)
    h = jnp.square(jnp.maximum(h, 0.0)).astype(_BF16)
    o_ref[...] += jnp.dot(h, w2_ref[...], preferred_element_type=_F32)

    @pl.when(j == pl.num_programs(1) - 1)
    def _():
        o_ref[...] = _rms(o_ref[...], gf_ref[...])


def _mlp(x2d, g, w1, w2, gf, *, tm, tf):
    T, D = x2d.shape
    F = w1.shape[1]
    assert T % tm == 0 and F % tf == 0
    return pl.pallas_call(
        _mlp_kernel,
        grid=(T // tm, F // tf),
        in_specs=[pl.BlockSpec((tm, D), lambda i, j: (i, 0)),
                  pl.BlockSpec((1, D), lambda i, j: (0, 0)),
                  pl.BlockSpec((D, tf), lambda i, j: (0, j)),
                  pl.BlockSpec((tf, D), lambda i, j: (j, 0)),
                  pl.BlockSpec((1, D), lambda i, j: (0, 0))],
        out_specs=pl.BlockSpec((tm, D), lambda i, j: (i, 0)),
        out_shape=jax.ShapeDtypeStruct((T, D), _F32),
        scratch_shapes=[pltpu.VMEM((tm, D), _BF16)],
        compiler_params=_params(("parallel", "arbitrary")),
        name="mlp_final_norm",
    )(x2d, g, w1, w2, gf)


_REST_GA = 0
_REST_GR = D_MODEL
_REST_RX = 2 * D_MODEL
_REST_RY = 2 * D_MODEL + LRU_W


def _layer(x, bias_tabs, p):
    B, S, D = x.shape
    T = B * S
    x2d = x.reshape(T, D)
    zgs = _qkv_proj(x2d, p["g_mix"], p["w_qkv"], B, S, tm=512)
    zr = _norm_proj(x2d, p["g_mix"], p["w_rest"], tm=256, tn=1792, out_dtype=_BF16)
    os_, lses = [], []
    for g, (_, dil) in enumerate(DILATED_GROUPS):
        o, lse = _attn_group(zgs[g], bias_tabs, g, dil, B, S)
        os_.append(o)
        lses.append(lse)
    rnn = _rglru(zr.reshape(B, S, -1), _REST_RX, _REST_RY, p["conv_w"], p["conv_b"],
                 p["w_gate"], p["b_gate"], p["lam"])
    x2 = _merge(os_, lses, rnn, zr, _REST_GA, _REST_GR, x2d,
                p["w_attn_o"], p["w_rnn_o"], p["w_out"], B, S, tm=256)
    y = _mlp(x2, p["g_mlp"], p["w_mlp_in"], p["w_mlp_out"], p["g_final"], tm=1024, tf=512)
    return y.reshape(B, S, D)


def kernel(x_prompt, x_sample, rel_bias, norm_mix_g, w_in, conv_w, conv_b, lru_wa, lru_ba,
           lru_wx, lru_bx, lru_lambda, w_attn_o, w_rnn_o, w_out, norm_mlp_g, w_mlp_in,
           w_mlp_out, norm_final_g):
    depth = w_in.shape[0]
    assert depth == 1, "the final norm is fused into the single layer's MLP kernel"
    bias_tabs = _bias_tables(rel_bias)
    q_end, rx_end, ry_end, ga_end = 3 * ATTN_W, 3 * ATTN_W + LRU_W, 3 * ATTN_W + 2 * LRU_W, \
        3 * ATTN_W + 2 * LRU_W + D_MODEL
    wl = w_in[0]
    w_qkv = jnp.concatenate(
        [wl[:, part * ATTN_W + g * ATTN_OUT_W:part * ATTN_W + (g + 1) * ATTN_OUT_W]
         for g in range(N_GROUPS) for part in range(3)], axis=-1).astype(_BF16)
    w_gate = jnp.concatenate([lru_wa[0, 0], lru_wx[0, 0], lru_wa[0, 1], lru_wx[0, 1]],
                             axis=-1).astype(_BF16)
    b_gate = jnp.concatenate(
        [b.reshape(LRU_BLOCKS, 1, LRU_BLOCK_W)
         for b in (lru_ba[0, 0], lru_bx[0, 0], lru_ba[0, 1], lru_bx[0, 1])], axis=-1)
    p = dict(
        g_mix=norm_mix_g[0].reshape(1, D_MODEL),
        w_qkv=w_qkv,
        w_rest=jnp.concatenate([wl[:, ry_end:ga_end], wl[:, ga_end:], wl[:, q_end:rx_end],
                                wl[:, rx_end:ry_end]], axis=-1).astype(_BF16),
        conv_w=conv_w[0], conv_b=conv_b[0].reshape(1, LRU_W),
        w_gate=w_gate, b_gate=0.5 * b_gate.astype(_F32), lam=lru_lambda[0],
        w_attn_o=w_attn_o[0].astype(_BF16), w_rnn_o=w_rnn_o[0].astype(_BF16),
        w_out=w_out[0].astype(_BF16),
        g_mlp=norm_mlp_g[0].reshape(1, D_MODEL),
        w_mlp_in=w_mlp_in[0].astype(_BF16), w_mlp_out=w_mlp_out[0].astype(_BF16),
        g_final=norm_final_g.reshape(1, D_MODEL),
    )
    return (_layer(x_prompt, bias_tabs, p), _layer(x_sample, bias_tabs, p))
```

```python
import functools
import math

import jax
import jax.numpy as jnp
import numpy as np
from jax import lax
from jax.experimental import pallas as pl
from jax.experimental.pallas import tpu as pltpu

D_MODEL = 2048
HEAD_DIM = 128
HEADS_PER_GROUP = 4
DILATED_GROUPS = ((128, 1), (512, 4), (2048, 16))
N_GROUPS = len(DILATED_GROUPS)
ATTN_W = N_GROUPS * HEADS_PER_GROUP * HEAD_DIM
ATTN_OUT_W = HEADS_PER_GROUP * HEAD_DIM
NUM_BUCKETS = 32
REL_MAX_DIST = 1024
LRU_W = 1536
LRU_BLOCKS = 12
LRU_BLOCK_W = LRU_W // LRU_BLOCKS
LRU_C = 8.0
CONV_W = 4
D_FF = 4 * D_MODEL
NORM_EPS = 1e-6
NEG_INF = -1e30

_F32 = jnp.float32
_BF16 = jnp.bfloat16
_V7X_VMEM_LIMIT_BYTES = 56 * 1024 * 1024
_LANES = 128
_SUBLANES = 8

_SIDE = DILATED_GROUPS[0][0] // (2 * DILATED_GROUPS[0][1])
assert all(w // (2 * d) == _SIDE for w, d in DILATED_GROUPS)
_QBLK = 2 * _SIDE
_KWIN = 4 * _SIDE
assert _QBLK == _LANES
_GROUP_QKV_W = 3 * ATTN_OUT_W


def _params(semantics):
    return pltpu.CompilerParams(dimension_semantics=semantics,
                                vmem_limit_bytes=_V7X_VMEM_LIMIT_BYTES)


def _rms(x, g):
    ms = jnp.mean(x * x, axis=-1, keepdims=True)
    return x * lax.rsqrt(ms + NORM_EPS) * g


def _lane_block(c):
    return slice(c * _LANES, (c + 1) * _LANES)


_LOG2E = math.log2(math.e)
_EDGE_VARIANTS = 4


def _rel_bucket_np(rel):
    half = NUM_BUCKETS // 2
    max_exact = half // 2
    n = np.abs(rel)
    nf = np.maximum(n, 1).astype(np.float32)
    ratio = np.log(nf / np.float32(max_exact)) / np.float32(math.log(REL_MAX_DIST / max_exact))
    large = max_exact + (ratio * np.float32(half - max_exact)).astype(np.int32)
    large = np.minimum(large, half - 1)
    return (np.where(rel > 0, half, 0) + np.where(n < max_exact, n, large)).astype(np.int32)


def _bias_kernel(tab_ref, idx_ref, o_ref):
    head = pl.program_id(0)
    idx = idx_ref[...]
    acc = jnp.zeros(idx.shape, _F32)
    for b in range(NUM_BUCKETS):
        acc = jnp.where(idx == b, tab_ref[b, head], acc)
    acc = acc * _LOG2E
    kj = lax.broadcasted_iota(jnp.int32, idx.shape, 1)
    in_band = idx >= 0
    after_start = kj >= _SIDE
    before_end = kj < _KWIN - _SIDE
    o_ref[0] = jnp.where(in_band, acc, NEG_INF)
    o_ref[1] = jnp.where(jnp.logical_and(in_band, after_start), acc, NEG_INF)
    o_ref[2] = jnp.where(jnp.logical_and(in_band, before_end), acc, NEG_INF)
    o_ref[3] = jnp.where(jnp.logical_and(in_band, jnp.logical_and(after_start, before_end)),
                         acc, NEG_INF)


def _bias_tables(rel_bias):
    qi = np.arange(_QBLK, dtype=np.int32)[:, None]
    kj = np.arange(_KWIN, dtype=np.int32)[None, :]
    rel = kj - _SIDE - qi
    band = np.abs(rel) <= _SIDE
    idx = np.stack([np.where(band, _rel_bucket_np(rel * dil), -1) for _, dil in DILATED_GROUPS])
    n_heads = N_GROUPS * HEADS_PER_GROUP
    return pl.pallas_call(
        _bias_kernel,
        grid=(n_heads,),
        in_specs=[pl.BlockSpec(memory_space=pltpu.SMEM),
                  pl.BlockSpec((None, _QBLK, _KWIN), lambda h: (h // HEADS_PER_GROUP, 0, 0))],
        out_specs=pl.BlockSpec((_EDGE_VARIANTS, None, _QBLK, _KWIN), lambda h: (0, h, 0, 0)),
        out_shape=jax.ShapeDtypeStruct((_EDGE_VARIANTS, n_heads, _QBLK, _KWIN), _F32),
        name="bias_tables",
    )(rel_bias.astype(_F32), jnp.asarray(idx.astype(np.int32)))


_Q_SCALE = HEAD_DIM ** -0.5 * _LOG2E


def _qkv_proj_kernel(x_ref, g_ref, w_ref, *refs):
    out_refs = refs[:N_GROUPS]
    u_ref, stage_ref = refs[N_GROUPS:]
    tm = x_ref.shape[0]
    n_slabs = _GROUP_QKV_W // _LANES
    u_ref[...] = _rms(x_ref[...], g_ref[...]).astype(u_ref.dtype)
    for g, (_, dil) in enumerate(DILATED_GROUPS):
        o_ref = out_refs[g]
        res = jnp.dot(u_ref[...], w_ref[:, g * _GROUP_QKV_W:(g + 1) * _GROUP_QKV_W],
                      preferred_element_type=_F32)
        if dil == 1:
            o_ref[:, :ATTN_OUT_W] = (res[:, :ATTN_OUT_W] * _Q_SCALE).astype(o_ref.dtype)
            o_ref[:, ATTN_OUT_W:] = res[:, ATTN_OUT_W:].astype(o_ref.dtype)
            continue
        for c in range(n_slabs):
            slab = res[:, _lane_block(c)]
            stage_ref[c] = slab * _Q_SCALE if c < ATTN_OUT_W // _LANES else slab
        rows = tm // dil
        for r in range(dil):
            for c in range(n_slabs):
                o_ref[r, :, _lane_block(c)] = stage_ref[
                    c, pl.ds(r, rows, stride=dil), :].astype(o_ref.dtype)


def _qkv_proj(x2d, g, w, B, S, *, tm):
    T, D = x2d.shape
    assert T == B * S and S % tm == 0
    tiles_per_seq = S // tm
    out_shape, out_specs = [], []
    for _, dil in DILATED_GROUPS:
        assert tm % (dil * 16) == 0
        if dil == 1:
            out_shape.append(jax.ShapeDtypeStruct((T, _GROUP_QKV_W), _BF16))
            out_specs.append(pl.BlockSpec((tm, _GROUP_QKV_W), lambda i: (i, 0)))
        else:
            out_shape.append(jax.ShapeDtypeStruct((B, dil, S // dil, _GROUP_QKV_W), _BF16))
            out_specs.append(pl.BlockSpec(
                (None, dil, tm // dil, _GROUP_QKV_W),
                lambda i: (i // tiles_per_seq, 0, i % tiles_per_seq, 0)))
    return pl.pallas_call(
        _qkv_proj_kernel,
        grid=(T // tm,),
        in_specs=[pl.BlockSpec((tm, D), lambda i: (i, 0)),
                  pl.BlockSpec((1, D), lambda i: (0, 0)),
                  pl.BlockSpec(w.shape, lambda i: (0, 0), pipeline_mode=pl.Buffered(1))],
        out_specs=out_specs,
        out_shape=out_shape,
        scratch_shapes=[pltpu.VMEM((tm, D), _BF16),
                        pltpu.VMEM((_GROUP_QKV_W // _LANES, tm, _LANES), _F32)],
        compiler_params=_params(("parallel",)),
        name="qkv_proj",
    )(x2d, g, w)


def _norm_proj_kernel(x_ref, g_ref, w_ref, o_ref, u_ref, *, tn):
    u_ref[...] = _rms(x_ref[...], g_ref[...]).astype(u_ref.dtype)
    for c in range(w_ref.shape[1] // tn):
        cols = slice(c * tn, (c + 1) * tn)
        o_ref[:, cols] = jnp.dot(u_ref[...], w_ref[:, cols],
                                 preferred_element_type=_F32).astype(o_ref.dtype)


def _norm_proj(x2d, g, w, *, tm, tn, out_dtype):
    T, D = x2d.shape
    N = w.shape[1]
    assert T % tm == 0 and N % tn == 0
    return pl.pallas_call(
        functools.partial(_norm_proj_kernel, tn=tn),
        grid=(T // tm,),
        in_specs=[pl.BlockSpec((tm, D), lambda i: (i, 0)),
                  pl.BlockSpec((1, D), lambda i: (0, 0)),
                  pl.BlockSpec(w.shape, lambda i: (0, 0), pipeline_mode=pl.Buffered(1))],
        out_specs=pl.BlockSpec((tm, N), lambda i: (i, 0)),
        out_shape=jax.ShapeDtypeStruct((T, N), out_dtype),
        scratch_shapes=[pltpu.VMEM((tm, D), _BF16)],
        compiler_params=_params(("parallel",)),
        name="norm_proj",
    )(x2d, g, w)


_ML_SHIFT = _LANES // 2


def _attn_kernel(q_ref, k_ref, v_ref, b_ref, o_ref, ml_ref, kpad, vpad, *, L, dil, heads, unroll):
    W = heads * HEAD_DIM
    zeros = jnp.zeros((_SIDE, W), _BF16)
    for r in range(dil):
        for pad, src in ((kpad, k_ref), (vpad, v_ref)):
            pad[r, 0:_SIDE, :] = zeros
            pad[r, _SIDE + L:2 * _SIDE + L, :] = zeros
            pad[r, _SIDE:_SIDE + L, :] = src[r]

    lane = lax.broadcasted_iota(jnp.int32, (_QBLK, _LANES), 1)
    nblk = L // _QBLK
    assert nblk & (nblk - 1) == 0
    blk_shift = nblk.bit_length() - 1

    def body(n, carry):
        r = lax.shift_right_logical(n, blk_shift)
        blk = n & (nblk - 1)
        qs = pl.multiple_of(blk * _QBLK, _QBLK)
        edge = jnp.where(blk == 0, 1, 0) + jnp.where(blk == nblk - 1, 2, 0)
        ml_tile = jnp.zeros((_QBLK, _LANES), _F32)
        for h in range(heads):
            cols = _lane_block(h)
            q = q_ref[r, pl.ds(qs, _QBLK), cols]
            kw = kpad[r, pl.ds(qs, _KWIN), cols]
            vw = vpad[r, pl.ds(qs, _KWIN), cols]
            s = lax.dot_general(q, kw, (((1,), (1,)), ((), ())),
                                preferred_element_type=_F32) + b_ref[edge, h]
            m = jnp.max(s, axis=-1, keepdims=True)
            p = jnp.exp2(s - m)
            l = jnp.sum(p, axis=-1, keepdims=True)
            o_ref[r, pl.ds(qs, _QBLK), cols] = jnp.dot(p.astype(_BF16), vw,
                                                       preferred_element_type=_F32)
            ml_tile = jnp.where(lane == h, m, jnp.where(lane == _ML_SHIFT + h, l, ml_tile))
        ml_ref[r, pl.ds(qs, _QBLK), :] = ml_tile
        return carry

    lax.fori_loop(0, dil * nblk, body, 0, unroll=unroll)


def _attn_heads_per_step(S):
    return HEADS_PER_GROUP if S <= 2048 else HEADS_PER_GROUP // 2


def _attn_group(zg, bias, group, dil, B, S):
    L = S // dil
    assert L % _QBLK == 0
    heads = _attn_heads_per_step(S)
    steps = HEADS_PER_GROUP // heads
    W = heads * HEAD_DIM
    zg = zg.reshape(B, dil, L, _GROUP_QKV_W)
    col = lambda part: (lambda b, hp: (b, 0, 0, part * steps + hp))
    return pl.pallas_call(
        functools.partial(_attn_kernel, L=L, dil=dil, heads=heads,
                          unroll=min(64 // heads, dil * L // _QBLK)),
        grid=(B, steps),
        in_specs=[pl.BlockSpec((None, dil, L, W), col(0)),
                  pl.BlockSpec((None, dil, L, W), col(1)),
                  pl.BlockSpec((None, dil, L, W), col(2)),
                  pl.BlockSpec((_EDGE_VARIANTS, heads, _QBLK, _KWIN),
                               lambda b, hp: (0, group * steps + hp, 0, 0))],
        out_specs=[pl.BlockSpec((None, dil, L, W), lambda b, hp: (b, 0, 0, hp)),
                   pl.BlockSpec((None, dil, L, _LANES), lambda b, hp: (b, 0, 0, hp))],
        out_shape=[jax.ShapeDtypeStruct((B, dil, L, ATTN_OUT_W), _F32),
                   jax.ShapeDtypeStruct((B, dil, L, steps * _LANES), _F32)],
        scratch_shapes=[pltpu.VMEM((dil, L + 2 * _SIDE, W), _BF16),
                        pltpu.VMEM((dil, L + 2 * _SIDE, W), _BF16)],
        compiler_params=_params(("parallel", "parallel")),
        name=f"dilated_attn_g{group}",
    )(zg, zg, zg, bias)


_CHUNKS = 4 * _SUBLANES
_PITCH_PAD = 4
_GATE_TAUS = 8
_SCAN_UNROLL = 16
_F32_TINY = float(np.finfo(np.float32).tiny)


def _sigmoid(x):
    return 0.5 * jnp.tanh(0.5 * x) + 0.5


def _gelu_tanh(x):
    k = math.sqrt(2.0 / math.pi)
    half = 0.5 * x
    return half * jnp.tanh(x * (k + (k * 0.044715) * (x * x))) + half


def _chunk_rows(ref, start, pitch):
    return jnp.concatenate(
        [ref[pl.ds(start + half * _SUBLANES * pitch, _SUBLANES, stride=pitch), :]
         for half in range(_CHUNKS // _SUBLANES)], axis=0)


def _rglru_kernel(rx_ref, ry_ref, cw_ref, cb_ref, wg_ref, bg_ref, lam_ref, o_ref,
                  xs, a_f, h_f, a_b, h_b, *, S):
    BW = LRU_BLOCK_W
    Lc = S // _CHUNKS
    P = Lc + _PITCH_PAD
    left = CONV_W // 2

    for c in range(_CHUNKS):
        lo = c * Lc - left
        hi = lo + P
        s_lo, s_hi = max(lo, 0), min(hi, S)
        xs[c * P + (s_lo - lo):c * P + (s_hi - lo), :] = rx_ref[s_lo:s_hi, :]
        if lo < 0:
            xs[c * P:c * P - lo, :] = jnp.zeros((-lo, BW), _F32)
        if hi > S:
            xs[c * P + (S - lo):(c + 1) * P, :] = jnp.zeros((hi - S, BW), _F32)

    cw = 0.5 * cw_ref[...]
    cb = 0.5 * cb_ref[...]
    half_decay = (-0.5 * LRU_C) * jax.nn.softplus(-lam_ref[...])
    TB = _GATE_TAUS

    def gate_body(bi, carry):
        t0 = bi * TB
        taps = [_chunk_rows(xs, t0 + j, P) for j in range(TB + CONV_W - 1)]
        rows = []
        for tau in range(TB):
            xc = cb
            for t in range(CONV_W):
                xc = xc + taps[tau + t] * cw[t:t + 1]
            rows.append(xc)
        half_x = jnp.concatenate(rows, axis=0)
        g = jnp.tanh(jnp.dot(half_x.astype(_BF16), wg_ref[...], preferred_element_type=_F32)
                     + bg_ref[...])
        r0 = pl.multiple_of(t0 * _CHUNKS, TB * _CHUNKS)
        for d, (a_ref, u_ref) in enumerate(((a_f, h_f), (a_b, h_b))):
            log_a = half_decay[d:d + 1] * g[:, (2 * d) * BW:(2 * d + 1) * BW] + half_decay[d:d + 1]
            ix = (g[:, (2 * d + 1) * BW:(2 * d + 2) * BW] + 1.0) * half_x
            a = jnp.exp(log_a)
            om = jnp.tanh(log_a) * (-1.0 - a * a)
            root = om * lax.rsqrt(jnp.maximum(om, _F32_TINY))
            a_ref[pl.ds(r0, TB * _CHUNKS), :] = a
            u_ref[pl.ds(r0, TB * _CHUNKS), :] = root * ix
        return carry

    lax.fori_loop(0, Lc // TB, gate_body, 0, unroll=4)

    def scan_body(tau, carry):
        hf, pf, hb, pb = carry
        rf = pl.multiple_of(tau * _CHUNKS, _CHUNKS)
        rb = pl.multiple_of((Lc - 1 - tau) * _CHUNKS, _CHUNKS)
        af = a_f[pl.ds(rf, _CHUNKS), :]
        hf = af * hf + h_f[pl.ds(rf, _CHUNKS), :]
        pf = af * pf
        h_f[pl.ds(rf, _CHUNKS), :] = hf
        a_f[pl.ds(rf, _CHUNKS), :] = pf
        ab = a_b[pl.ds(rb, _CHUNKS), :]
        hb = ab * hb + h_b[pl.ds(rb, _CHUNKS), :]
        pb = ab * pb
        h_b[pl.ds(rb, _CHUNKS), :] = hb
        a_b[pl.ds(rb, _CHUNKS), :] = pb
        return hf, pf, hb, pb

    zero = jnp.zeros((_CHUNKS, BW), _F32)
    one = jnp.ones((_CHUNKS, BW), _F32)
    h_end, p_end, h_beg, p_beg = lax.fori_loop(0, Lc, scan_body, (zero, one, zero, one),
                                               unroll=_SCAN_UNROLL)

    row = lax.broadcasted_iota(jnp.int32, (_CHUNKS, BW), 0)
    in_f, in_b = zero, zero
    prev_f = jnp.zeros((1, BW), _F32)
    prev_b = jnp.zeros((1, BW), _F32)
    for c in range(1, _CHUNKS):
        prev_f = h_end[c - 1:c] + p_end[c - 1:c] * prev_f
        in_f = jnp.where(row == c, prev_f, in_f)
        cb_ = _CHUNKS - 1 - c
        prev_b = h_beg[cb_ + 1:cb_ + 2] + p_beg[cb_ + 1:cb_ + 2] * prev_b
        in_b = jnp.where(row == cb_, prev_b, in_b)

    def fix_body(tau, carry):
        r = pl.multiple_of(tau * _CHUNKS, _CHUNKS)
        h = (h_f[pl.ds(r, _CHUNKS), :] + a_f[pl.ds(r, _CHUNKS), :] * in_f
             + h_b[pl.ds(r, _CHUNKS), :] + a_b[pl.ds(r, _CHUNKS), :] * in_b)
        for half in range(_CHUNKS // _SUBLANES):
            xs[pl.ds(tau + half * _SUBLANES * P, _SUBLANES, stride=P), :] = \
                h[half * _SUBLANES:(half + 1) * _SUBLANES]
        return carry

    lax.fori_loop(0, Lc, fix_body, 0, unroll=_SCAN_UNROLL)

    for c in range(_CHUNKS):
        t = slice(c * Lc, (c + 1) * Lc)
        o_ref[t, :] = (xs[c * P:c * P + Lc, :] * _gelu_tanh(ry_ref[t, :])).astype(o_ref.dtype)


def _rglru(zr, rx_col, ry_col, conv_w, conv_b, wg, bg, lam):
    B, S, _ = zr.shape
    BW = LRU_BLOCK_W
    Lc = S // _CHUNKS
    assert S % (_CHUNKS * _GATE_TAUS) == 0 and Lc % _SCAN_UNROLL == 0
    assert rx_col % BW == 0 and ry_col % BW == 0
    seq = pltpu.VMEM((S, BW), _F32)
    out = pl.pallas_call(
        functools.partial(_rglru_kernel, S=S),
        grid=(B, LRU_BLOCKS),
        in_specs=[pl.BlockSpec((None, S, BW), lambda b, n: (b, 0, rx_col // BW + n)),
                  pl.BlockSpec((None, S, BW), lambda b, n: (b, 0, ry_col // BW + n)),
                  pl.BlockSpec((CONV_W, BW), lambda b, n: (0, n)),
                  pl.BlockSpec((1, BW), lambda b, n: (0, n)),
                  pl.BlockSpec((None, BW, 4 * BW), lambda b, n: (n, 0, 0)),
                  pl.BlockSpec((None, 1, 4 * BW), lambda b, n: (n, 0, 0)),
                  pl.BlockSpec((2, BW), lambda b, n: (0, n))],
        out_specs=pl.BlockSpec((None, S, BW), lambda b, n: (b, 0, n)),
        out_shape=jax.ShapeDtypeStruct((B, S, LRU_W), _BF16),
        scratch_shapes=[pltpu.VMEM((_CHUNKS * (Lc + _PITCH_PAD), BW), _F32), seq, seq, seq, seq],
        compiler_params=_params(("parallel", "parallel")),
        name="conv_rglru",
    )(zr, zr, conv_w, conv_b, wg, bg, lam)
    return out.reshape(B * S, LRU_W)


def _unfold(src_ref, dst_ref, dil):
    rows = src_ref.shape[1]
    for r in range(dil):
        for c in range(dst_ref.shape[0]):
            dst_ref[c, pl.ds(r, rows, stride=dil), :] = src_ref[r, :, _lane_block(c)]


def _merge_kernel(*refs, heads):
    o_refs = refs[0:N_GROUPS]
    l_refs = refs[N_GROUPS:2 * N_GROUPS]
    rnn_ref, ga_ref, gr_ref, x_ref, wa_ref, wr_ref, wo_ref, out_ref = refs[2 * N_GROUPS:2 * N_GROUPS + 8]
    scratch = refs[2 * N_GROUPS + 8:]

    pr = jnp.dot(rnn_ref[...], wr_ref[...], preferred_element_type=_F32)

    o_get, l_get, k = [], [], 0
    for g, (_, dil) in enumerate(DILATED_GROUPS):
        if dil == 1:
            o_get.append(lambda c, ref=o_refs[g]: ref[:, _lane_block(c)])
            l_get.append(lambda c, ref=l_refs[g]: ref[:, _lane_block(c)])
        else:
            o_scr, l_scr = scratch[k], scratch[k + 1]
            k += 2
            _unfold(o_refs[g], o_scr, dil)
            _unfold(l_refs[g], l_scr, dil)
            o_get.append(lambda c, ref=o_scr: ref[c])
            l_get.append(lambda c, ref=l_scr: ref[c])

    wts = []
    for step in range(HEADS_PER_GROUP // heads):
        ml = [get(step) for get in l_get]
        mx = functools.reduce(jnp.maximum, ml)
        es = [jnp.exp2(t - mx) for t in ml]
        den = functools.reduce(jnp.add, [e * pltpu.roll(t, _ML_SHIFT, axis=1)
                                         for e, t in zip(es, ml)])
        inv = 1.0 / den
        wts.append([e * inv for e in es])
    attn = []
    for h in range(HEADS_PER_GROUP):
        w = wts[h // heads]
        c = h % heads
        attn.append(functools.reduce(
            jnp.add, [w[g][:, c:c + 1] * o_get[g](h) for g in range(N_GROUPS)]))
    attn = jnp.concatenate(attn, axis=-1).astype(_BF16)
    pa = jnp.dot(attn, wa_ref[...], preferred_element_type=_F32)
    merged = _sigmoid(ga_ref[...]) * pa + _sigmoid(gr_ref[...]) * pr
    out_ref[...] = x_ref[...] + jnp.dot(merged.astype(_BF16), wo_ref[...],
                                        preferred_element_type=_F32)


def _merge(os_, lses, rnn, zr, ga_col, gr_col, x2d, wa, wr, wo, B, S, *, tm):
    T, D = x2d.shape
    assert S % tm == 0 and ga_col % D == 0 and gr_col % D == 0
    heads = _attn_heads_per_step(S)
    LW = (HEADS_PER_GROUP // heads) * _LANES
    tiles_per_seq = S // tm
    row = lambda w: pl.BlockSpec((tm, w), lambda i: (i, 0))
    const = lambda a: pl.BlockSpec(a.shape, lambda i: (0, 0), pipeline_mode=pl.Buffered(1))

    def group_spec(dil, w):
        if dil == 1:
            return row(w)
        assert tm % (dil * _SUBLANES) == 0
        return pl.BlockSpec((None, dil, tm // dil, w),
                            lambda i: (i // tiles_per_seq, 0, i % tiles_per_seq, 0))

    scratch = []
    for _, dil in DILATED_GROUPS:
        if dil > 1:
            scratch += [pltpu.VMEM((ATTN_OUT_W // _LANES, tm, _LANES), _F32),
                        pltpu.VMEM((LW // _LANES, tm, _LANES), _F32)]
    os_ = [o.reshape(T, ATTN_OUT_W) if dil == 1 else o for o, (_, dil) in zip(os_, DILATED_GROUPS)]
    lses = [l.reshape(T, LW) if dil == 1 else l for l, (_, dil) in zip(lses, DILATED_GROUPS)]
    return pl.pallas_call(
        functools.partial(_merge_kernel, heads=heads),
        grid=(T // tm,),
        in_specs=[group_spec(dil, ATTN_OUT_W) for _, dil in DILATED_GROUPS]
        + [group_spec(dil, LW) for _, dil in DILATED_GROUPS]
        + [row(LRU_W),
           pl.BlockSpec((tm, D), lambda i: (i, ga_col // D)),
           pl.BlockSpec((tm, D), lambda i: (i, gr_col // D)),
           row(D), const(wa), const(wr), const(wo)],
        out_specs=row(D),
        out_shape=jax.ShapeDtypeStruct((T, D), _F32),
        scratch_shapes=scratch,
        compiler_params=_params(("parallel",)),
        name="merge_proj",
    )(*os_, *lses, rnn, zr, zr, x2d, wa, wr, wo)


def _mlp_kernel(x_ref, g_ref, w1_ref, w2_ref, gf_ref, o_ref, u_ref):
    j = pl.program_id(1)

    @pl.when(j == 0)
    def _():
        x = x_ref[...]
        u_ref[...] = _rms(x, g_ref[...]).astype(u_ref.dtype)
        o_ref[...] = x

    h = jnp.dot(u_ref[...], w1_ref[...], preferred_element_type=_F32)
    h = jnp.square(jnp.maximum(h, 0.0)).astype(_BF16)
    o_ref[...] += jnp.dot(h, w2_ref[...], preferred_element_type=_F32)

    @pl.when(j == pl.num_programs(1) - 1)
    def _():
        o_ref[...] = _rms(o_ref[...], gf_ref[...])


def _mlp(x2d, g, w1, w2, gf, *, tm, tf):
    T, D = x2d.shape
    F = w1.shape[1]
    assert T % tm == 0 and F % tf == 0
    return pl.pallas_call(
        _mlp_kernel,
        grid=(T // tm, F // tf),
        in_specs=[pl.BlockSpec((tm, D), lambda i, j: (i, 0)),
                  pl.BlockSpec((1, D), lambda i, j: (0, 0)),
                  pl.BlockSpec((D, tf), lambda i, j: (0, j)),
                  pl.BlockSpec((tf, D), lambda i, j: (j, 0)),
                  pl.BlockSpec((1, D), lambda i, j: (0, 0))],
        out_specs=pl.BlockSpec((tm, D), lambda i, j: (i, 0)),
        out_shape=jax.ShapeDtypeStruct((T, D), _F32),
        scratch_shapes=[pltpu.VMEM((tm, D), _BF16)],
        compiler_params=_params(("parallel", "arbitrary")),
        name="mlp_final_norm",
    )(x2d, g, w1, w2, gf)


_REST_GA = 0
_REST_GR = D_MODEL
_REST_RX = 2 * D_MODEL
_REST_RY = 2 * D_MODEL + LRU_W


def _layer(x, bias_tabs, p):
    B, S, D = x.shape
    T = B * S
    x2d = x.reshape(T, D)
    zgs = _qkv_proj(x2d, p["g_mix"], p["w_qkv"], B, S, tm=512)
    zr = _norm_proj(x2d, p["g_mix"], p["w_rest"], tm=256, tn=1792, out_dtype=_F32)
    os_, lses = [], []
    for g, (_, dil) in enumerate(DILATED_GROUPS):
        o, lse = _attn_group(zgs[g], bias_tabs, g, dil, B, S)
        os_.append(o)
        lses.append(lse)
    rnn = _rglru(zr.reshape(B, S, -1), _REST_RX, _REST_RY, p["conv_w"], p["conv_b"],
                 p["w_gate"], p["b_gate"], p["lam"])
    x2 = _merge(os_, lses, rnn, zr, _REST_GA, _REST_GR, x2d,
                p["w_attn_o"], p["w_rnn_o"], p["w_out"], B, S, tm=256)
    y = _mlp(x2, p["g_mlp"], p["w_mlp_in"], p["w_mlp_out"], p["g_final"], tm=512, tf=2048)
    return y.reshape(B, S, D)


def kernel(x_prompt, x_sample, rel_bias, norm_mix_g, w_in, conv_w, conv_b, lru_wa, lru_ba,
           lru_wx, lru_bx, lru_lambda, w_attn_o, w_rnn_o, w_out, norm_mlp_g, w_mlp_in,
           w_mlp_out, norm_final_g):
    depth = w_in.shape[0]
    assert depth == 1, "the final norm is fused into the single layer's MLP kernel"
    bias_tabs = _bias_tables(rel_bias)
    q_end, rx_end, ry_end, ga_end = 3 * ATTN_W, 3 * ATTN_W + LRU_W, 3 * ATTN_W + 2 * LRU_W, \
        3 * ATTN_W + 2 * LRU_W + D_MODEL
    wl = w_in[0]
    w_qkv = jnp.concatenate(
        [wl[:, part * ATTN_W + g * ATTN_OUT_W:part * ATTN_W + (g + 1) * ATTN_OUT_W]
         for g in range(N_GROUPS) for part in range(3)], axis=-1).astype(_BF16)
    w_gate = jnp.concatenate([lru_wa[0, 0], lru_wx[0, 0], lru_wa[0, 1], lru_wx[0, 1]],
                             axis=-1).astype(_BF16)
    b_gate = jnp.concatenate(
        [b.reshape(LRU_BLOCKS, 1, LRU_BLOCK_W)
         for b in (lru_ba[0, 0], lru_bx[0, 0], lru_ba[0, 1], lru_bx[0, 1])], axis=-1)
    p = dict(
        g_mix=norm_mix_g[0].reshape(1, D_MODEL),
        w_qkv=w_qkv,
        w_rest=jnp.concatenate([wl[:, ry_end:ga_end], wl[:, ga_end:], wl[:, q_end:rx_end],
                                wl[:, rx_end:ry_end]], axis=-1).astype(_BF16),
        conv_w=conv_w[0], conv_b=conv_b[0].reshape(1, LRU_W),
        w_gate=w_gate, b_gate=0.5 * b_gate.astype(_F32), lam=lru_lambda[0],
        w_attn_o=w_attn_o[0].astype(_BF16), w_rnn_o=w_rnn_o[0].astype(_BF16),
        w_out=w_out[0].astype(_BF16),
        g_mlp=norm_mlp_g[0].reshape(1, D_MODEL),
        w_mlp_in=w_mlp_in[0].astype(_BF16), w_mlp_out=w_mlp_out[0].astype(_BF16),
        g_final=norm_final_g.reshape(1, D_MODEL),
    )
    return (_layer(x_prompt, bias_tabs, p), _layer(x_sample, bias_tabs, p))
```

```python
import functools
import math

import jax
import jax.numpy as jnp
import numpy as np
from jax import lax
from jax.experimental import pallas as pl
from jax.experimental.pallas import tpu as pltpu

D_MODEL = 2048
HEAD_DIM = 128
HEADS_PER_GROUP = 4
DILATED_GROUPS = ((128, 1), (512, 4), (2048, 16))
N_GROUPS = len(DILATED_GROUPS)
ATTN_W = N_GROUPS * HEADS_PER_GROUP * HEAD_DIM
ATTN_OUT_W = HEADS_PER_GROUP * HEAD_DIM
NUM_BUCKETS = 32
REL_MAX_DIST = 1024
LRU_W = 1536
LRU_BLOCKS = 12
LRU_BLOCK_W = LRU_W // LRU_BLOCKS
LRU_C = 8.0
CONV_W = 4
D_FF = 4 * D_MODEL
NORM_EPS = 1e-6
NEG_INF = -1e30

_F32 = jnp.float32
_BF16 = jnp.bfloat16
_V7X_VMEM_LIMIT_BYTES = 56 * 1024 * 1024
_LANES = 128
_SUBLANES = 8

_SIDE = DILATED_GROUPS[0][0] // (2 * DILATED_GROUPS[0][1])
assert all(w // (2 * d) == _SIDE for w, d in DILATED_GROUPS)
_QBLK = 2 * _SIDE
_KWIN = 4 * _SIDE
assert _QBLK == _LANES
_GROUP_QKV_W = 3 * ATTN_OUT_W


def _params(semantics):
    return pltpu.CompilerParams(dimension_semantics=semantics,
                                vmem_limit_bytes=_V7X_VMEM_LIMIT_BYTES)


def _rms(x, g):
    ms = jnp.mean(x * x, axis=-1, keepdims=True)
    return x * lax.rsqrt(ms + NORM_EPS) * g


def _lane_block(c):
    return slice(c * _LANES, (c + 1) * _LANES)


_LOG2E = math.log2(math.e)
_EDGE_VARIANTS = 4


def _rel_bucket_np(rel):
    half = NUM_BUCKETS // 2
    max_exact = half // 2
    n = np.abs(rel)
    nf = np.maximum(n, 1).astype(np.float32)
    ratio = np.log(nf / np.float32(max_exact)) / np.float32(math.log(REL_MAX_DIST / max_exact))
    large = max_exact + (ratio * np.float32(half - max_exact)).astype(np.int32)
    large = np.minimum(large, half - 1)
    return (np.where(rel > 0, half, 0) + np.where(n < max_exact, n, large)).astype(np.int32)


def _bias_kernel(tab_ref, idx_ref, o_ref):
    head = pl.program_id(0)
    idx = idx_ref[...]
    acc = jnp.zeros(idx.shape, _F32)
    for b in range(NUM_BUCKETS):
        acc = jnp.where(idx == b, tab_ref[b, head], acc)
    acc = acc * _LOG2E
    kj = lax.broadcasted_iota(jnp.int32, idx.shape, 1)
    in_band = idx >= 0
    after_start = kj >= _SIDE
    before_end = kj < _KWIN - _SIDE
    o_ref[0] = jnp.where(in_band, acc, NEG_INF)
    o_ref[1] = jnp.where(jnp.logical_and(in_band, after_start), acc, NEG_INF)
    o_ref[2] = jnp.where(jnp.logical_and(in_band, before_end), acc, NEG_INF)
    o_ref[3] = jnp.where(jnp.logical_and(in_band, jnp.logical_and(after_start, before_end)),
                         acc, NEG_INF)


def _bias_tables(rel_bias):
    qi = np.arange(_QBLK, dtype=np.int32)[:, None]
    kj = np.arange(_KWIN, dtype=np.int32)[None, :]
    rel = kj - _SIDE - qi
    band = np.abs(rel) <= _SIDE
    idx = np.stack([np.where(band, _rel_bucket_np(rel * dil), -1) for _, dil in DILATED_GROUPS])
    n_heads = N_GROUPS * HEADS_PER_GROUP
    return pl.pallas_call(
        _bias_kernel,
        grid=(n_heads,),
        in_specs=[pl.BlockSpec(memory_space=pltpu.SMEM),
                  pl.BlockSpec((None, _QBLK, _KWIN), lambda h: (h // HEADS_PER_GROUP, 0, 0))],
        out_specs=pl.BlockSpec((_EDGE_VARIANTS, None, _QBLK, _KWIN), lambda h: (0, h, 0, 0)),
        out_shape=jax.ShapeDtypeStruct((_EDGE_VARIANTS, n_heads, _QBLK, _KWIN), _F32),
        name="bias_tables",
    )(rel_bias.astype(_F32), jnp.asarray(idx.astype(np.int32)))


_FOLD_STRIDE = 4
_Q_SCALE = HEAD_DIM ** -0.5 * _LOG2E


def _qkv_proj_kernel(x_ref, g_ref, w_ref, *refs):
    out_refs = refs[:N_GROUPS]
    u_ref, stage_ref, stage2_ref = refs[N_GROUPS:]
    tm = x_ref.shape[0]
    n_blocks = ATTN_OUT_W // _LANES
    u_ref[...] = _rms(x_ref[...], g_ref[...]).astype(u_ref.dtype)

    def fold(o_ref, dil, part, slot):
        for c in range(n_blocks):
            lanes = _lane_block(part * n_blocks + c)
            if dil == _FOLD_STRIDE:
                for r in range(dil):
                    o_ref[r, :, lanes] = stage_ref[
                        slot, c, pl.ds(r, tm // dil, stride=dil), :].astype(o_ref.dtype)
                continue
            assert dil == _FOLD_STRIDE ** 2
            for r0 in range(_FOLD_STRIDE):
                stage2_ref[c, r0] = stage_ref[
                    slot, c, pl.ds(r0, tm // _FOLD_STRIDE, stride=_FOLD_STRIDE), :]
            for r0 in range(_FOLD_STRIDE):
                for r1 in range(_FOLD_STRIDE):
                    o_ref[r0 + _FOLD_STRIDE * r1, :, lanes] = stage2_ref[
                        c, r0, pl.ds(r1, tm // dil, stride=_FOLD_STRIDE), :].astype(o_ref.dtype)

    pending, slot = None, 0
    for g, (_, dil) in sorted(enumerate(DILATED_GROUPS), key=lambda e: -e[1][1]):
        o_ref = out_refs[g]
        for part in range(3):
            col0 = g * _GROUP_QKV_W + part * ATTN_OUT_W
            res = jnp.dot(u_ref[...], w_ref[:, col0:col0 + ATTN_OUT_W],
                          preferred_element_type=_F32)
            if part == 0:
                res = res * _Q_SCALE
            if dil == 1:
                o_ref[:, part * ATTN_OUT_W:(part + 1) * ATTN_OUT_W] = res.astype(o_ref.dtype)
            else:
                for c in range(n_blocks):
                    stage_ref[slot, c] = res[:, _lane_block(c)]
            if pending is not None:
                fold(*pending)
                pending = None
            if dil > 1:
                pending = (o_ref, dil, part, slot)
                slot = 1 - slot
    if pending is not None:
        fold(*pending)


def _qkv_proj(x2d, g, w, B, S, *, tm):
    T, D = x2d.shape
    assert T == B * S and S % tm == 0
    tiles_per_seq = S // tm
    out_shape, out_specs = [], []
    for _, dil in DILATED_GROUPS:
        assert tm % (dil * 16) == 0
        if dil == 1:
            out_shape.append(jax.ShapeDtypeStruct((T, _GROUP_QKV_W), _BF16))
            out_specs.append(pl.BlockSpec((tm, _GROUP_QKV_W), lambda i: (i, 0)))
        else:
            out_shape.append(jax.ShapeDtypeStruct((B, dil, S // dil, _GROUP_QKV_W), _BF16))
            out_specs.append(pl.BlockSpec(
                (None, dil, tm // dil, _GROUP_QKV_W),
                lambda i: (i // tiles_per_seq, 0, i % tiles_per_seq, 0)))
    return pl.pallas_call(
        _qkv_proj_kernel,
        grid=(T // tm,),
        in_specs=[pl.BlockSpec((tm, D), lambda i: (i, 0)),
                  pl.BlockSpec((1, D), lambda i: (0, 0)),
                  pl.BlockSpec(w.shape, lambda i: (0, 0), pipeline_mode=pl.Buffered(1))],
        out_specs=out_specs,
        out_shape=out_shape,
        scratch_shapes=[pltpu.VMEM((tm, D), _BF16),
                        pltpu.VMEM((2, ATTN_OUT_W // _LANES, tm, _LANES), _F32),
                        pltpu.VMEM((ATTN_OUT_W // _LANES, _FOLD_STRIDE, tm // _FOLD_STRIDE, _LANES),
                                   _F32)],
        compiler_params=_params(("parallel",)),
        name="qkv_proj",
    )(x2d, g, w)


def _norm_proj_kernel(x_ref, g_ref, w_ref, o_ref, u_ref, *, tn):
    u_ref[...] = _rms(x_ref[...], g_ref[...]).astype(u_ref.dtype)
    for c in range(w_ref.shape[1] // tn):
        cols = slice(c * tn, (c + 1) * tn)
        o_ref[:, cols] = jnp.dot(u_ref[...], w_ref[:, cols],
                                 preferred_element_type=_F32).astype(o_ref.dtype)


def _norm_proj(x2d, g, w, *, tm, tn, out_dtype):
    T, D = x2d.shape
    N = w.shape[1]
    assert T % tm == 0 and N % tn == 0
    return pl.pallas_call(
        functools.partial(_norm_proj_kernel, tn=tn),
        grid=(T // tm,),
        in_specs=[pl.BlockSpec((tm, D), lambda i: (i, 0)),
                  pl.BlockSpec((1, D), lambda i: (0, 0)),
                  pl.BlockSpec(w.shape, lambda i: (0, 0), pipeline_mode=pl.Buffered(1))],
        out_specs=pl.BlockSpec((tm, N), lambda i: (i, 0)),
        out_shape=jax.ShapeDtypeStruct((T, N), out_dtype),
        scratch_shapes=[pltpu.VMEM((tm, D), _BF16)],
        compiler_params=_params(("parallel",)),
        name="norm_proj",
    )(x2d, g, w)


_ML_SHIFT = _LANES // 2


def _attn_kernel(q_ref, k_ref, v_ref, b_ref, o_ref, ml_ref, kpad, vpad, *, L, dil, heads, unroll):
    W = heads * HEAD_DIM
    zeros = jnp.zeros((_SIDE, W), _BF16)
    for r in range(dil):
        for pad, src in ((kpad, k_ref), (vpad, v_ref)):
            pad[r, 0:_SIDE, :] = zeros
            pad[r, _SIDE + L:2 * _SIDE + L, :] = zeros
            pad[r, _SIDE:_SIDE + L, :] = src[r]

    lane = lax.broadcasted_iota(jnp.int32, (_QBLK, _LANES), 1)
    nblk = L // _QBLK
    assert nblk & (nblk - 1) == 0
    blk_shift = nblk.bit_length() - 1

    def body(n, carry):
        r = lax.shift_right_logical(n, blk_shift)
        blk = n & (nblk - 1)
        qs = pl.multiple_of(blk * _QBLK, _QBLK)
        edge = jnp.where(blk == 0, 1, 0) + jnp.where(blk == nblk - 1, 2, 0)
        ml_tile = jnp.zeros((_QBLK, _LANES), _F32)
        for h in range(heads):
            cols = _lane_block(h)
            q = q_ref[r, pl.ds(qs, _QBLK), cols]
            kw = kpad[r, pl.ds(qs, _KWIN), cols]
            vw = vpad[r, pl.ds(qs, _KWIN), cols]
            s = lax.dot_general(q, kw, (((1,), (1,)), ((), ())),
                                preferred_element_type=_F32) + b_ref[edge, h]
            m = jnp.max(s, axis=-1, keepdims=True)
            p = jnp.exp2(s - m)
            l = jnp.sum(p, axis=-1, keepdims=True)
            o_ref[r, pl.ds(qs, _QBLK), cols] = jnp.dot(p.astype(_BF16), vw,
                                                       preferred_element_type=_F32)
            ml_tile = jnp.where(lane == h, m, jnp.where(lane == _ML_SHIFT + h, l, ml_tile))
        ml_ref[r, pl.ds(qs, _QBLK), :] = ml_tile
        return carry

    lax.fori_loop(0, dil * nblk, body, 0, unroll=unroll)


def _attn_heads_per_step(S):
    return HEADS_PER_GROUP if S <= 2048 else HEADS_PER_GROUP // 2


def _attn_group(zg, bias, group, dil, B, S):
    L = S // dil
    assert L % _QBLK == 0
    heads = _attn_heads_per_step(S)
    steps = HEADS_PER_GROUP // heads
    W = heads * HEAD_DIM
    zg = zg.reshape(B, dil, L, _GROUP_QKV_W)
    col = lambda part: (lambda b, hp: (b, 0, 0, part * steps + hp))
    return pl.pallas_call(
        functools.partial(_attn_kernel, L=L, dil=dil, heads=heads,
                          unroll=min(64 // heads, dil * L // _QBLK)),
        grid=(B, steps),
        in_specs=[pl.BlockSpec((None, dil, L, W), col(0)),
                  pl.BlockSpec((None, dil, L, W), col(1)),
                  pl.BlockSpec((None, dil, L, W), col(2)),
                  pl.BlockSpec((_EDGE_VARIANTS, heads, _QBLK, _KWIN),
                               lambda b, hp: (0, group * steps + hp, 0, 0))],
        out_specs=[pl.BlockSpec((None, dil, L, W), lambda b, hp: (b, 0, 0, hp)),
                   pl.BlockSpec((None, dil, L, _LANES), lambda b, hp: (b, 0, 0, hp))],
        out_shape=[jax.ShapeDtypeStruct((B, dil, L, ATTN_OUT_W), _F32),
                   jax.ShapeDtypeStruct((B, dil, L, steps * _LANES), _F32)],
        scratch_shapes=[pltpu.VMEM((dil, L + 2 * _SIDE, W), _BF16),
                        pltpu.VMEM((dil, L + 2 * _SIDE, W), _BF16)],
        compiler_params=_params(("parallel", "parallel")),
        name=f"dilated_attn_g{group}",
    )(zg, zg, zg, bias)


_CHUNKS = 4 * _SUBLANES
_PITCH_PAD = 4
_GATE_TAUS = 8
_SCAN_UNROLL = 16
_F32_TINY = float(np.finfo(np.float32).tiny)


def _sigmoid(x):
    return 0.5 * jnp.tanh(0.5 * x) + 0.5


def _gelu_tanh(x):
    k = math.sqrt(2.0 / math.pi)
    half = 0.5 * x
    return half * jnp.tanh(x * (k + (k * 0.044715) * (x * x))) + half


def _chunk_rows(ref, start, pitch):
    return jnp.concatenate(
        [ref[pl.ds(start + half * _SUBLANES * pitch, _SUBLANES, stride=pitch), :]
         for half in range(_CHUNKS // _SUBLANES)], axis=0)


def _rglru_kernel(rx_ref, ry_ref, cw_ref, cb_ref, wg_ref, bg_ref, lam_ref, o_ref,
                  xs, a_f, h_f, a_b, h_b, *, S):
    BW = LRU_BLOCK_W
    Lc = S // _CHUNKS
    P = Lc + _PITCH_PAD
    left = CONV_W // 2

    for c in range(_CHUNKS):
        lo = c * Lc - left
        hi = lo + P
        s_lo, s_hi = max(lo, 0), min(hi, S)
        xs[c * P + (s_lo - lo):c * P + (s_hi - lo), :] = rx_ref[s_lo:s_hi, :]
        if lo < 0:
            xs[c * P:c * P - lo, :] = jnp.zeros((-lo, BW), _F32)
        if hi > S:
            xs[c * P + (S - lo):(c + 1) * P, :] = jnp.zeros((hi - S, BW), _F32)

    cw = 0.5 * cw_ref[...]
    cb = 0.5 * cb_ref[...]
    half_decay = (-0.5 * LRU_C) * jax.nn.softplus(-lam_ref[...])
    TB = _GATE_TAUS

    def gate_body(bi, carry):
        t0 = bi * TB
        taps = [_chunk_rows(xs, t0 + j, P) for j in range(TB + CONV_W - 1)]
        rows = []
        for tau in range(TB):
            xc = cb
            for t in range(CONV_W):
                xc = xc + taps[tau + t] * cw[t:t + 1]
            rows.append(xc)
        half_x = jnp.concatenate(rows, axis=0)
        g = jnp.tanh(jnp.dot(half_x.astype(_BF16), wg_ref[...], preferred_element_type=_F32)
                     + bg_ref[...])
        r0 = pl.multiple_of(t0 * _CHUNKS, TB * _CHUNKS)
        for d, (a_ref, u_ref) in enumerate(((a_f, h_f), (a_b, h_b))):
            log_a = half_decay[d:d + 1] * g[:, (2 * d) * BW:(2 * d + 1) * BW] + half_decay[d:d + 1]
            ix = (g[:, (2 * d + 1) * BW:(2 * d + 2) * BW] + 1.0) * half_x
            a = jnp.exp(log_a)
            om = jnp.tanh(log_a) * (-1.0 - a * a)
            root = om * lax.rsqrt(jnp.maximum(om, _F32_TINY))
            a_ref[pl.ds(r0, TB * _CHUNKS), :] = a
            u_ref[pl.ds(r0, TB * _CHUNKS), :] = root * ix
        return carry

    lax.fori_loop(0, Lc // TB, gate_body, 0, unroll=4)

    def scan_body(tau, carry):
        hf, pf, hb, pb = carry
        rf = pl.multiple_of(tau * _CHUNKS, _CHUNKS)
        rb = pl.multiple_of((Lc - 1 - tau) * _CHUNKS, _CHUNKS)
        af = a_f[pl.ds(rf, _CHUNKS), :]
        hf = af * hf + h_f[pl.ds(rf, _CHUNKS), :]
        pf = af * pf
        h_f[pl.ds(rf, _CHUNKS), :] = hf
        a_f[pl.ds(rf, _CHUNKS), :] = pf
        ab = a_b[pl.ds(rb, _CHUNKS), :]
        hb = ab * hb + h_b[pl.ds(rb, _CHUNKS), :]
        pb = ab * pb
        h_b[pl.ds(rb, _CHUNKS), :] = hb
        a_b[pl.ds(rb, _CHUNKS), :] = pb
        return hf, pf, hb, pb

    zero = jnp.zeros((_CHUNKS, BW), _F32)
    one = jnp.ones((_CHUNKS, BW), _F32)
    h_end, p_end, h_beg, p_beg = lax.fori_loop(0, Lc, scan_body, (zero, one, zero, one),
                                               unroll=_SCAN_UNROLL)

    row = lax.broadcasted_iota(jnp.int32, (_CHUNKS, BW), 0)
    in_f, in_b = zero, zero
    prev_f = jnp.zeros((1, BW), _F32)
    prev_b = jnp.zeros((1, BW), _F32)
    for c in range(1, _CHUNKS):
        prev_f = h_end[c - 1:c] + p_end[c - 1:c] * prev_f
        in_f = jnp.where(row == c, prev_f, in_f)
        cb_ = _CHUNKS - 1 - c
        prev_b = h_beg[cb_ + 1:cb_ + 2] + p_beg[cb_ + 1:cb_ + 2] * prev_b
        in_b = jnp.where(row == cb_, prev_b, in_b)

    def fix_body(tau, carry):
        r = pl.multiple_of(tau * _CHUNKS, _CHUNKS)
        h = (h_f[pl.ds(r, _CHUNKS), :] + a_f[pl.ds(r, _CHUNKS), :] * in_f
             + h_b[pl.ds(r, _CHUNKS), :] + a_b[pl.ds(r, _CHUNKS), :] * in_b)
        for half in range(_CHUNKS // _SUBLANES):
            xs[pl.ds(tau + half * _SUBLANES * P, _SUBLANES, stride=P), :] = \
                h[half * _SUBLANES:(half + 1) * _SUBLANES]
        return carry

    lax.fori_loop(0, Lc, fix_body, 0, unroll=_SCAN_UNROLL)

    for c in range(_CHUNKS):
        t = slice(c * Lc, (c + 1) * Lc)
        o_ref[t, :] = (xs[c * P:c * P + Lc, :] * _gelu_tanh(ry_ref[t, :])).astype(o_ref.dtype)


def _rglru(zr, rx_col, ry_col, conv_w, conv_b, wg, bg, lam):
    B, S, _ = zr.shape
    BW = LRU_BLOCK_W
    Lc = S // _CHUNKS
    assert S % (_CHUNKS * _GATE_TAUS) == 0 and Lc % _SCAN_UNROLL == 0
    assert rx_col % BW == 0 and ry_col % BW == 0
    seq = pltpu.VMEM((S, BW), _F32)
    out = pl.pallas_call(
        functools.partial(_rglru_kernel, S=S),
        grid=(B, LRU_BLOCKS),
        in_specs=[pl.BlockSpec((None, S, BW), lambda b, n: (b, 0, rx_col // BW + n)),
                  pl.BlockSpec((None, S, BW), lambda b, n: (b, 0, ry_col // BW + n)),
                  pl.BlockSpec((CONV_W, BW), lambda b, n: (0, n)),
                  pl.BlockSpec((1, BW), lambda b, n: (0, n)),
                  pl.BlockSpec((None, BW, 4 * BW), lambda b, n: (n, 0, 0)),
                  pl.BlockSpec((None, 1, 4 * BW), lambda b, n: (n, 0, 0)),
                  pl.BlockSpec((2, BW), lambda b, n: (0, n))],
        out_specs=pl.BlockSpec((None, S, BW), lambda b, n: (b, 0, n)),
        out_shape=jax.ShapeDtypeStruct((B, S, LRU_W), _BF16),
        scratch_shapes=[pltpu.VMEM((_CHUNKS * (Lc + _PITCH_PAD), BW), _F32), seq, seq, seq, seq],
        compiler_params=_params(("parallel", "parallel")),
        name="conv_rglru",
    )(zr, zr, conv_w, conv_b, wg, bg, lam)
    return out.reshape(B * S, LRU_W)


def _unfold(src_ref, dst_ref, dil):
    rows = src_ref.shape[1]
    for r in range(dil):
        for c in range(dst_ref.shape[0]):
            dst_ref[c, pl.ds(r, rows, stride=dil), :] = src_ref[r, :, _lane_block(c)]


def _merge_kernel(*refs, heads):
    o_refs = refs[0:N_GROUPS]
    l_refs = refs[N_GROUPS:2 * N_GROUPS]
    rnn_ref, ga_ref, gr_ref, x_ref, wa_ref, wr_ref, wo_ref, out_ref = refs[2 * N_GROUPS:2 * N_GROUPS + 8]
    scratch = refs[2 * N_GROUPS + 8:]

    pr = jnp.dot(rnn_ref[...], wr_ref[...], preferred_element_type=_F32)

    o_get, l_get, k = [], [], 0
    for g, (_, dil) in enumerate(DILATED_GROUPS):
        if dil == 1:
            o_get.append(lambda c, ref=o_refs[g]: ref[:, _lane_block(c)])
            l_get.append(lambda c, ref=l_refs[g]: ref[:, _lane_block(c)])
        else:
            o_scr, l_scr = scratch[k], scratch[k + 1]
            k += 2
            _unfold(o_refs[g], o_scr, dil)
            _unfold(l_refs[g], l_scr, dil)
            o_get.append(lambda c, ref=o_scr: ref[c])
            l_get.append(lambda c, ref=l_scr: ref[c])

    wts = []
    for step in range(HEADS_PER_GROUP // heads):
        ml = [get(step) for get in l_get]
        mx = functools.reduce(jnp.maximum, ml)
        es = [jnp.exp2(t - mx) for t in ml]
        den = functools.reduce(jnp.add, [e * pltpu.roll(t, _ML_SHIFT, axis=1)
                                         for e, t in zip(es, ml)])
        inv = 1.0 / den
        wts.append([e * inv for e in es])
    attn = []
    for h in range(HEADS_PER_GROUP):
        w = wts[h // heads]
        c = h % heads
        attn.append(functools.reduce(
            jnp.add, [w[g][:, c:c + 1] * o_get[g](h) for g in range(N_GROUPS)]))
    attn = jnp.concatenate(attn, axis=-1).astype(_BF16)
    pa = jnp.dot(attn, wa_ref[...], preferred_element_type=_F32)
    merged = _sigmoid(ga_ref[...]) * pa + _sigmoid(gr_ref[...]) * pr
    out_ref[...] = x_ref[...] + jnp.dot(merged.astype(_BF16), wo_ref[...],
                                        preferred_element_type=_F32)


def _merge(os_, lses, rnn, zr, ga_col, gr_col, x2d, wa, wr, wo, B, S, *, tm):
    T, D = x2d.shape
    assert S % tm == 0 and ga_col % D == 0 and gr_col % D == 0
    heads = _attn_heads_per_step(S)
    LW = (HEADS_PER_GROUP // heads) * _LANES
    tiles_per_seq = S // tm
    row = lambda w: pl.BlockSpec((tm, w), lambda i: (i, 0))
    const = lambda a: pl.BlockSpec(a.shape, lambda i: (0, 0), pipeline_mode=pl.Buffered(1))

    def group_spec(dil, w):
        if dil == 1:
            return row(w)
        assert tm % (dil * _SUBLANES) == 0
        return pl.BlockSpec((None, dil, tm // dil, w),
                            lambda i: (i // tiles_per_seq, 0, i % tiles_per_seq, 0))

    scratch = []
    for _, dil in DILATED_GROUPS:
        if dil > 1:
            scratch += [pltpu.VMEM((ATTN_OUT_W // _LANES, tm, _LANES), _F32),
                        pltpu.VMEM((LW // _LANES, tm, _LANES), _F32)]
    os_ = [o.reshape(T, ATTN_OUT_W) if dil == 1 else o for o, (_, dil) in zip(os_, DILATED_GROUPS)]
    lses = [l.reshape(T, LW) if dil == 1 else l for l, (_, dil) in zip(lses, DILATED_GROUPS)]
    return pl.pallas_call(
        functools.partial(_merge_kernel, heads=heads),
        grid=(T // tm,),
        in_specs=[group_spec(dil, ATTN_OUT_W) for _, dil in DILATED_GROUPS]
        + [group_spec(dil, LW) for _, dil in DILATED_GROUPS]
        + [row(LRU_W),
           pl.BlockSpec((tm, D), lambda i: (i, ga_col // D)),
           pl.BlockSpec((tm, D), lambda i: (i, gr_col // D)),
           row(D), const(wa), const(wr), const(wo)],
        out_specs=row(D),
        out_shape=jax.ShapeDtypeStruct((T, D), _F32),
        scratch_shapes=scratch,
        compiler_params=_params(("parallel",)),
        name="merge_proj",
    )(*os_, *lses, rnn, zr, zr, x2d, wa, wr, wo)


def _mlp_kernel(x_ref, g_ref, w1_ref, w2_ref, gf_ref, o_ref, u_ref):
    j = pl.program_id(1)

    @pl.when(j == 0)
    def _():
        x = x_ref[...]
        u_ref[...] = _rms(x, g_ref[...]).astype(u_ref.dtype)
        o_ref[...] = x

    h = jnp.dot(u_ref[...], w1_ref[...], preferred_element_type=_F32)
    h = jnp.square(jnp.maximum(h, 0.0)).astype(_BF16)
    o_ref[...] += jnp.dot(h, w2_ref[...], preferred_element_type=_F32)

    @pl.when(j == pl.num_programs(1) - 1)
    def _():
        o_ref[...] = _rms(o_ref[...], gf_ref[...])


def _mlp(x2d, g, w1, w2, gf, *, tm, tf):
    T, D = x2d.shape
    F = w1.shape[1]
    assert T % tm == 0 and F % tf == 0
    return pl.pallas_call(
        _mlp_kernel,
        grid=(T // tm, F // tf),
        in_specs=[pl.BlockSpec((tm, D), lambda i, j: (i, 0)),
                  pl.BlockSpec((1, D), lambda i, j: (0, 0)),
                  pl.BlockSpec((D, tf), lambda i, j: (0, j)),
                  pl.BlockSpec((tf, D), lambda i, j: (j, 0)),
                  pl.BlockSpec((1, D), lambda i, j: (0, 0))],
        out_specs=pl.BlockSpec((tm, D), lambda i, j: (i, 0)),
        out_shape=jax.ShapeDtypeStruct((T, D), _F32),
        scratch_shapes=[pltpu.VMEM((tm, D), _BF16)],
        compiler_params=_params(("parallel", "arbitrary")),
        name="mlp_final_norm",
    )(x2d, g, w1, w2, gf)


_REST_GA = 0
_REST_GR = D_MODEL
_REST_RX = 2 * D_MODEL
_REST_RY = 2 * D_MODEL + LRU_W


def _layer(x, bias_tabs, p):
    B, S, D = x.shape
    T = B * S
    x2d = x.reshape(T, D)
    zgs = _qkv_proj(x2d, p["g_mix"], p["w_qkv"], B, S, tm=512)
    zr = _norm_proj(x2d, p["g_mix"], p["w_rest"], tm=256, tn=1792, out_dtype=_F32)
    os_, lses = [], []
    for g, (_, dil) in enumerate(DILATED_GROUPS):
        o, lse = _attn_group(zgs[g], bias_tabs, g, dil, B, S)
        os_.append(o)
        lses.append(lse)
    rnn = _rglru(zr.reshape(B, S, -1), _REST_RX, _REST_RY, p["conv_w"], p["conv_b"],
                 p["w_gate"], p["b_gate"], p["lam"])
    x2 = _merge(os_, lses, rnn, zr, _REST_GA, _REST_GR, x2d,
                p["w_attn_o"], p["w_rnn_o"], p["w_out"], B, S, tm=256)
    y = _mlp(x2, p["g_mlp"], p["w_mlp_in"], p["w_mlp_out"], p["g_final"], tm=512, tf=2048)
    return y.reshape(B, S, D)


def kernel(x_prompt, x_sample, rel_bias, norm_mix_g, w_in, conv_w, conv_b, lru_wa, lru_ba,
           lru_wx, lru_bx, lru_lambda, w_attn_o, w_rnn_o, w_out, norm_mlp_g, w_mlp_in,
           w_mlp_out, norm_final_g):
    depth = w_in.shape[0]
    assert depth == 1, "the final norm is fused into the single layer's MLP kernel"
    bias_tabs = _bias_tables(rel_bias)
    q_end, rx_end, ry_end, ga_end = 3 * ATTN_W, 3 * ATTN_W + LRU_W, 3 * ATTN_W + 2 * LRU_W, \
        3 * ATTN_W + 2 * LRU_W + D_MODEL
    wl = w_in[0]
    w_qkv = jnp.concatenate(
        [wl[:, part * ATTN_W + g * ATTN_OUT_W:part * ATTN_W + (g + 1) * ATTN_OUT_W]
         for g in range(N_GROUPS) for part in range(3)], axis=-1).astype(_BF16)
    w_gate = jnp.concatenate([lru_wa[0, 0], lru_wx[0, 0], lru_wa[0, 1], lru_wx[0, 1]],
                             axis=-1).astype(_BF16)
    b_gate = jnp.concatenate(
        [b.reshape(LRU_BLOCKS, 1, LRU_BLOCK_W)
         for b in (lru_ba[0, 0], lru_bx[0, 0], lru_ba[0, 1], lru_bx[0, 1])], axis=-1)
    p = dict(
        g_mix=norm_mix_g[0].reshape(1, D_MODEL),
        w_qkv=w_qkv,
        w_rest=jnp.concatenate([wl[:, ry_end:ga_end], wl[:, ga_end:], wl[:, q_end:rx_end],
                                wl[:, rx_end:ry_end]], axis=-1).astype(_BF16),
        conv_w=conv_w[0], conv_b=conv_b[0].reshape(1, LRU_W),
        w_gate=w_gate, b_gate=0.5 * b_gate.astype(_F32), lam=lru_lambda[0],
        w_attn_o=w_attn_o[0].astype(_BF16), w_rnn_o=w_rnn_o[0].astype(_BF16),
        w_out=w_out[0].astype(_BF16),
        g_mlp=norm_mlp_g[0].reshape(1, D_MODEL),
        w_mlp_in=w_mlp_in[0].astype(_BF16), w_mlp_out=w_mlp_out[0].astype(_BF16),
        g_final=norm_final_g.reshape(1, D_MODEL),
    )
    return (_layer(x_prompt, bias_tabs, p), _layer(x_sample, bias_tabs, p))
```

```python
import functools
import math

import jax
import jax.numpy as jnp
import numpy as np
from jax import lax
from jax.experimental import pallas as pl
from jax.experimental.pallas import tpu as pltpu

D_MODEL = 2048
HEAD_DIM = 128
HEADS_PER_GROUP = 4
DILATED_GROUPS = ((128, 1), (512, 4), (2048, 16))
N_GROUPS = len(DILATED_GROUPS)
ATTN_W = N_GROUPS * HEADS_PER_GROUP * HEAD_DIM
ATTN_OUT_W = HEADS_PER_GROUP * HEAD_DIM
NUM_BUCKETS = 32
REL_MAX_DIST = 1024
LRU_W = 1536
LRU_BLOCKS = 12
LRU_BLOCK_W = LRU_W // LRU_BLOCKS
LRU_C = 8.0
CONV_W = 4
D_FF = 4 * D_MODEL
NORM_EPS = 1e-6
NEG_INF = -1e30

_F32 = jnp.float32
_BF16 = jnp.bfloat16
_V7X_VMEM_LIMIT_BYTES = 56 * 1024 * 1024
_LANES = 128
_SUBLANES = 8

_SIDE = DILATED_GROUPS[0][0] // (2 * DILATED_GROUPS[0][1])
assert all(w // (2 * d) == _SIDE for w, d in DILATED_GROUPS)
_QBLK = 2 * _SIDE
_KWIN = 4 * _SIDE
assert _QBLK == _LANES
_GROUP_QKV_W = 3 * ATTN_OUT_W


def _params(semantics):
    return pltpu.CompilerParams(dimension_semantics=semantics,
                                vmem_limit_bytes=_V7X_VMEM_LIMIT_BYTES)


def _rms(x, g):
    ms = jnp.mean(x * x, axis=-1, keepdims=True)
    return x * lax.rsqrt(ms + NORM_EPS) * g


def _sigmoid(x):
    return 0.5 * jnp.tanh(0.5 * x) + 0.5


def _gelu_tanh(x):
    k = math.sqrt(2.0 / math.pi)
    half = 0.5 * x
    return half * jnp.tanh(x * (k + (k * 0.044715) * (x * x))) + half


def _lane_block(c):
    return slice(c * _LANES, (c + 1) * _LANES)


_LOG2E = math.log2(math.e)
_EDGE_VARIANTS = 4


def _rel_bucket_np(rel):
    half = NUM_BUCKETS // 2
    max_exact = half // 2
    n = np.abs(rel)
    nf = np.maximum(n, 1).astype(np.float32)
    ratio = np.log(nf / np.float32(max_exact)) / np.float32(math.log(REL_MAX_DIST / max_exact))
    large = max_exact + (ratio * np.float32(half - max_exact)).astype(np.int32)
    large = np.minimum(large, half - 1)
    return (np.where(rel > 0, half, 0) + np.where(n < max_exact, n, large)).astype(np.int32)


def _bias_kernel(tab_ref, idx_ref, o_ref):
    head = pl.program_id(0)
    idx = idx_ref[...]
    acc = jnp.zeros(idx.shape, _F32)
    for b in range(NUM_BUCKETS):
        acc = jnp.where(idx == b, tab_ref[b, head], acc)
    acc = acc * _LOG2E
    kj = lax.broadcasted_iota(jnp.int32, idx.shape, 1)
    in_band = idx >= 0
    after_start = kj >= _SIDE
    before_end = kj < _KWIN - _SIDE
    o_ref[0] = jnp.where(in_band, acc, NEG_INF)
    o_ref[1] = jnp.where(jnp.logical_and(in_band, after_start), acc, NEG_INF)
    o_ref[2] = jnp.where(jnp.logical_and(in_band, before_end), acc, NEG_INF)
    o_ref[3] = jnp.where(jnp.logical_and(in_band, jnp.logical_and(after_start, before_end)),
                         acc, NEG_INF)


def _bias_tables(rel_bias):
    qi = np.arange(_QBLK, dtype=np.int32)[:, None]
    kj = np.arange(_KWIN, dtype=np.int32)[None, :]
    rel = kj - _SIDE - qi
    band = np.abs(rel) <= _SIDE
    idx = np.stack([np.where(band, _rel_bucket_np(rel * dil), -1) for _, dil in DILATED_GROUPS])
    n_heads = N_GROUPS * HEADS_PER_GROUP
    return pl.pallas_call(
        _bias_kernel,
        grid=(n_heads,),
        in_specs=[pl.BlockSpec(memory_space=pltpu.SMEM),
                  pl.BlockSpec((None, _QBLK, _KWIN), lambda h: (h // HEADS_PER_GROUP, 0, 0))],
        out_specs=pl.BlockSpec((_EDGE_VARIANTS, None, _QBLK, _KWIN), lambda h: (0, h, 0, 0)),
        out_shape=jax.ShapeDtypeStruct((_EDGE_VARIANTS, n_heads, _QBLK, _KWIN), _F32),
        name="bias_tables",
    )(rel_bias.astype(_F32), jnp.asarray(idx.astype(np.int32)))


_FOLD_STRIDE = 4
_Q_SCALE = HEAD_DIM ** -0.5 * _LOG2E


def _qkv_proj_kernel(x_ref, g_ref, w_ref, *refs):
    out_refs = refs[:N_GROUPS]
    u_ref, stage_ref, stage2_ref = refs[N_GROUPS:]
    tm = x_ref.shape[0]
    n_blocks = ATTN_OUT_W // _LANES
    u_ref[...] = _rms(x_ref[...], g_ref[...]).astype(u_ref.dtype)

    def fold(o_ref, dil, part, slot):
        for c in range(n_blocks):
            lanes = _lane_block(part * n_blocks + c)
            if dil == _FOLD_STRIDE:
                for r in range(dil):
                    o_ref[r, :, lanes] = stage_ref[
                        slot, c, pl.ds(r, tm // dil, stride=dil), :].astype(o_ref.dtype)
                continue
            assert dil == _FOLD_STRIDE ** 2
            for r0 in range(_FOLD_STRIDE):
                stage2_ref[c, r0] = stage_ref[
                    slot, c, pl.ds(r0, tm // _FOLD_STRIDE, stride=_FOLD_STRIDE), :]
            for r0 in range(_FOLD_STRIDE):
                for r1 in range(_FOLD_STRIDE):
                    o_ref[r0 + _FOLD_STRIDE * r1, :, lanes] = stage2_ref[
                        c, r0, pl.ds(r1, tm // dil, stride=_FOLD_STRIDE), :].astype(o_ref.dtype)

    pending, slot = None, 0
    for g, (_, dil) in sorted(enumerate(DILATED_GROUPS), key=lambda e: -e[1][1]):
        o_ref = out_refs[g]
        for part in range(3):
            col0 = g * _GROUP_QKV_W + part * ATTN_OUT_W
            res = jnp.dot(u_ref[...], w_ref[:, col0:col0 + ATTN_OUT_W],
                          preferred_element_type=_F32)
            if part == 0:
                res = res * _Q_SCALE
            if dil == 1:
                o_ref[:, part * ATTN_OUT_W:(part + 1) * ATTN_OUT_W] = res.astype(o_ref.dtype)
            else:
                for c in range(n_blocks):
                    stage_ref[slot, c] = res[:, _lane_block(c)]
            if pending is not None:
                fold(*pending)
                pending = None
            if dil > 1:
                pending = (o_ref, dil, part, slot)
                slot = 1 - slot
    if pending is not None:
        fold(*pending)


def _qkv_proj(x2d, g, w, B, S, *, tm):
    T, D = x2d.shape
    assert T == B * S and S % tm == 0
    tiles_per_seq = S // tm
    out_shape, out_specs = [], []
    for _, dil in DILATED_GROUPS:
        assert tm % (dil * 16) == 0
        if dil == 1:
            out_shape.append(jax.ShapeDtypeStruct((T, _GROUP_QKV_W), _BF16))
            out_specs.append(pl.BlockSpec((tm, _GROUP_QKV_W), lambda i: (i, 0)))
        else:
            out_shape.append(jax.ShapeDtypeStruct((B, dil, S // dil, _GROUP_QKV_W), _BF16))
            out_specs.append(pl.BlockSpec(
                (None, dil, tm // dil, _GROUP_QKV_W),
                lambda i: (i // tiles_per_seq, 0, i % tiles_per_seq, 0)))
    return pl.pallas_call(
        _qkv_proj_kernel,
        grid=(T // tm,),
        in_specs=[pl.BlockSpec((tm, D), lambda i: (i, 0)),
                  pl.BlockSpec((1, D), lambda i: (0, 0)),
                  pl.BlockSpec(w.shape, lambda i: (0, 0), pipeline_mode=pl.Buffered(1))],
        out_specs=out_specs,
        out_shape=out_shape,
        scratch_shapes=[pltpu.VMEM((tm, D), _BF16),
                        pltpu.VMEM((2, ATTN_OUT_W // _LANES, tm, _LANES), _F32),
                        pltpu.VMEM((ATTN_OUT_W // _LANES, _FOLD_STRIDE, tm // _FOLD_STRIDE, _LANES),
                                   _F32)],
        compiler_params=_params(("parallel",)),
        name="qkv_proj",
    )(x2d, g, w)


def _norm_proj_kernel(x_ref, g_ref, w_ref, o_ref, u_ref, *, segments):
    u_ref[...] = _rms(x_ref[...], g_ref[...]).astype(u_ref.dtype)
    for start, stop, act in sorted(segments, key=lambda s: s[2] is None):
        res = jnp.dot(u_ref[...], w_ref[:, start:stop], preferred_element_type=_F32)
        o_ref[:, start:stop] = (res if act is None else act(res)).astype(o_ref.dtype)


def _norm_proj(x2d, g, w, *, tm, segments, out_dtype):
    T, D = x2d.shape
    N = w.shape[1]
    assert T % tm == 0 and segments[0][0] == 0 and segments[-1][1] == N
    assert all(a[1] == b[0] for a, b in zip(segments, segments[1:]))
    return pl.pallas_call(
        functools.partial(_norm_proj_kernel, segments=segments),
        grid=(T // tm,),
        in_specs=[pl.BlockSpec((tm, D), lambda i: (i, 0)),
                  pl.BlockSpec((1, D), lambda i: (0, 0)),
                  pl.BlockSpec(w.shape, lambda i: (0, 0), pipeline_mode=pl.Buffered(1))],
        out_specs=pl.BlockSpec((tm, N), lambda i: (i, 0)),
        out_shape=jax.ShapeDtypeStruct((T, N), out_dtype),
        scratch_shapes=[pltpu.VMEM((tm, D), _BF16)],
        compiler_params=_params(("parallel",)),
        name="norm_proj",
    )(x2d, g, w)


_ML_SHIFT = _LANES // 2


def _attn_kernel(q_ref, k_ref, v_ref, b_ref, o_ref, ml_ref, kpad, vpad, *, L, dil, heads, unroll):
    W = heads * HEAD_DIM
    zeros = jnp.zeros((_SIDE, W), _BF16)
    for r in range(dil):
        for pad, src in ((kpad, k_ref), (vpad, v_ref)):
            pad[r, 0:_SIDE, :] = zeros
            pad[r, _SIDE + L:2 * _SIDE + L, :] = zeros
            pad[r, _SIDE:_SIDE + L, :] = src[r]

    lane = lax.broadcasted_iota(jnp.int32, (_QBLK, _LANES), 1)
    nblk = L // _QBLK
    assert nblk & (nblk - 1) == 0
    blk_shift = nblk.bit_length() - 1

    def body(n, carry):
        r = lax.shift_right_logical(n, blk_shift)
        blk = n & (nblk - 1)
        qs = pl.multiple_of(blk * _QBLK, _QBLK)
        edge = jnp.where(blk == 0, 1, 0) + jnp.where(blk == nblk - 1, 2, 0)
        ml_tile = jnp.zeros((_QBLK, _LANES), _F32)
        for h in range(heads):
            cols = _lane_block(h)
            q = q_ref[r, pl.ds(qs, _QBLK), cols]
            kw = kpad[r, pl.ds(qs, _KWIN), cols]
            vw = vpad[r, pl.ds(qs, _KWIN), cols]
            s = lax.dot_general(q, kw, (((1,), (1,)), ((), ())),
                                preferred_element_type=_F32) + b_ref[edge, h]
            m = jnp.max(s, axis=-1, keepdims=True)
            p = jnp.exp2(s - m)
            l = jnp.sum(p, axis=-1, keepdims=True)
            o_ref[r, pl.ds(qs, _QBLK), cols] = jnp.dot(p.astype(_BF16), vw,
                                                       preferred_element_type=_F32)
            ml_tile = jnp.where(lane == h, m, jnp.where(lane == _ML_SHIFT + h, l, ml_tile))
        ml_ref[r, pl.ds(qs, _QBLK), :] = ml_tile
        return carry

    lax.fori_loop(0, dil * nblk, body, 0, unroll=unroll)


def _attn_heads_per_step(S):
    return HEADS_PER_GROUP if S <= 2048 else HEADS_PER_GROUP // 2


def _attn_group(zg, bias, group, dil, B, S):
    L = S // dil
    assert L % _QBLK == 0
    heads = _attn_heads_per_step(S)
    steps = HEADS_PER_GROUP // heads
    W = heads * HEAD_DIM
    zg = zg.reshape(B, dil, L, _GROUP_QKV_W)
    col = lambda part: (lambda b, hp: (b, 0, 0, part * steps + hp))
    return pl.pallas_call(
        functools.partial(_attn_kernel, L=L, dil=dil, heads=heads,
                          unroll=min(64 // heads, dil * L // _QBLK)),
        grid=(B, steps),
        in_specs=[pl.BlockSpec((None, dil, L, W), col(0)),
                  pl.BlockSpec((None, dil, L, W), col(1)),
                  pl.BlockSpec((None, dil, L, W), col(2)),
                  pl.BlockSpec((_EDGE_VARIANTS, heads, _QBLK, _KWIN),
                               lambda b, hp: (0, group * steps + hp, 0, 0))],
        out_specs=[pl.BlockSpec((None, dil, L, W), lambda b, hp: (b, 0, 0, hp)),
                   pl.BlockSpec((None, dil, L, _LANES), lambda b, hp: (b, 0, 0, hp))],
        out_shape=[jax.ShapeDtypeStruct((B, dil, L, ATTN_OUT_W), _F32),
                   jax.ShapeDtypeStruct((B, dil, L, steps * _LANES), _F32)],
        scratch_shapes=[pltpu.VMEM((dil, L + 2 * _SIDE, W), _BF16),
                        pltpu.VMEM((dil, L + 2 * _SIDE, W), _BF16)],
        compiler_params=_params(("parallel", "parallel")),
        name=f"dilated_attn_g{group}",
    )(zg, zg, zg, bias)


_CHUNKS = 4 * _SUBLANES
_PITCH_PAD = 4
_GATE_TAUS = 8
_SCAN_UNROLL = 16
_F32_TINY = float(np.finfo(np.float32).tiny)


def _chunk_rows(ref, start, pitch):
    return jnp.concatenate(
        [ref[pl.ds(start + half * _SUBLANES * pitch, _SUBLANES, stride=pitch), :]
         for half in range(_CHUNKS // _SUBLANES)], axis=0)


def _rglru_kernel(rx_ref, ry_ref, cw_ref, cb_ref, wg_ref, bg_ref, lam_ref, o_ref,
                  xs, a_f, h_f, a_b, h_b, *, S):
    BW = LRU_BLOCK_W
    Lc = S // _CHUNKS
    P = Lc + _PITCH_PAD
    left = CONV_W // 2

    for c in range(_CHUNKS):
        lo = c * Lc - left
        hi = lo + P
        s_lo, s_hi = max(lo, 0), min(hi, S)
        xs[c * P + (s_lo - lo):c * P + (s_hi - lo), :] = rx_ref[s_lo:s_hi, :]
        if lo < 0:
            xs[c * P:c * P - lo, :] = jnp.zeros((-lo, BW), _F32)
        if hi > S:
            xs[c * P + (S - lo):(c + 1) * P, :] = jnp.zeros((hi - S, BW), _F32)

    cw = 0.5 * cw_ref[...]
    cb = 0.5 * cb_ref[...]
    half_decay = (-0.5 * LRU_C) * jax.nn.softplus(-lam_ref[...])
    TB = _GATE_TAUS

    def gate_body(bi, carry):
        t0 = bi * TB
        taps = [_chunk_rows(xs, t0 + j, P) for j in range(TB + CONV_W - 1)]
        rows = []
        for tau in range(TB):
            xc = cb
            for t in range(CONV_W):
                xc = xc + taps[tau + t] * cw[t:t + 1]
            rows.append(xc)
        half_x = jnp.concatenate(rows, axis=0)
        g = jnp.tanh(jnp.dot(half_x.astype(_BF16), wg_ref[...], preferred_element_type=_F32)
                     + bg_ref[...])
        r0 = pl.multiple_of(t0 * _CHUNKS, TB * _CHUNKS)
        for d, (a_ref, u_ref) in enumerate(((a_f, h_f), (a_b, h_b))):
            log_a = half_decay[d:d + 1] * g[:, (2 * d) * BW:(2 * d + 1) * BW] + half_decay[d:d + 1]
            ix = (g[:, (2 * d + 1) * BW:(2 * d + 2) * BW] + 1.0) * half_x
            a = jnp.exp(log_a)
            om = jnp.tanh(log_a) * (-1.0 - a * a)
            root = om * lax.rsqrt(jnp.maximum(om, _F32_TINY))
            a_ref[pl.ds(r0, TB * _CHUNKS), :] = a
            u_ref[pl.ds(r0, TB * _CHUNKS), :] = root * ix
        return carry

    lax.fori_loop(0, Lc // TB, gate_body, 0, unroll=4)

    def scan_body(tau, carry):
        hf, pf, hb, pb = carry
        rf = pl.multiple_of(tau * _CHUNKS, _CHUNKS)
        rb = pl.multiple_of((Lc - 1 - tau) * _CHUNKS, _CHUNKS)
        af = a_f[pl.ds(rf, _CHUNKS), :]
        hf = af * hf + h_f[pl.ds(rf, _CHUNKS), :]
        pf = af * pf
        h_f[pl.ds(rf, _CHUNKS), :] = hf
        a_f[pl.ds(rf, _CHUNKS), :] = pf
        ab = a_b[pl.ds(rb, _CHUNKS), :]
        hb = ab * hb + h_b[pl.ds(rb, _CHUNKS), :]
        pb = ab * pb
        h_b[pl.ds(rb, _CHUNKS), :] = hb
        a_b[pl.ds(rb, _CHUNKS), :] = pb
        return hf, pf, hb, pb

    zero = jnp.zeros((_CHUNKS, BW), _F32)
    one = jnp.ones((_CHUNKS, BW), _F32)
    h_end, p_end, h_beg, p_beg = lax.fori_loop(0, Lc, scan_body, (zero, one, zero, one),
                                               unroll=_SCAN_UNROLL)

    row = lax.broadcasted_iota(jnp.int32, (_CHUNKS, BW), 0)
    in_f, in_b = zero, zero
    prev_f = jnp.zeros((1, BW), _F32)
    prev_b = jnp.zeros((1, BW), _F32)
    for c in range(1, _CHUNKS):
        prev_f = h_end[c - 1:c] + p_end[c - 1:c] * prev_f
        in_f = jnp.where(row == c, prev_f, in_f)
        cb_ = _CHUNKS - 1 - c
        prev_b = h_beg[cb_ + 1:cb_ + 2] + p_beg[cb_ + 1:cb_ + 2] * prev_b
        in_b = jnp.where(row == cb_, prev_b, in_b)

    def fix_body(tau, carry):
        r = pl.multiple_of(tau * _CHUNKS, _CHUNKS)
        h = (h_f[pl.ds(r, _CHUNKS), :] + a_f[pl.ds(r, _CHUNKS), :] * in_f
             + h_b[pl.ds(r, _CHUNKS), :] + a_b[pl.ds(r, _CHUNKS), :] * in_b)
        for half in range(_CHUNKS // _SUBLANES):
            xs[pl.ds(tau + half * _SUBLANES * P, _SUBLANES, stride=P), :] = \
                h[half * _SUBLANES:(half + 1) * _SUBLANES]
        return carry

    lax.fori_loop(0, Lc, fix_body, 0, unroll=_SCAN_UNROLL)

    for c in range(_CHUNKS):
        t = slice(c * Lc, (c + 1) * Lc)
        o_ref[t, :] = (xs[c * P:c * P + Lc, :] * ry_ref[t, :]).astype(o_ref.dtype)


def _rglru(zr, rx_col, ry_col, conv_w, conv_b, wg, bg, lam):
    B, S, _ = zr.shape
    BW = LRU_BLOCK_W
    Lc = S // _CHUNKS
    assert S % (_CHUNKS * _GATE_TAUS) == 0 and Lc % _SCAN_UNROLL == 0
    assert rx_col % BW == 0 and ry_col % BW == 0
    seq = pltpu.VMEM((S, BW), _F32)
    out = pl.pallas_call(
        functools.partial(_rglru_kernel, S=S),
        grid=(B, LRU_BLOCKS),
        in_specs=[pl.BlockSpec((None, S, BW), lambda b, n: (b, 0, rx_col // BW + n)),
                  pl.BlockSpec((None, S, BW), lambda b, n: (b, 0, ry_col // BW + n)),
                  pl.BlockSpec((CONV_W, BW), lambda b, n: (0, n)),
                  pl.BlockSpec((1, BW), lambda b, n: (0, n)),
                  pl.BlockSpec((None, BW, 4 * BW), lambda b, n: (n, 0, 0)),
                  pl.BlockSpec((None, 1, 4 * BW), lambda b, n: (n, 0, 0)),
                  pl.BlockSpec((2, BW), lambda b, n: (0, n))],
        out_specs=pl.BlockSpec((None, S, BW), lambda b, n: (b, 0, n)),
        out_shape=jax.ShapeDtypeStruct((B, S, LRU_W), _BF16),
        scratch_shapes=[pltpu.VMEM((_CHUNKS * (Lc + _PITCH_PAD), BW), _F32), seq, seq, seq, seq],
        compiler_params=_params(("parallel", "parallel")),
        name="conv_rglru",
    )(zr, zr, conv_w, conv_b, wg, bg, lam)
    return out.reshape(B * S, LRU_W)


def _unfold(src_ref, dst_ref, dil):
    rows = src_ref.shape[1]
    for r in range(dil):
        for c in range(dst_ref.shape[0]):
            dst_ref[c, pl.ds(r, rows, stride=dil), :] = src_ref[r, :, _lane_block(c)]


def _merge_kernel(*refs, heads):
    o_refs = refs[0:N_GROUPS]
    l_refs = refs[N_GROUPS:2 * N_GROUPS]
    rnn_ref, ga_ref, gr_ref, x_ref, wa_ref, wr_ref, wo_ref, out_ref = refs[2 * N_GROUPS:2 * N_GROUPS + 8]
    scratch = refs[2 * N_GROUPS + 8:]

    pr = jnp.dot(rnn_ref[...], wr_ref[...], preferred_element_type=_F32)

    o_get, l_get, k = [], [], 0
    for g, (_, dil) in enumerate(DILATED_GROUPS):
        if dil == 1:
            o_get.append(lambda c, ref=o_refs[g]: ref[:, _lane_block(c)])
            l_get.append(lambda c, ref=l_refs[g]: ref[:, _lane_block(c)])
        else:
            o_scr, l_scr = scratch[k], scratch[k + 1]
            k += 2
            _unfold(o_refs[g], o_scr, dil)
            _unfold(l_refs[g], l_scr, dil)
            o_get.append(lambda c, ref=o_scr: ref[c])
            l_get.append(lambda c, ref=l_scr: ref[c])

    wts = []
    for step in range(HEADS_PER_GROUP // heads):
        ml = [get(step) for get in l_get]
        mx = functools.reduce(jnp.maximum, ml)
        es = [jnp.exp2(t - mx) for t in ml]
        den = functools.reduce(jnp.add, [e * pltpu.roll(t, _ML_SHIFT, axis=1)
                                         for e, t in zip(es, ml)])
        inv = 1.0 / den
        wts.append([e * inv for e in es])
    attn = []
    for h in range(HEADS_PER_GROUP):
        w = wts[h // heads]
        c = h % heads
        attn.append(functools.reduce(
            jnp.add, [w[g][:, c:c + 1] * o_get[g](h) for g in range(N_GROUPS)]))
    attn = jnp.concatenate(attn, axis=-1).astype(_BF16)
    pa = jnp.dot(attn, wa_ref[...], preferred_element_type=_F32)
    merged = ga_ref[...] * pa + gr_ref[...] * pr
    out_ref[...] = x_ref[...] + jnp.dot(merged.astype(_BF16), wo_ref[...],
                                        preferred_element_type=_F32)


def _merge(os_, lses, rnn, zr, ga_col, gr_col, x2d, wa, wr, wo, B, S, *, tm):
    T, D = x2d.shape
    assert S % tm == 0 and ga_col % D == 0 and gr_col % D == 0
    heads = _attn_heads_per_step(S)
    LW = (HEADS_PER_GROUP // heads) * _LANES
    tiles_per_seq = S // tm
    row = lambda w: pl.BlockSpec((tm, w), lambda i: (i, 0))
    const = lambda a: pl.BlockSpec(a.shape, lambda i: (0, 0), pipeline_mode=pl.Buffered(1))

    def group_spec(dil, w):
        if dil == 1:
            return row(w)
        assert tm % (dil * _SUBLANES) == 0
        return pl.BlockSpec((None, dil, tm // dil, w),
                            lambda i: (i // tiles_per_seq, 0, i % tiles_per_seq, 0))

    scratch = []
    for _, dil in DILATED_GROUPS:
        if dil > 1:
            scratch += [pltpu.VMEM((ATTN_OUT_W // _LANES, tm, _LANES), _F32),
                        pltpu.VMEM((LW // _LANES, tm, _LANES), _F32)]
    os_ = [o.reshape(T, ATTN_OUT_W) if dil == 1 else o for o, (_, dil) in zip(os_, DILATED_GROUPS)]
    lses = [l.reshape(T, LW) if dil == 1 else l for l, (_, dil) in zip(lses, DILATED_GROUPS)]
    return pl.pallas_call(
        functools.partial(_merge_kernel, heads=heads),
        grid=(T // tm,),
        in_specs=[group_spec(dil, ATTN_OUT_W) for _, dil in DILATED_GROUPS]
        + [group_spec(dil, LW) for _, dil in DILATED_GROUPS]
        + [row(LRU_W),
           pl.BlockSpec((tm, D), lambda i: (i, ga_col // D)),
           pl.BlockSpec((tm, D), lambda i: (i, gr_col // D)),
           row(D), const(wa), const(wr), const(wo)],
        out_specs=row(D),
        out_shape=jax.ShapeDtypeStruct((T, D), _F32),
        scratch_shapes=scratch,
        compiler_params=_params(("parallel",)),
        name="merge_proj",
    )(*os_, *lses, rnn, zr, zr, x2d, wa, wr, wo)


def _mlp_kernel(x_ref, g_ref, w1_ref, w2_ref, gf_ref, o_ref, u_ref):
    j = pl.program_id(1)

    @pl.when(j == 0)
    def _():
        x = x_ref[...]
        u_ref[...] = _rms(x, g_ref[...]).astype(u_ref.dtype)
        o_ref[...] = x

    h = jnp.dot(u_ref[...], w1_ref[...], preferred_element_type=_F32)
    h = jnp.square(jnp.maximum(h, 0.0)).astype(_BF16)
    o_ref[...] += jnp.dot(h, w2_ref[...], preferred_element_type=_F32)

    @pl.when(j == pl.num_programs(1) - 1)
    def _():
        o_ref[...] = _rms(o_ref[...], gf_ref[...])


def _mlp(x2d, g, w1, w2, gf, *, tm, tf):
    T, D = x2d.shape
    F = w1.shape[1]
    assert T % tm == 0 and F % tf == 0
    return pl.pallas_call(
        _mlp_kernel,
        grid=(T // tm, F // tf),
        in_specs=[pl.BlockSpec((tm, D), lambda i, j: (i, 0)),
                  pl.BlockSpec((1, D), lambda i, j: (0, 0)),
                  pl.BlockSpec((D, tf), lambda i, j: (0, j)),
                  pl.BlockSpec((tf, D), lambda i, j: (j, 0)),
                  pl.BlockSpec((1, D), lambda i, j: (0, 0))],
        out_specs=pl.BlockSpec((tm, D), lambda i, j: (i, 0)),
        out_shape=jax.ShapeDtypeStruct((T, D), _F32),
        scratch_shapes=[pltpu.VMEM((tm, D), _BF16)],
        compiler_params=_params(("parallel", "arbitrary")),
        name="mlp_final_norm",
    )(x2d, g, w1, w2, gf)


_REST_GA = 0
_REST_GR = D_MODEL
_REST_RX = 2 * D_MODEL
_REST_RY = 2 * D_MODEL + LRU_W
_REST_SEGMENTS = ((_REST_GA, _REST_GR, _sigmoid), (_REST_GR, _REST_RX, _sigmoid),
                  (_REST_RX, _REST_RY, None), (_REST_RY, _REST_RY + LRU_W, _gelu_tanh))


def _layer(x, bias_tabs, p):
    B, S, D = x.shape
    T = B * S
    x2d = x.reshape(T, D)
    zgs = _qkv_proj(x2d, p["g_mix"], p["w_qkv"], B, S, tm=512)
    zr = _norm_proj(x2d, p["g_mix"], p["w_rest"], tm=256, segments=_REST_SEGMENTS, out_dtype=_F32)
    os_, lses = [], []
    for g, (_, dil) in enumerate(DILATED_GROUPS):
        o, lse = _attn_group(zgs[g], bias_tabs, g, dil, B, S)
        os_.append(o)
        lses.append(lse)
    rnn = _rglru(zr.reshape(B, S, -1), _REST_RX, _REST_RY, p["conv_w"], p["conv_b"],
                 p["w_gate"], p["b_gate"], p["lam"])
    x2 = _merge(os_, lses, rnn, zr, _REST_GA, _REST_GR, x2d,
                p["w_attn_o"], p["w_rnn_o"], p["w_out"], B, S, tm=256)
    y = _mlp(x2, p["g_mlp"], p["w_mlp_in"], p["w_mlp_out"], p["g_final"], tm=512, tf=2048)
    return y.reshape(B, S, D)


def kernel(x_prompt, x_sample, rel_bias, norm_mix_g, w_in, conv_w, conv_b, lru_wa, lru_ba,
           lru_wx, lru_bx, lru_lambda, w_attn_o, w_rnn_o, w_out, norm_mlp_g, w_mlp_in,
           w_mlp_out, norm_final_g):
    depth = w_in.shape[0]
    assert depth == 1, "the final norm is fused into the single layer's MLP kernel"
    bias_tabs = _bias_tables(rel_bias)
    q_end, rx_end, ry_end, ga_end = 3 * ATTN_W, 3 * ATTN_W + LRU_W, 3 * ATTN_W + 2 * LRU_W, \
        3 * ATTN_W + 2 * LRU_W + D_MODEL
    wl = w_in[0]
    w_qkv = jnp.concatenate(
        [wl[:, part * ATTN_W + g * ATTN_OUT_W:part * ATTN_W + (g + 1) * ATTN_OUT_W]
         for g in range(N_GROUPS) for part in range(3)], axis=-1).astype(_BF16)
    w_gate = jnp.concatenate([lru_wa[0, 0], lru_wx[0, 0], lru_wa[0, 1], lru_wx[0, 1]],
                             axis=-1).astype(_BF16)
    b_gate = jnp.concatenate(
        [b.reshape(LRU_BLOCKS, 1, LRU_BLOCK_W)
         for b in (lru_ba[0, 0], lru_bx[0, 0], lru_ba[0, 1], lru_bx[0, 1])], axis=-1)
    p = dict(
        g_mix=norm_mix_g[0].reshape(1, D_MODEL),
        w_qkv=w_qkv,
        w_rest=jnp.concatenate([wl[:, ry_end:ga_end], wl[:, ga_end:], wl[:, q_end:rx_end],
                                wl[:, rx_end:ry_end]], axis=-1).astype(_BF16),
        conv_w=conv_w[0], conv_b=conv_b[0].reshape(1, LRU_W),
        w_gate=w_gate, b_gate=0.5 * b_gate.astype(_F32), lam=lru_lambda[0],
        w_attn_o=w_attn_o[0].astype(_BF16), w_rnn_o=w_rnn_o[0].astype(_BF16),
        w_out=w_out[0].astype(_BF16),
        g_mlp=norm_mlp_g[0].reshape(1, D_MODEL),
        w_mlp_in=w_mlp_in[0].astype(_BF16), w_mlp_out=w_mlp_out[0].astype(_BF16),
        g_final=norm_final_g.reshape(1, D_MODEL),
    )
    return (_layer(x_prompt, bias_tabs, p), _layer(x_sample, bias_tabs, p))
```

```python
import functools
import math

import jax
import jax.numpy as jnp
import numpy as np
from jax import lax
from jax.experimental import pallas as pl
from jax.experimental.pallas import tpu as pltpu

D_MODEL = 2048
HEAD_DIM = 128
HEADS_PER_GROUP = 4
DILATED_GROUPS = ((128, 1), (512, 4), (2048, 16))
N_GROUPS = len(DILATED_GROUPS)
ATTN_W = N_GROUPS * HEADS_PER_GROUP * HEAD_DIM
ATTN_OUT_W = HEADS_PER_GROUP * HEAD_DIM
NUM_BUCKETS = 32
REL_MAX_DIST = 1024
LRU_W = 1536
LRU_BLOCKS = 12
LRU_BLOCK_W = LRU_W // LRU_BLOCKS
LRU_C = 8.0
CONV_W = 4
D_FF = 4 * D_MODEL
NORM_EPS = 1e-6
NEG_INF = -1e30

_F32 = jnp.float32
_BF16 = jnp.bfloat16
_V7X_VMEM_LIMIT_BYTES = 56 * 1024 * 1024
_LANES = 128
_SUBLANES = 8

_SIDE = DILATED_GROUPS[0][0] // (2 * DILATED_GROUPS[0][1])
assert all(w // (2 * d) == _SIDE for w, d in DILATED_GROUPS)
_QBLK = 2 * _SIDE
_KWIN = 4 * _SIDE
assert _QBLK == _LANES
_GROUP_QKV_W = 3 * ATTN_OUT_W


def _params(semantics):
    return pltpu.CompilerParams(dimension_semantics=semantics,
                                vmem_limit_bytes=_V7X_VMEM_LIMIT_BYTES)


def _rms(x, g):
    ms = jnp.mean(x * x, axis=-1, keepdims=True)
    return x * lax.rsqrt(ms + NORM_EPS) * g


def _lane_block(c):
    return slice(c * _LANES, (c + 1) * _LANES)


_LOG2E = math.log2(math.e)
_EDGE_VARIANTS = 4


def _rel_bucket_np(rel):
    half = NUM_BUCKETS // 2
    max_exact = half // 2
    n = np.abs(rel)
    nf = np.maximum(n, 1).astype(np.float32)
    ratio = np.log(nf / np.float32(max_exact)) / np.float32(math.log(REL_MAX_DIST / max_exact))
    large = max_exact + (ratio * np.float32(half - max_exact)).astype(np.int32)
    large = np.minimum(large, half - 1)
    return (np.where(rel > 0, half, 0) + np.where(n < max_exact, n, large)).astype(np.int32)


def _bias_kernel(tab_ref, idx_ref, o_ref):
    head = pl.program_id(0)
    idx = idx_ref[...]
    acc = jnp.zeros(idx.shape, _F32)
    for b in range(NUM_BUCKETS):
        acc = jnp.where(idx == b, tab_ref[b, head], acc)
    acc = acc * _LOG2E
    kj = lax.broadcasted_iota(jnp.int32, idx.shape, 1)
    in_band = idx >= 0
    after_start = kj >= _SIDE
    before_end = kj < _KWIN - _SIDE
    o_ref[0] = jnp.where(in_band, acc, NEG_INF)
    o_ref[1] = jnp.where(jnp.logical_and(in_band, after_start), acc, NEG_INF)
    o_ref[2] = jnp.where(jnp.logical_and(in_band, before_end), acc, NEG_INF)
    o_ref[3] = jnp.where(jnp.logical_and(in_band, jnp.logical_and(after_start, before_end)),
                         acc, NEG_INF)


def _bias_tables(rel_bias):
    qi = np.arange(_QBLK, dtype=np.int32)[:, None]
    kj = np.arange(_KWIN, dtype=np.int32)[None, :]
    rel = kj - _SIDE - qi
    band = np.abs(rel) <= _SIDE
    idx = np.stack([np.where(band, _rel_bucket_np(rel * dil), -1) for _, dil in DILATED_GROUPS])
    n_heads = N_GROUPS * HEADS_PER_GROUP
    return pl.pallas_call(
        _bias_kernel,
        grid=(n_heads,),
        in_specs=[pl.BlockSpec(memory_space=pltpu.SMEM),
                  pl.BlockSpec((None, _QBLK, _KWIN), lambda h: (h // HEADS_PER_GROUP, 0, 0))],
        out_specs=pl.BlockSpec((_EDGE_VARIANTS, None, _QBLK, _KWIN), lambda h: (0, h, 0, 0)),
        out_shape=jax.ShapeDtypeStruct((_EDGE_VARIANTS, n_heads, _QBLK, _KWIN), _F32),
        name="bias_tables",
    )(rel_bias.astype(_F32), jnp.asarray(idx.astype(np.int32)))


_FOLD_STRIDE = 4
_Q_SCALE = HEAD_DIM ** -0.5 * _LOG2E


def _qkv_proj_kernel(x_ref, g_ref, w_ref, *refs):
    out_refs = refs[:N_GROUPS]
    u_ref, stage_ref, stage2_ref = refs[N_GROUPS:]
    tm = x_ref.shape[0]
    n_blocks = ATTN_OUT_W // _LANES
    u_ref[...] = _rms(x_ref[...], g_ref[...]).astype(u_ref.dtype)

    def fold(o_ref, dil, part, slot):
        for c in range(n_blocks):
            lanes = _lane_block(part * n_blocks + c)
            if dil == _FOLD_STRIDE:
                for r in range(dil):
                    o_ref[r, :, lanes] = stage_ref[
                        slot, c, pl.ds(r, tm // dil, stride=dil), :].astype(o_ref.dtype)
                continue
            assert dil == _FOLD_STRIDE ** 2
            for r0 in range(_FOLD_STRIDE):
                stage2_ref[c, r0] = stage_ref[
                    slot, c, pl.ds(r0, tm // _FOLD_STRIDE, stride=_FOLD_STRIDE), :]
            for r0 in range(_FOLD_STRIDE):
                for r1 in range(_FOLD_STRIDE):
                    o_ref[r0 + _FOLD_STRIDE * r1, :, lanes] = stage2_ref[
                        c, r0, pl.ds(r1, tm // dil, stride=_FOLD_STRIDE), :].astype(o_ref.dtype)

    pending, slot = None, 0
    for g, (_, dil) in sorted(enumerate(DILATED_GROUPS), key=lambda e: -e[1][1]):
        o_ref = out_refs[g]
        for part in range(3):
            col0 = part * ATTN_W + g * ATTN_OUT_W
            res = jnp.dot(u_ref[...], w_ref[:, col0:col0 + ATTN_OUT_W],
                          preferred_element_type=_F32)
            if part == 0:
                res = res * _Q_SCALE
            if dil == 1:
                o_ref[:, part * ATTN_OUT_W:(part + 1) * ATTN_OUT_W] = res.astype(o_ref.dtype)
            else:
                for c in range(n_blocks):
                    stage_ref[slot, c] = res[:, _lane_block(c)]
            if pending is not None:
                fold(*pending)
                pending = None
            if dil > 1:
                pending = (o_ref, dil, part, slot)
                slot = 1 - slot
    if pending is not None:
        fold(*pending)


def _qkv_proj(x2d, g, w, B, S, *, tm):
    T, D = x2d.shape
    assert T == B * S and S % tm == 0
    tiles_per_seq = S // tm
    out_shape, out_specs = [], []
    for _, dil in DILATED_GROUPS:
        assert tm % (dil * 16) == 0
        if dil == 1:
            out_shape.append(jax.ShapeDtypeStruct((T, _GROUP_QKV_W), _BF16))
            out_specs.append(pl.BlockSpec((tm, _GROUP_QKV_W), lambda i: (i, 0)))
        else:
            out_shape.append(jax.ShapeDtypeStruct((B, dil, S // dil, _GROUP_QKV_W), _BF16))
            out_specs.append(pl.BlockSpec(
                (None, dil, tm // dil, _GROUP_QKV_W),
                lambda i: (i // tiles_per_seq, 0, i % tiles_per_seq, 0)))
    return pl.pallas_call(
        _qkv_proj_kernel,
        grid=(T // tm,),
        in_specs=[pl.BlockSpec((tm, D), lambda i: (i, 0)),
                  pl.BlockSpec((1, D), lambda i: (0, 0)),
                  pl.BlockSpec(w.shape, lambda i: (0, 0), pipeline_mode=pl.Buffered(1))],
        out_specs=out_specs,
        out_shape=out_shape,
        scratch_shapes=[pltpu.VMEM((tm, D), _BF16),
                        pltpu.VMEM((2, ATTN_OUT_W // _LANES, tm, _LANES), _F32),
                        pltpu.VMEM((ATTN_OUT_W // _LANES, _FOLD_STRIDE, tm // _FOLD_STRIDE, _LANES),
                                   _F32)],
        compiler_params=_params(("parallel",)),
        name="qkv_proj",
    )(x2d, g, w)


def _norm_proj_kernel(x_ref, g_ref, w_ref, o_ref, u_ref, *, segments):
    u_ref[...] = _rms(x_ref[...], g_ref[...]).astype(u_ref.dtype)
    for src, dst, width in segments:
        o_ref[:, dst:dst + width] = jnp.dot(u_ref[...], w_ref[:, src:src + width],
                                            preferred_element_type=_F32).astype(o_ref.dtype)


def _norm_proj(x2d, g, w, *, tm, segments, out_dtype):
    T, D = x2d.shape
    N = w.shape[1]
    assert T % tm == 0 and sum(s[2] for s in segments) == N
    return pl.pallas_call(
        functools.partial(_norm_proj_kernel, segments=segments),
        grid=(T // tm,),
        in_specs=[pl.BlockSpec((tm, D), lambda i: (i, 0)),
                  pl.BlockSpec((1, D), lambda i: (0, 0)),
                  pl.BlockSpec(w.shape, lambda i: (0, 0), pipeline_mode=pl.Buffered(1))],
        out_specs=pl.BlockSpec((tm, N), lambda i: (i, 0)),
        out_shape=jax.ShapeDtypeStruct((T, N), out_dtype),
        scratch_shapes=[pltpu.VMEM((tm, D), _BF16)],
        compiler_params=_params(("parallel",)),
        name="norm_proj",
    )(x2d, g, w)


_ML_SHIFT = _LANES // 2


def _attn_kernel(q_ref, k_ref, v_ref, b_ref, o_ref, ml_ref, kpad, vpad, *, L, dil, heads, unroll):
    W = heads * HEAD_DIM
    zeros = jnp.zeros((_SIDE, W), _BF16)
    for r in range(dil):
        for pad, src in ((kpad, k_ref), (vpad, v_ref)):
            pad[r, 0:_SIDE, :] = zeros
            pad[r, _SIDE + L:2 * _SIDE + L, :] = zeros
            pad[r, _SIDE:_SIDE + L, :] = src[r]

    lane = lax.broadcasted_iota(jnp.int32, (_QBLK, _LANES), 1)
    nblk = L // _QBLK
    assert nblk & (nblk - 1) == 0
    blk_shift = nblk.bit_length() - 1

    def body(n, carry):
        r = lax.shift_right_logical(n, blk_shift)
        blk = n & (nblk - 1)
        qs = pl.multiple_of(blk * _QBLK, _QBLK)
        edge = jnp.where(blk == 0, 1, 0) + jnp.where(blk == nblk - 1, 2, 0)
        ml_tile = jnp.zeros((_QBLK, _LANES), _F32)
        for h in range(heads):
            cols = _lane_block(h)
            q = q_ref[r, pl.ds(qs, _QBLK), cols]
            kw = kpad[r, pl.ds(qs, _KWIN), cols]
            vw = vpad[r, pl.ds(qs, _KWIN), cols]
            s = lax.dot_general(q, kw, (((1,), (1,)), ((), ())),
                                preferred_element_type=_F32) + b_ref[edge, h]
            m = jnp.max(s, axis=-1, keepdims=True)
            p = jnp.exp2(s - m)
            l = jnp.sum(p, axis=-1, keepdims=True)
            o_ref[r, pl.ds(qs, _QBLK), cols] = jnp.dot(p.astype(_BF16), vw,
                                                       preferred_element_type=_F32)
            ml_tile = jnp.where(lane == h, m, jnp.where(lane == _ML_SHIFT + h, l, ml_tile))
        ml_ref[r, pl.ds(qs, _QBLK), :] = ml_tile
        return carry

    lax.fori_loop(0, dil * nblk, body, 0, unroll=unroll)


def _attn_heads_per_step(S):
    return HEADS_PER_GROUP if S <= 2048 else HEADS_PER_GROUP // 2


def _attn_group(zg, bias, group, dil, B, S):
    L = S // dil
    assert L % _QBLK == 0
    heads = _attn_heads_per_step(S)
    steps = HEADS_PER_GROUP // heads
    W = heads * HEAD_DIM
    zg = zg.reshape(B, dil, L, _GROUP_QKV_W)
    col = lambda part: (lambda b, hp: (b, 0, 0, part * steps + hp))
    return pl.pallas_call(
        functools.partial(_attn_kernel, L=L, dil=dil, heads=heads,
                          unroll=min(64 // heads, dil * L // _QBLK)),
        grid=(B, steps),
        in_specs=[pl.BlockSpec((None, dil, L, W), col(0)),
                  pl.BlockSpec((None, dil, L, W), col(1)),
                  pl.BlockSpec((None, dil, L, W), col(2)),
                  pl.BlockSpec((_EDGE_VARIANTS, heads, _QBLK, _KWIN),
                               lambda b, hp: (0, group * steps + hp, 0, 0))],
        out_specs=[pl.BlockSpec((None, dil, L, W), lambda b, hp: (b, 0, 0, hp)),
                   pl.BlockSpec((None, dil, L, _LANES), lambda b, hp: (b, 0, 0, hp))],
        out_shape=[jax.ShapeDtypeStruct((B, dil, L, ATTN_OUT_W), _F32),
                   jax.ShapeDtypeStruct((B, dil, L, steps * _LANES), _F32)],
        scratch_shapes=[pltpu.VMEM((dil, L + 2 * _SIDE, W), _BF16),
                        pltpu.VMEM((dil, L + 2 * _SIDE, W), _BF16)],
        compiler_params=_params(("parallel", "parallel")),
        name=f"dilated_attn_g{group}",
    )(zg, zg, zg, bias)


_CHUNKS = 4 * _SUBLANES
_PITCH_PAD = 4
_GATE_TAUS = 8
_SCAN_UNROLL = 16
_F32_TINY = float(np.finfo(np.float32).tiny)


def _sigmoid(x):
    return 0.5 * jnp.tanh(0.5 * x) + 0.5


def _gelu_tanh(x):
    k = math.sqrt(2.0 / math.pi)
    half = 0.5 * x
    return half * jnp.tanh(x * (k + (k * 0.044715) * (x * x))) + half


def _chunk_rows(ref, start, pitch):
    return jnp.concatenate(
        [ref[pl.ds(start + half * _SUBLANES * pitch, _SUBLANES, stride=pitch), :]
         for half in range(_CHUNKS // _SUBLANES)], axis=0)


def _rglru_kernel(rx_ref, ry_ref, cw_ref, cb_ref, wg_ref, bg_ref, lam_ref, o_ref,
                  xs, a_f, h_f, a_b, h_b, *, S):
    BW = LRU_BLOCK_W
    Lc = S // _CHUNKS
    P = Lc + _PITCH_PAD
    left = CONV_W // 2

    for c in range(_CHUNKS):
        lo = c * Lc - left
        hi = lo + P
        s_lo, s_hi = max(lo, 0), min(hi, S)
        xs[c * P + (s_lo - lo):c * P + (s_hi - lo), :] = rx_ref[s_lo:s_hi, :]
        if lo < 0:
            xs[c * P:c * P - lo, :] = jnp.zeros((-lo, BW), _F32)
        if hi > S:
            xs[c * P + (S - lo):(c + 1) * P, :] = jnp.zeros((hi - S, BW), _F32)

    cw = 0.5 * cw_ref[...]
    cb = 0.5 * cb_ref[...]
    half_decay = (-0.5 * LRU_C) * jax.nn.softplus(-lam_ref[...])
    TB = _GATE_TAUS

    def gate_body(bi, carry):
        t0 = bi * TB
        taps = [_chunk_rows(xs, t0 + j, P) for j in range(TB + CONV_W - 1)]
        rows = []
        for tau in range(TB):
            xc = cb
            for t in range(CONV_W):
                xc = xc + taps[tau + t] * cw[t:t + 1]
            rows.append(xc)
        half_x = jnp.concatenate(rows, axis=0)
        g = jnp.tanh(jnp.dot(half_x.astype(_BF16), wg_ref[...], preferred_element_type=_F32)
                     + bg_ref[...])
        r0 = pl.multiple_of(t0 * _CHUNKS, TB * _CHUNKS)
        for d, (a_ref, u_ref) in enumerate(((a_f, h_f), (a_b, h_b))):
            log_a = half_decay[d:d + 1] * g[:, (2 * d) * BW:(2 * d + 1) * BW] + half_decay[d:d + 1]
            ix = (g[:, (2 * d + 1) * BW:(2 * d + 2) * BW] + 1.0) * half_x
            a = jnp.exp(log_a)
            om = jnp.tanh(log_a) * (-1.0 - a * a)
            root = om * lax.rsqrt(jnp.maximum(om, _F32_TINY))
            a_ref[pl.ds(r0, TB * _CHUNKS), :] = a
            u_ref[pl.ds(r0, TB * _CHUNKS), :] = root * ix
        return carry

    lax.fori_loop(0, Lc // TB, gate_body, 0, unroll=4)

    def scan_body(tau, carry):
        hf, pf, hb, pb = carry
        rf = pl.multiple_of(tau * _CHUNKS, _CHUNKS)
        rb = pl.multiple_of((Lc - 1 - tau) * _CHUNKS, _CHUNKS)
        af = a_f[pl.ds(rf, _CHUNKS), :]
        hf = af * hf + h_f[pl.ds(rf, _CHUNKS), :]
        pf = af * pf
        h_f[pl.ds(rf, _CHUNKS), :] = hf
        a_f[pl.ds(rf, _CHUNKS), :] = pf
        ab = a_b[pl.ds(rb, _CHUNKS), :]
        hb = ab * hb + h_b[pl.ds(rb, _CHUNKS), :]
        pb = ab * pb
        h_b[pl.ds(rb, _CHUNKS), :] = hb
        a_b[pl.ds(rb, _CHUNKS), :] = pb
        return hf, pf, hb, pb

    zero = jnp.zeros((_CHUNKS, BW), _F32)
    one = jnp.ones((_CHUNKS, BW), _F32)
    h_end, p_end, h_beg, p_beg = lax.fori_loop(0, Lc, scan_body, (zero, one, zero, one),
                                               unroll=_SCAN_UNROLL)

    row = lax.broadcasted_iota(jnp.int32, (_CHUNKS, BW), 0)
    in_f, in_b = zero, zero
    prev_f = jnp.zeros((1, BW), _F32)
    prev_b = jnp.zeros((1, BW), _F32)
    for c in range(1, _CHUNKS):
        prev_f = h_end[c - 1:c] + p_end[c - 1:c] * prev_f
        in_f = jnp.where(row == c, prev_f, in_f)
        cb_ = _CHUNKS - 1 - c
        prev_b = h_beg[cb_ + 1:cb_ + 2] + p_beg[cb_ + 1:cb_ + 2] * prev_b
        in_b = jnp.where(row == cb_, prev_b, in_b)

    def fix_body(tau, carry):
        r = pl.multiple_of(tau * _CHUNKS, _CHUNKS)
        h = (h_f[pl.ds(r, _CHUNKS), :] + a_f[pl.ds(r, _CHUNKS), :] * in_f
             + h_b[pl.ds(r, _CHUNKS), :] + a_b[pl.ds(r, _CHUNKS), :] * in_b)
        for half in range(_CHUNKS // _SUBLANES):
            xs[pl.ds(tau + half * _SUBLANES * P, _SUBLANES, stride=P), :] = \
                h[half * _SUBLANES:(half + 1) * _SUBLANES]
        return carry

    lax.fori_loop(0, Lc, fix_body, 0, unroll=_SCAN_UNROLL)

    for c in range(_CHUNKS):
        t = slice(c * Lc, (c + 1) * Lc)
        o_ref[t, :] = (xs[c * P:c * P + Lc, :] * _gelu_tanh(ry_ref[t, :])).astype(o_ref.dtype)


def _rglru(zr, rx_col, ry_col, conv_w, conv_b, wg, bg, lam):
    B, S, _ = zr.shape
    BW = LRU_BLOCK_W
    Lc = S // _CHUNKS
    assert S % (_CHUNKS * _GATE_TAUS) == 0 and Lc % _SCAN_UNROLL == 0
    assert rx_col % BW == 0 and ry_col % BW == 0
    seq = pltpu.VMEM((S, BW), _F32)
    out = pl.pallas_call(
        functools.partial(_rglru_kernel, S=S),
        grid=(B, LRU_BLOCKS),
        in_specs=[pl.BlockSpec((None, S, BW), lambda b, n: (b, 0, rx_col // BW + n)),
                  pl.BlockSpec((None, S, BW), lambda b, n: (b, 0, ry_col // BW + n)),
                  pl.BlockSpec((CONV_W, BW), lambda b, n: (0, n)),
                  pl.BlockSpec((1, BW), lambda b, n: (0, n)),
                  pl.BlockSpec((None, BW, 4 * BW), lambda b, n: (n, 0, 0)),
                  pl.BlockSpec((None, 1, 4 * BW), lambda b, n: (n, 0, 0)),
                  pl.BlockSpec((2, BW), lambda b, n: (0, n))],
        out_specs=pl.BlockSpec((None, S, BW), lambda b, n: (b, 0, n)),
        out_shape=jax.ShapeDtypeStruct((B, S, LRU_W), _BF16),
        scratch_shapes=[pltpu.VMEM((_CHUNKS * (Lc + _PITCH_PAD), BW), _F32), seq, seq, seq, seq],
        compiler_params=_params(("parallel", "parallel")),
        name="conv_rglru",
    )(zr, zr, conv_w, conv_b, wg, bg, lam)
    return out.reshape(B * S, LRU_W)


def _unfold(src_ref, dst_ref, dil):
    rows = src_ref.shape[1]
    for r in range(dil):
        for c in range(dst_ref.shape[0]):
            dst_ref[c, pl.ds(r, rows, stride=dil), :] = src_ref[r, :, _lane_block(c)]


def _merge_kernel(*refs, heads):
    o_refs = refs[0:N_GROUPS]
    l_refs = refs[N_GROUPS:2 * N_GROUPS]
    rnn_ref, ga_ref, gr_ref, x_ref, wa_ref, wr_ref, wo_ref, out_ref = refs[2 * N_GROUPS:2 * N_GROUPS + 8]
    scratch = refs[2 * N_GROUPS + 8:]

    pr = jnp.dot(rnn_ref[...], wr_ref[...], preferred_element_type=_F32)

    o_get, l_get, k = [], [], 0
    for g, (_, dil) in enumerate(DILATED_GROUPS):
        if dil == 1:
            o_get.append(lambda c, ref=o_refs[g]: ref[:, _lane_block(c)])
            l_get.append(lambda c, ref=l_refs[g]: ref[:, _lane_block(c)])
        else:
            o_scr, l_scr = scratch[k], scratch[k + 1]
            k += 2
            _unfold(o_refs[g], o_scr, dil)
            _unfold(l_refs[g], l_scr, dil)
            o_get.append(lambda c, ref=o_scr: ref[c])
            l_get.append(lambda c, ref=l_scr: ref[c])

    wts = []
    for step in range(HEADS_PER_GROUP // heads):
        ml = [get(step) for get in l_get]
        mx = functools.reduce(jnp.maximum, ml)
        es = [jnp.exp2(t - mx) for t in ml]
        den = functools.reduce(jnp.add, [e * pltpu.roll(t, _ML_SHIFT, axis=1)
                                         for e, t in zip(es, ml)])
        inv = 1.0 / den
        wts.append([e * inv for e in es])
    attn = []
    for h in range(HEADS_PER_GROUP):
        w = wts[h // heads]
        c = h % heads
        attn.append(functools.reduce(
            jnp.add, [w[g][:, c:c + 1] * o_get[g](h) for g in range(N_GROUPS)]))
    attn = jnp.concatenate(attn, axis=-1).astype(_BF16)
    pa = jnp.dot(attn, wa_ref[...], preferred_element_type=_F32)
    merged = _sigmoid(ga_ref[...]) * pa + _sigmoid(gr_ref[...]) * pr
    out_ref[...] = x_ref[...] + jnp.dot(merged.astype(_BF16), wo_ref[...],
                                        preferred_element_type=_F32)


def _merge(os_, lses, rnn, zr, ga_col, gr_col, x2d, wa, wr, wo, B, S, *, tm):
    T, D = x2d.shape
    assert S % tm == 0 and ga_col % D == 0 and gr_col % D == 0
    heads = _attn_heads_per_step(S)
    LW = (HEADS_PER_GROUP // heads) * _LANES
    tiles_per_seq = S // tm
    row = lambda w: pl.BlockSpec((tm, w), lambda i: (i, 0))
    const = lambda a: pl.BlockSpec(a.shape, lambda i: (0, 0), pipeline_mode=pl.Buffered(1))

    def group_spec(dil, w):
        if dil == 1:
            return row(w)
        assert tm % (dil * _SUBLANES) == 0
        return pl.BlockSpec((None, dil, tm // dil, w),
                            lambda i: (i // tiles_per_seq, 0, i % tiles_per_seq, 0))

    scratch = []
    for _, dil in DILATED_GROUPS:
        if dil > 1:
            scratch += [pltpu.VMEM((ATTN_OUT_W // _LANES, tm, _LANES), _F32),
                        pltpu.VMEM((LW // _LANES, tm, _LANES), _F32)]
    os_ = [o.reshape(T, ATTN_OUT_W) if dil == 1 else o for o, (_, dil) in zip(os_, DILATED_GROUPS)]
    lses = [l.reshape(T, LW) if dil == 1 else l for l, (_, dil) in zip(lses, DILATED_GROUPS)]
    return pl.pallas_call(
        functools.partial(_merge_kernel, heads=heads),
        grid=(T // tm,),
        in_specs=[group_spec(dil, ATTN_OUT_W) for _, dil in DILATED_GROUPS]
        + [group_spec(dil, LW) for _, dil in DILATED_GROUPS]
        + [row(LRU_W),
           pl.BlockSpec((tm, D), lambda i: (i, ga_col // D)),
           pl.BlockSpec((tm, D), lambda i: (i, gr_col // D)),
           row(D), const(wa), const(wr), const(wo)],
        out_specs=row(D),
        out_shape=jax.ShapeDtypeStruct((T, D), _F32),
        scratch_shapes=scratch,
        compiler_params=_params(("parallel",)),
        name="merge_proj",
    )(*os_, *lses, rnn, zr, zr, x2d, wa, wr, wo)


def _mlp_kernel(x_ref, g_ref, w1_ref, w2_ref, gf_ref, o_ref, u_ref):
    j = pl.program_id(1)

    def ff_chunk():
        h = jnp.dot(u_ref[...], w1_ref[...], preferred_element_type=_F32)
        h = jnp.square(jnp.maximum(h, 0.0)).astype(_BF16)
        return jnp.dot(h, w2_ref[...], preferred_element_type=_F32)

    @pl.when(j == 0)
    def _():
        x = x_ref[...]
        u_ref[...] = _rms(x, g_ref[...]).astype(u_ref.dtype)
        o_ref[...] = x + ff_chunk()

    @pl.when(j > 0)
    def _():
        o_ref[...] += ff_chunk()

    @pl.when(j == pl.num_programs(1) - 1)
    def _():
        o_ref[...] = _rms(o_ref[...], gf_ref[...])


def _mlp(x2d, g, w1, w2, gf, *, tm, tf):
    T, D = x2d.shape
    F = w1.shape[1]
    assert T % tm == 0 and F % tf == 0
    return pl.pallas_call(
        _mlp_kernel,
        grid=(T // tm, F // tf),
        in_specs=[pl.BlockSpec((tm, D), lambda i, j: (i, 0)),
                  pl.BlockSpec((1, D), lambda i, j: (0, 0)),
                  pl.BlockSpec((D, tf), lambda i, j: (0, j)),
                  pl.BlockSpec((tf, D), lambda i, j: (j, 0)),
                  pl.BlockSpec((1, D), lambda i, j: (0, 0))],
        out_specs=pl.BlockSpec((tm, D), lambda i, j: (i, 0)),
        out_shape=jax.ShapeDtypeStruct((T, D), _F32),
        scratch_shapes=[pltpu.VMEM((tm, D), _BF16)],
        compiler_params=_params(("parallel", "arbitrary")),
        name="mlp_final_norm",
    )(x2d, g, w1, w2, gf)


_REST_GA = 0
_REST_GR = D_MODEL
_REST_RX = 2 * D_MODEL
_REST_RY = 2 * D_MODEL + LRU_W
_REST_SEGMENTS = ((2 * LRU_W, _REST_GA, D_MODEL), (2 * LRU_W + D_MODEL, _REST_GR, D_MODEL),
                  (0, _REST_RX, LRU_W), (LRU_W, _REST_RY, LRU_W))


def _layer(x, bias_tabs, p):
    B, S, D = x.shape
    T = B * S
    x2d = x.reshape(T, D)
    zgs = _qkv_proj(x2d, p["g_mix"], p["w_qkv"], B, S, tm=512)
    zr = _norm_proj(x2d, p["g_mix"], p["w_rest"], tm=256, segments=_REST_SEGMENTS, out_dtype=_F32)
    os_, lses = [], []
    for g, (_, dil) in enumerate(DILATED_GROUPS):
        o, lse = _attn_group(zgs[g], bias_tabs, g, dil, B, S)
        os_.append(o)
        lses.append(lse)
    rnn = _rglru(zr.reshape(B, S, -1), _REST_RX, _REST_RY, p["conv_w"], p["conv_b"],
                 p["w_gate"], p["b_gate"], p["lam"])
    x2 = _merge(os_, lses, rnn, zr, _REST_GA, _REST_GR, x2d,
                p["w_attn_o"], p["w_rnn_o"], p["w_out"], B, S, tm=256)
    y = _mlp(x2, p["g_mlp"], p["w_mlp_in"], p["w_mlp_out"], p["g_final"], tm=512, tf=2048)
    return y.reshape(B, S, D)


def kernel(x_prompt, x_sample, rel_bias, norm_mix_g, w_in, conv_w, conv_b, lru_wa, lru_ba,
           lru_wx, lru_bx, lru_lambda, w_attn_o, w_rnn_o, w_out, norm_mlp_g, w_mlp_in,
           w_mlp_out, norm_final_g):
    depth = w_in.shape[0]
    assert depth == 1, "the final norm is fused into the single layer's MLP kernel"
    bias_tabs = _bias_tables(rel_bias)
    wl = w_in[0]
    w_gate = jnp.concatenate([lru_wa[0, 0], lru_wx[0, 0], lru_wa[0, 1], lru_wx[0, 1]],
                             axis=-1).astype(_BF16)
    b_gate = jnp.concatenate(
        [b.reshape(LRU_BLOCKS, 1, LRU_BLOCK_W)
         for b in (lru_ba[0, 0], lru_bx[0, 0], lru_ba[0, 1], lru_bx[0, 1])], axis=-1)
    p = dict(
        g_mix=norm_mix_g[0].reshape(1, D_MODEL),
        w_qkv=wl[:, :3 * ATTN_W].astype(_BF16),
        w_rest=wl[:, 3 * ATTN_W:].astype(_BF16),
        conv_w=conv_w[0], conv_b=conv_b[0].reshape(1, LRU_W),
        w_gate=w_gate, b_gate=0.5 * b_gate.astype(_F32), lam=lru_lambda[0],
        w_attn_o=w_attn_o[0].astype(_BF16), w_rnn_o=w_rnn_o[0].astype(_BF16),
        w_out=w_out[0].astype(_BF16),
        g_mlp=norm_mlp_g[0].reshape(1, D_MODEL),
        w_mlp_in=w_mlp_in[0].astype(_BF16), w_mlp_out=w_mlp_out[0].astype(_BF16),
        g_final=norm_final_g.reshape(1, D_MODEL),
    )
    return (_layer(x_prompt, bias_tabs, p), _layer(x_sample, bias_tabs, p))
```

```python
import functools
import math

import jax
import jax.numpy as jnp
import numpy as np
from jax import lax
from jax.experimental import pallas as pl
from jax.experimental.pallas import tpu as pltpu

D_MODEL = 2048
HEAD_DIM = 128
HEADS_PER_GROUP = 4
DILATED_GROUPS = ((128, 1), (512, 4), (2048, 16))
N_GROUPS = len(DILATED_GROUPS)
ATTN_W = N_GROUPS * HEADS_PER_GROUP * HEAD_DIM
ATTN_OUT_W = HEADS_PER_GROUP * HEAD_DIM
NUM_BUCKETS = 32
REL_MAX_DIST = 1024
LRU_W = 1536
LRU_BLOCKS = 12
LRU_BLOCK_W = LRU_W // LRU_BLOCKS
LRU_C = 8.0
CONV_W = 4
D_FF = 4 * D_MODEL
NORM_EPS = 1e-6
NEG_INF = -1e30

_F32 = jnp.float32
_BF16 = jnp.bfloat16
_V7X_VMEM_LIMIT_BYTES = 56 * 1024 * 1024
_LANES = 128
_SUBLANES = 8

_SIDE = DILATED_GROUPS[0][0] // (2 * DILATED_GROUPS[0][1])
assert all(w // (2 * d) == _SIDE for w, d in DILATED_GROUPS)
_QBLK = 2 * _SIDE
_KWIN = 4 * _SIDE
assert _QBLK == _LANES
_GROUP_QKV_W = 3 * ATTN_OUT_W


def _params(semantics):
    return pltpu.CompilerParams(dimension_semantics=semantics,
                                vmem_limit_bytes=_V7X_VMEM_LIMIT_BYTES)


def _rms(x, g):
    ms = jnp.mean(x * x, axis=-1, keepdims=True)
    return x * lax.rsqrt(ms + NORM_EPS) * g


def _lane_block(c):
    return slice(c * _LANES, (c + 1) * _LANES)


_LOG2E = math.log2(math.e)
_EDGE_VARIANTS = 4


def _rel_bucket_np(rel):
    half = NUM_BUCKETS // 2
    max_exact = half // 2
    n = np.abs(rel)
    nf = np.maximum(n, 1).astype(np.float32)
    ratio = np.log(nf / np.float32(max_exact)) / np.float32(math.log(REL_MAX_DIST / max_exact))
    large = max_exact + (ratio * np.float32(half - max_exact)).astype(np.int32)
    large = np.minimum(large, half - 1)
    return (np.where(rel > 0, half, 0) + np.where(n < max_exact, n, large)).astype(np.int32)


def _bias_kernel(tab_ref, idx_ref, o_ref):
    head = pl.program_id(0)
    idx = idx_ref[...]
    acc = jnp.zeros(idx.shape, _F32)
    for b in range(NUM_BUCKETS):
        acc = jnp.where(idx == b, tab_ref[b, head], acc)
    acc = acc * _LOG2E
    kj = lax.broadcasted_iota(jnp.int32, idx.shape, 1)
    in_band = idx >= 0
    after_start = kj >= _SIDE
    before_end = kj < _KWIN - _SIDE
    o_ref[0] = jnp.where(in_band, acc, NEG_INF)
    o_ref[1] = jnp.where(jnp.logical_and(in_band, after_start), acc, NEG_INF)
    o_ref[2] = jnp.where(jnp.logical_and(in_band, before_end), acc, NEG_INF)
    o_ref[3] = jnp.where(jnp.logical_and(in_band, jnp.logical_and(after_start, before_end)),
                         acc, NEG_INF)


def _bias_tables(rel_bias):
    qi = np.arange(_QBLK, dtype=np.int32)[:, None]
    kj = np.arange(_KWIN, dtype=np.int32)[None, :]
    rel = kj - _SIDE - qi
    band = np.abs(rel) <= _SIDE
    idx = np.stack([np.where(band, _rel_bucket_np(rel * dil), -1) for _, dil in DILATED_GROUPS])
    n_heads = N_GROUPS * HEADS_PER_GROUP
    return pl.pallas_call(
        _bias_kernel,
        grid=(n_heads,),
        in_specs=[pl.BlockSpec(memory_space=pltpu.SMEM),
                  pl.BlockSpec((None, _QBLK, _KWIN), lambda h: (h // HEADS_PER_GROUP, 0, 0))],
        out_specs=pl.BlockSpec((_EDGE_VARIANTS, None, _QBLK, _KWIN), lambda h: (0, h, 0, 0)),
        out_shape=jax.ShapeDtypeStruct((_EDGE_VARIANTS, n_heads, _QBLK, _KWIN), _F32),
        name="bias_tables",
    )(rel_bias.astype(_F32), jnp.asarray(idx.astype(np.int32)))


def _cast_split_kernel(w_ref, *o_refs, bounds):
    for o_ref, (start, stop) in zip(o_refs, bounds):
        o_ref[...] = w_ref[:, start:stop].astype(o_ref.dtype)


def _cast_split(w, bounds, *, tm):
    R, C = w.shape
    assert R % tm == 0
    return pl.pallas_call(
        functools.partial(_cast_split_kernel, bounds=bounds),
        grid=(R // tm,),
        in_specs=[pl.BlockSpec((tm, C), lambda i: (i, 0))],
        out_specs=[pl.BlockSpec((tm, stop - start), lambda i: (i, 0)) for start, stop in bounds],
        out_shape=[jax.ShapeDtypeStruct((R, stop - start), _BF16) for start, stop in bounds],
        compiler_params=_params(("parallel",)),
        name="cast_split",
    )(w)


_FOLD_STRIDE = 4
_Q_SCALE = HEAD_DIM ** -0.5 * _LOG2E


def _qkv_proj_kernel(x_ref, g_ref, w_ref, *refs):
    out_refs = refs[:N_GROUPS]
    u_ref, stage_ref, stage2_ref = refs[N_GROUPS:]
    tm = x_ref.shape[0]
    n_blocks = ATTN_OUT_W // _LANES
    u_ref[...] = _rms(x_ref[...], g_ref[...]).astype(u_ref.dtype)

    def fold(o_ref, dil, part, slot):
        for c in range(n_blocks):
            lanes = _lane_block(part * n_blocks + c)
            if dil == _FOLD_STRIDE:
                for r in range(dil):
                    o_ref[r, :, lanes] = stage_ref[
                        slot, c, pl.ds(r, tm // dil, stride=dil), :].astype(o_ref.dtype)
                continue
            assert dil == _FOLD_STRIDE ** 2
            for r0 in range(_FOLD_STRIDE):
                stage2_ref[c, r0] = stage_ref[
                    slot, c, pl.ds(r0, tm // _FOLD_STRIDE, stride=_FOLD_STRIDE), :]
            for r0 in range(_FOLD_STRIDE):
                for r1 in range(_FOLD_STRIDE):
                    o_ref[r0 + _FOLD_STRIDE * r1, :, lanes] = stage2_ref[
                        c, r0, pl.ds(r1, tm // dil, stride=_FOLD_STRIDE), :].astype(o_ref.dtype)

    pending, slot = None, 0
    for g, (_, dil) in sorted(enumerate(DILATED_GROUPS), key=lambda e: -e[1][1]):
        o_ref = out_refs[g]
        for part in range(3):
            col0 = part * ATTN_W + g * ATTN_OUT_W
            res = jnp.dot(u_ref[...], w_ref[:, col0:col0 + ATTN_OUT_W],
                          preferred_element_type=_F32)
            if part == 0:
                res = res * _Q_SCALE
            if dil == 1:
                o_ref[:, part * ATTN_OUT_W:(part + 1) * ATTN_OUT_W] = res.astype(o_ref.dtype)
            else:
                for c in range(n_blocks):
                    stage_ref[slot, c] = res[:, _lane_block(c)]
            if pending is not None:
                fold(*pending)
                pending = None
            if dil > 1:
                pending = (o_ref, dil, part, slot)
                slot = 1 - slot
    if pending is not None:
        fold(*pending)


def _qkv_proj(x2d, g, w, B, S, *, tm):
    T, D = x2d.shape
    assert T == B * S and S % tm == 0
    tiles_per_seq = S // tm
    out_shape, out_specs = [], []
    for _, dil in DILATED_GROUPS:
        assert tm % (dil * 16) == 0
        if dil == 1:
            out_shape.append(jax.ShapeDtypeStruct((T, _GROUP_QKV_W), _BF16))
            out_specs.append(pl.BlockSpec((tm, _GROUP_QKV_W), lambda i: (i, 0)))
        else:
            out_shape.append(jax.ShapeDtypeStruct((B, dil, S // dil, _GROUP_QKV_W), _BF16))
            out_specs.append(pl.BlockSpec(
                (None, dil, tm // dil, _GROUP_QKV_W),
                lambda i: (i // tiles_per_seq, 0, i % tiles_per_seq, 0)))
    return pl.pallas_call(
        _qkv_proj_kernel,
        grid=(T // tm,),
        in_specs=[pl.BlockSpec((tm, D), lambda i: (i, 0)),
                  pl.BlockSpec((1, D), lambda i: (0, 0)),
                  pl.BlockSpec(w.shape, lambda i: (0, 0), pipeline_mode=pl.Buffered(1))],
        out_specs=out_specs,
        out_shape=out_shape,
        scratch_shapes=[pltpu.VMEM((tm, D), _BF16),
                        pltpu.VMEM((2, ATTN_OUT_W // _LANES, tm, _LANES), _F32),
                        pltpu.VMEM((ATTN_OUT_W // _LANES, _FOLD_STRIDE, tm // _FOLD_STRIDE, _LANES),
                                   _F32)],
        compiler_params=_params(("parallel",)),
        name="qkv_proj",
    )(x2d, g, w)


def _norm_proj_kernel(x_ref, g_ref, w_ref, o_ref, u_ref, *, segments):
    u_ref[...] = _rms(x_ref[...], g_ref[...]).astype(u_ref.dtype)
    for src, dst, width in segments:
        o_ref[:, dst:dst + width] = jnp.dot(u_ref[...], w_ref[:, src:src + width],
                                            preferred_element_type=_F32).astype(o_ref.dtype)


def _norm_proj(x2d, g, w, *, tm, segments, out_dtype):
    T, D = x2d.shape
    N = w.shape[1]
    assert T % tm == 0 and sum(s[2] for s in segments) == N
    return pl.pallas_call(
        functools.partial(_norm_proj_kernel, segments=segments),
        grid=(T // tm,),
        in_specs=[pl.BlockSpec((tm, D), lambda i: (i, 0)),
                  pl.BlockSpec((1, D), lambda i: (0, 0)),
                  pl.BlockSpec(w.shape, lambda i: (0, 0), pipeline_mode=pl.Buffered(1))],
        out_specs=pl.BlockSpec((tm, N), lambda i: (i, 0)),
        out_shape=jax.ShapeDtypeStruct((T, N), out_dtype),
        scratch_shapes=[pltpu.VMEM((tm, D), _BF16)],
        compiler_params=_params(("parallel",)),
        name="norm_proj",
    )(x2d, g, w)


_ML_SHIFT = _LANES // 2


def _attn_kernel(q_ref, k_ref, v_ref, b_ref, o_ref, ml_ref, kpad, vpad, *, L, dil, heads, unroll):
    W = heads * HEAD_DIM
    zeros = jnp.zeros((_SIDE, W), _BF16)
    for r in range(dil):
        for pad, src in ((kpad, k_ref), (vpad, v_ref)):
            pad[r, 0:_SIDE, :] = zeros
            pad[r, _SIDE + L:2 * _SIDE + L, :] = zeros
            pad[r, _SIDE:_SIDE + L, :] = src[r]

    lane = lax.broadcasted_iota(jnp.int32, (_QBLK, _LANES), 1)
    nblk = L // _QBLK
    assert nblk & (nblk - 1) == 0
    blk_shift = nblk.bit_length() - 1

    def body(n, carry):
        r = lax.shift_right_logical(n, blk_shift)
        blk = n & (nblk - 1)
        qs = pl.multiple_of(blk * _QBLK, _QBLK)
        edge = jnp.where(blk == 0, 1, 0) + jnp.where(blk == nblk - 1, 2, 0)
        ml_tile = jnp.zeros((_QBLK, _LANES), _F32)
        for h in range(heads):
            cols = _lane_block(h)
            q = q_ref[r, pl.ds(qs, _QBLK), cols]
            kw = kpad[r, pl.ds(qs, _KWIN), cols]
            vw = vpad[r, pl.ds(qs, _KWIN), cols]
            s = lax.dot_general(q, kw, (((1,), (1,)), ((), ())),
                                preferred_element_type=_F32) + b_ref[edge, h]
            m = jnp.max(s, axis=-1, keepdims=True)
            p = jnp.exp2(s - m)
            l = jnp.sum(p, axis=-1, keepdims=True)
            o_ref[r, pl.ds(qs, _QBLK), cols] = jnp.dot(p.astype(_BF16), vw,
                                                       preferred_element_type=_F32)
            ml_tile = jnp.where(lane == h, m, jnp.where(lane == _ML_SHIFT + h, l, ml_tile))
        ml_ref[r, pl.ds(qs, _QBLK), :] = ml_tile
        return carry

    lax.fori_loop(0, dil * nblk, body, 0, unroll=unroll)


def _attn_heads_per_step(S):
    return HEADS_PER_GROUP if S <= 2048 else HEADS_PER_GROUP // 2


def _attn_group(zg, bias, group, dil, B, S):
    L = S // dil
    assert L % _QBLK == 0
    heads = _attn_heads_per_step(S)
    steps = HEADS_PER_GROUP // heads
    W = heads * HEAD_DIM
    zg = zg.reshape(B, dil, L, _GROUP_QKV_W)
    col = lambda part: (lambda b, hp: (b, 0, 0, part * steps + hp))
    return pl.pallas_call(
        functools.partial(_attn_kernel, L=L, dil=dil, heads=heads,
                          unroll=min(64 // heads, dil * L // _QBLK)),
        grid=(B, steps),
        in_specs=[pl.BlockSpec((None, dil, L, W), col(0)),
                  pl.BlockSpec((None, dil, L, W), col(1)),
                  pl.BlockSpec((None, dil, L, W), col(2)),
                  pl.BlockSpec((_EDGE_VARIANTS, heads, _QBLK, _KWIN),
                               lambda b, hp: (0, group * steps + hp, 0, 0))],
        out_specs=[pl.BlockSpec((None, dil, L, W), lambda b, hp: (b, 0, 0, hp)),
                   pl.BlockSpec((None, dil, L, _LANES), lambda b, hp: (b, 0, 0, hp))],
        out_shape=[jax.ShapeDtypeStruct((B, dil, L, ATTN_OUT_W), _F32),
                   jax.ShapeDtypeStruct((B, dil, L, steps * _LANES), _F32)],
        scratch_shapes=[pltpu.VMEM((dil, L + 2 * _SIDE, W), _BF16),
                        pltpu.VMEM((dil, L + 2 * _SIDE, W), _BF16)],
        compiler_params=_params(("parallel", "parallel")),
        name=f"dilated_attn_g{group}",
    )(zg, zg, zg, bias)


_CHUNKS = 4 * _SUBLANES
_PITCH_PAD = 4
_GATE_TAUS = 8
_SCAN_UNROLL = 16
_F32_TINY = float(np.finfo(np.float32).tiny)


def _sigmoid(x):
    return 0.5 * jnp.tanh(0.5 * x) + 0.5


def _gelu_tanh(x):
    k = math.sqrt(2.0 / math.pi)
    half = 0.5 * x
    return half * jnp.tanh(x * (k + (k * 0.044715) * (x * x))) + half


def _chunk_rows(ref, start, pitch):
    return jnp.concatenate(
        [ref[pl.ds(start + half * _SUBLANES * pitch, _SUBLANES, stride=pitch), :]
         for half in range(_CHUNKS // _SUBLANES)], axis=0)


def _rglru_kernel(rx_ref, ry_ref, cw_ref, cb_ref, wg_ref, bg_ref, lam_ref, o_ref,
                  xs, a_f, h_f, a_b, h_b, *, S):
    BW = LRU_BLOCK_W
    Lc = S // _CHUNKS
    P = Lc + _PITCH_PAD
    left = CONV_W // 2

    for c in range(_CHUNKS):
        lo = c * Lc - left
        hi = lo + P
        s_lo, s_hi = max(lo, 0), min(hi, S)
        xs[c * P + (s_lo - lo):c * P + (s_hi - lo), :] = rx_ref[s_lo:s_hi, :]
        if lo < 0:
            xs[c * P:c * P - lo, :] = jnp.zeros((-lo, BW), _F32)
        if hi > S:
            xs[c * P + (S - lo):(c + 1) * P, :] = jnp.zeros((hi - S, BW), _F32)

    cw = 0.5 * cw_ref[...]
    cb = 0.5 * cb_ref[...]
    half_decay = (-0.5 * LRU_C) * jax.nn.softplus(-lam_ref[...])
    TB = _GATE_TAUS

    def gate_body(bi, carry):
        t0 = bi * TB
        taps = [_chunk_rows(xs, t0 + j, P) for j in range(TB + CONV_W - 1)]
        rows = []
        for tau in range(TB):
            xc = cb
            for t in range(CONV_W):
                xc = xc + taps[tau + t] * cw[t:t + 1]
            rows.append(xc)
        half_x = jnp.concatenate(rows, axis=0)
        g = jnp.tanh(jnp.dot(half_x.astype(_BF16), wg_ref[...], preferred_element_type=_F32)
                     + bg_ref[...])
        r0 = pl.multiple_of(t0 * _CHUNKS, TB * _CHUNKS)
        for d, (a_ref, u_ref) in enumerate(((a_f, h_f), (a_b, h_b))):
            log_a = half_decay[d:d + 1] * g[:, (2 * d) * BW:(2 * d + 1) * BW] + half_decay[d:d + 1]
            ix = (g[:, (2 * d + 1) * BW:(2 * d + 2) * BW] + 1.0) * half_x
            a = jnp.exp(log_a)
            om = jnp.tanh(log_a) * (-1.0 - a * a)
            root = om * lax.rsqrt(jnp.maximum(om, _F32_TINY))
            a_ref[pl.ds(r0, TB * _CHUNKS), :] = a
            u_ref[pl.ds(r0, TB * _CHUNKS), :] = root * ix
        return carry

    lax.fori_loop(0, Lc // TB, gate_body, 0, unroll=4)

    def scan_body(tau, carry):
        hf, pf, hb, pb = carry
        rf = pl.multiple_of(tau * _CHUNKS, _CHUNKS)
        rb = pl.multiple_of((Lc - 1 - tau) * _CHUNKS, _CHUNKS)
        af = a_f[pl.ds(rf, _CHUNKS), :]
        hf = af * hf + h_f[pl.ds(rf, _CHUNKS), :]
        pf = af * pf
        h_f[pl.ds(rf, _CHUNKS), :] = hf
        a_f[pl.ds(rf, _CHUNKS), :] = pf
        ab = a_b[pl.ds(rb, _CHUNKS), :]
        hb = ab * hb + h_b[pl.ds(rb, _CHUNKS), :]
        pb = ab * pb
        h_b[pl.ds(rb, _CHUNKS), :] = hb
        a_b[pl.ds(rb, _CHUNKS), :] = pb
        return hf, pf, hb, pb

    zero = jnp.zeros((_CHUNKS, BW), _F32)
    one = jnp.ones((_CHUNKS, BW), _F32)
    h_end, p_end, h_beg, p_beg = lax.fori_loop(0, Lc, scan_body, (zero, one, zero, one),
                                               unroll=_SCAN_UNROLL)

    row = lax.broadcasted_iota(jnp.int32, (_CHUNKS, BW), 0)
    in_f, in_b = zero, zero
    prev_f = jnp.zeros((1, BW), _F32)
    prev_b = jnp.zeros((1, BW), _F32)
    for c in range(1, _CHUNKS):
        prev_f = h_end[c - 1:c] + p_end[c - 1:c] * prev_f
        in_f = jnp.where(row == c, prev_f, in_f)
        cb_ = _CHUNKS - 1 - c
        prev_b = h_beg[cb_ + 1:cb_ + 2] + p_beg[cb_ + 1:cb_ + 2] * prev_b
        in_b = jnp.where(row == cb_, prev_b, in_b)

    def fix_body(tau, carry):
        r = pl.multiple_of(tau * _CHUNKS, _CHUNKS)
        h = (h_f[pl.ds(r, _CHUNKS), :] + a_f[pl.ds(r, _CHUNKS), :] * in_f
             + h_b[pl.ds(r, _CHUNKS), :] + a_b[pl.ds(r, _CHUNKS), :] * in_b)
        for half in range(_CHUNKS // _SUBLANES):
            xs[pl.ds(tau + half * _SUBLANES * P, _SUBLANES, stride=P), :] = \
                h[half * _SUBLANES:(half + 1) * _SUBLANES]
        return carry

    lax.fori_loop(0, Lc, fix_body, 0, unroll=_SCAN_UNROLL)

    for c in range(_CHUNKS):
        t = slice(c * Lc, (c + 1) * Lc)
        o_ref[t, :] = (xs[c * P:c * P + Lc, :] * _gelu_tanh(ry_ref[t, :])).astype(o_ref.dtype)


def _rglru(zr, rx_col, ry_col, conv_w, conv_b, wg, bg, lam):
    B, S, _ = zr.shape
    BW = LRU_BLOCK_W
    Lc = S // _CHUNKS
    assert S % (_CHUNKS * _GATE_TAUS) == 0 and Lc % _SCAN_UNROLL == 0
    assert rx_col % BW == 0 and ry_col % BW == 0
    seq = pltpu.VMEM((S, BW), _F32)
    out = pl.pallas_call(
        functools.partial(_rglru_kernel, S=S),
        grid=(B, LRU_BLOCKS),
        in_specs=[pl.BlockSpec((None, S, BW), lambda b, n: (b, 0, rx_col // BW + n)),
                  pl.BlockSpec((None, S, BW), lambda b, n: (b, 0, ry_col // BW + n)),
                  pl.BlockSpec((CONV_W, BW), lambda b, n: (0, n)),
                  pl.BlockSpec((1, BW), lambda b, n: (0, n)),
                  pl.BlockSpec((None, BW, 4 * BW), lambda b, n: (n, 0, 0)),
                  pl.BlockSpec((None, 1, 4 * BW), lambda b, n: (n, 0, 0)),
                  pl.BlockSpec((2, BW), lambda b, n: (0, n))],
        out_specs=pl.BlockSpec((None, S, BW), lambda b, n: (b, 0, n)),
        out_shape=jax.ShapeDtypeStruct((B, S, LRU_W), _BF16),
        scratch_shapes=[pltpu.VMEM((_CHUNKS * (Lc + _PITCH_PAD), BW), _F32), seq, seq, seq, seq],
        compiler_params=_params(("parallel", "parallel")),
        name="conv_rglru",
    )(zr, zr, conv_w, conv_b, wg, bg, lam)
    return out.reshape(B * S, LRU_W)


def _unfold(src_ref, dst_ref, dil):
    rows = src_ref.shape[1]
    for r in range(dil):
        for c in range(dst_ref.shape[0]):
            dst_ref[c, pl.ds(r, rows, stride=dil), :] = src_ref[r, :, _lane_block(c)]


def _merge_kernel(*refs, heads):
    o_refs = refs[0:N_GROUPS]
    l_refs = refs[N_GROUPS:2 * N_GROUPS]
    rnn_ref, ga_ref, gr_ref, x_ref, wa_ref, wr_ref, wo_ref, out_ref = refs[2 * N_GROUPS:2 * N_GROUPS + 8]
    scratch = refs[2 * N_GROUPS + 8:]

    pr = jnp.dot(rnn_ref[...], wr_ref[...], preferred_element_type=_F32)

    o_get, l_get, k = [], [], 0
    for g, (_, dil) in enumerate(DILATED_GROUPS):
        if dil == 1:
            o_get.append(lambda c, ref=o_refs[g]: ref[:, _lane_block(c)])
            l_get.append(lambda c, ref=l_refs[g]: ref[:, _lane_block(c)])
        else:
            o_scr, l_scr = scratch[k], scratch[k + 1]
            k += 2
            _unfold(o_refs[g], o_scr, dil)
            _unfold(l_refs[g], l_scr, dil)
            o_get.append(lambda c, ref=o_scr: ref[c])
            l_get.append(lambda c, ref=l_scr: ref[c])

    wts = []
    for step in range(HEADS_PER_GROUP // heads):
        ml = [get(step) for get in l_get]
        mx = functools.reduce(jnp.maximum, ml)
        es = [jnp.exp2(t - mx) for t in ml]
        den = functools.reduce(jnp.add, [e * pltpu.roll(t, _ML_SHIFT, axis=1)
                                         for e, t in zip(es, ml)])
        inv = 1.0 / den
        wts.append([e * inv for e in es])
    attn = []
    for h in range(HEADS_PER_GROUP):
        w = wts[h // heads]
        c = h % heads
        attn.append(functools.reduce(
            jnp.add, [w[g][:, c:c + 1] * o_get[g](h) for g in range(N_GROUPS)]))
    attn = jnp.concatenate(attn, axis=-1).astype(_BF16)
    pa = jnp.dot(attn, wa_ref[...], preferred_element_type=_F32)
    merged = _sigmoid(ga_ref[...]) * pa + _sigmoid(gr_ref[...]) * pr
    out_ref[...] = x_ref[...] + jnp.dot(merged.astype(_BF16), wo_ref[...],
                                        preferred_element_type=_F32)


def _merge(os_, lses, rnn, zr, ga_col, gr_col, x2d, wa, wr, wo, B, S, *, tm):
    T, D = x2d.shape
    assert S % tm == 0 and ga_col % D == 0 and gr_col % D == 0
    heads = _attn_heads_per_step(S)
    LW = (HEADS_PER_GROUP // heads) * _LANES
    tiles_per_seq = S // tm
    row = lambda w: pl.BlockSpec((tm, w), lambda i: (i, 0))
    const = lambda a: pl.BlockSpec(a.shape, lambda i: (0, 0), pipeline_mode=pl.Buffered(1))

    def group_spec(dil, w):
        if dil == 1:
            return row(w)
        assert tm % (dil * _SUBLANES) == 0
        return pl.BlockSpec((None, dil, tm // dil, w),
                            lambda i: (i // tiles_per_seq, 0, i % tiles_per_seq, 0))

    scratch = []
    for _, dil in DILATED_GROUPS:
        if dil > 1:
            scratch += [pltpu.VMEM((ATTN_OUT_W // _LANES, tm, _LANES), _F32),
                        pltpu.VMEM((LW // _LANES, tm, _LANES), _F32)]
    os_ = [o.reshape(T, ATTN_OUT_W) if dil == 1 else o for o, (_, dil) in zip(os_, DILATED_GROUPS)]
    lses = [l.reshape(T, LW) if dil == 1 else l for l, (_, dil) in zip(lses, DILATED_GROUPS)]
    return pl.pallas_call(
        functools.partial(_merge_kernel, heads=heads),
        grid=(T // tm,),
        in_specs=[group_spec(dil, ATTN_OUT_W) for _, dil in DILATED_GROUPS]
        + [group_spec(dil, LW) for _, dil in DILATED_GROUPS]
        + [row(LRU_W),
           pl.BlockSpec((tm, D), lambda i: (i, ga_col // D)),
           pl.BlockSpec((tm, D), lambda i: (i, gr_col // D)),
           row(D), const(wa), const(wr), const(wo)],
        out_specs=row(D),
        out_shape=jax.ShapeDtypeStruct((T, D), _F32),
        scratch_shapes=scratch,
        compiler_params=_params(("parallel",)),
        name="merge_proj",
    )(*os_, *lses, rnn, zr, zr, x2d, wa, wr, wo)


def _mlp_kernel(x_ref, g_ref, w1_ref, w2_ref, gf_ref, o_ref, u_ref):
    j = pl.program_id(1)

    def ff_chunk():
        h = jnp.dot(u_ref[...], w1_ref[...], preferred_element_type=_F32)
        h = jnp.square(jnp.maximum(h, 0.0)).astype(_BF16)
        return jnp.dot(h, w2_ref[...], preferred_element_type=_F32)

    @pl.when(j == 0)
    def _():
        x = x_ref[...]
        u_ref[...] = _rms(x, g_ref[...]).astype(u_ref.dtype)
        o_ref[...] = x + ff_chunk()

    @pl.when(j > 0)
    def _():
        o_ref[...] += ff_chunk()

    @pl.when(j == pl.num_programs(1) - 1)
    def _():
        o_ref[...] = _rms(o_ref[...], gf_ref[...])


def _mlp(x2d, g, w1, w2, gf, *, tm, tf):
    T, D = x2d.shape
    F = w1.shape[1]
    assert T % tm == 0 and F % tf == 0
    return pl.pallas_call(
        _mlp_kernel,
        grid=(T // tm, F // tf),
        in_specs=[pl.BlockSpec((tm, D), lambda i, j: (i, 0)),
                  pl.BlockSpec((1, D), lambda i, j: (0, 0)),
                  pl.BlockSpec((D, tf), lambda i, j: (0, j)),
                  pl.BlockSpec((tf, D), lambda i, j: (j, 0)),
                  pl.BlockSpec((1, D), lambda i, j: (0, 0))],
        out_specs=pl.BlockSpec((tm, D), lambda i, j: (i, 0)),
        out_shape=jax.ShapeDtypeStruct((T, D), _F32),
        scratch_shapes=[pltpu.VMEM((tm, D), _BF16)],
        compiler_params=_params(("parallel", "arbitrary")),
        name="mlp_final_norm",
    )(x2d, g, w1, w2, gf)


_REST_GA = 0
_REST_GR = D_MODEL
_REST_RX = 2 * D_MODEL
_REST_RY = 2 * D_MODEL + LRU_W
_REST_SEGMENTS = ((2 * LRU_W, _REST_GA, D_MODEL), (2 * LRU_W + D_MODEL, _REST_GR, D_MODEL),
                  (0, _REST_RX, LRU_W), (LRU_W, _REST_RY, LRU_W))


def _layer(x, bias_tabs, p):
    B, S, D = x.shape
    T = B * S
    x2d = x.reshape(T, D)
    zgs = _qkv_proj(x2d, p["g_mix"], p["w_qkv"], B, S, tm=512)
    zr = _norm_proj(x2d, p["g_mix"], p["w_rest"], tm=256, segments=_REST_SEGMENTS, out_dtype=_F32)
    os_, lses = [], []
    for g, (_, dil) in enumerate(DILATED_GROUPS):
        o, lse = _attn_group(zgs[g], bias_tabs, g, dil, B, S)
        os_.append(o)
        lses.append(lse)
    rnn = _rglru(zr.reshape(B, S, -1), _REST_RX, _REST_RY, p["conv_w"], p["conv_b"],
                 p["w_gate"], p["b_gate"], p["lam"])
    x2 = _merge(os_, lses, rnn, zr, _REST_GA, _REST_GR, x2d,
                p["w_attn_o"], p["w_rnn_o"], p["w_out"], B, S, tm=256)
    y = _mlp(x2, p["g_mlp"], p["w_mlp_in"], p["w_mlp_out"], p["g_final"], tm=512, tf=2048)
    return y.reshape(B, S, D)


def kernel(x_prompt, x_sample, rel_bias, norm_mix_g, w_in, conv_w, conv_b, lru_wa, lru_ba,
           lru_wx, lru_bx, lru_lambda, w_attn_o, w_rnn_o, w_out, norm_mlp_g, w_mlp_in,
           w_mlp_out, norm_final_g):
    depth = w_in.shape[0]
    assert depth == 1, "the final norm is fused into the single layer's MLP kernel"
    bias_tabs = _bias_tables(rel_bias)
    w_qkv, w_rest = _cast_split(w_in[0], ((0, 3 * ATTN_W), (3 * ATTN_W, w_in.shape[2])), tm=256)
    w_gate = jnp.concatenate([lru_wa[0, 0], lru_wx[0, 0], lru_wa[0, 1], lru_wx[0, 1]],
                             axis=-1).astype(_BF16)
    b_gate = jnp.concatenate(
        [b.reshape(LRU_BLOCKS, 1, LRU_BLOCK_W)
         for b in (lru_ba[0, 0], lru_bx[0, 0], lru_ba[0, 1], lru_bx[0, 1])], axis=-1)
    p = dict(
        g_mix=norm_mix_g[0].reshape(1, D_MODEL),
        w_qkv=w_qkv, w_rest=w_rest,
        conv_w=conv_w[0], conv_b=conv_b[0].reshape(1, LRU_W),
        w_gate=w_gate, b_gate=0.5 * b_gate.astype(_F32), lam=lru_lambda[0],
        w_attn_o=w_attn_o[0].astype(_BF16), w_rnn_o=w_rnn_o[0].astype(_BF16),
        w_out=w_out[0].astype(_BF16),
        g_mlp=norm_mlp_g[0].reshape(1, D_MODEL),
        w_mlp_in=w_mlp_in[0].astype(_BF16), w_mlp_out=w_mlp_out[0].astype(_BF16),
        g_final=norm_final_g.reshape(1, D_MODEL),
    )
    return (_layer(x_prompt, bias_tabs, p), _layer(x_sample, bias_tabs, p))
```

```python
import functools
import math

import jax
import jax.numpy as jnp
import numpy as np
from jax import lax
from jax.experimental import pallas as pl
from jax.experimental.pallas import tpu as pltpu

D_MODEL = 2048
HEAD_DIM = 128
HEADS_PER_GROUP = 4
DILATED_GROUPS = ((128, 1), (512, 4), (2048, 16))
N_GROUPS = len(DILATED_GROUPS)
ATTN_W = N_GROUPS * HEADS_PER_GROUP * HEAD_DIM
ATTN_OUT_W = HEADS_PER_GROUP * HEAD_DIM
NUM_BUCKETS = 32
REL_MAX_DIST = 1024
LRU_W = 1536
LRU_BLOCKS = 12
LRU_BLOCK_W = LRU_W // LRU_BLOCKS
LRU_C = 8.0
CONV_W = 4
D_FF = 4 * D_MODEL
NORM_EPS = 1e-6
NEG_INF = -1e30

_F32 = jnp.float32
_BF16 = jnp.bfloat16
_V7X_VMEM_LIMIT_BYTES = 56 * 1024 * 1024
_LANES = 128
_SUBLANES = 8

_SIDE = DILATED_GROUPS[0][0] // (2 * DILATED_GROUPS[0][1])
assert all(w // (2 * d) == _SIDE for w, d in DILATED_GROUPS)
_QBLK = 2 * _SIDE
_KWIN = 4 * _SIDE
assert _QBLK == _LANES
_GROUP_QKV_W = 3 * ATTN_OUT_W


def _params(semantics):
    return pltpu.CompilerParams(dimension_semantics=semantics,
                                vmem_limit_bytes=_V7X_VMEM_LIMIT_BYTES)


def _rms(x, g):
    ms = jnp.mean(x * x, axis=-1, keepdims=True)
    return x * lax.rsqrt(ms + NORM_EPS) * g


def _lane_block(c):
    return slice(c * _LANES, (c + 1) * _LANES)


_LOG2E = math.log2(math.e)
_EDGE_VARIANTS = 4


def _rel_bucket_np(rel):
    half = NUM_BUCKETS // 2
    max_exact = half // 2
    n = np.abs(rel)
    nf = np.maximum(n, 1).astype(np.float32)
    ratio = np.log(nf / np.float32(max_exact)) / np.float32(math.log(REL_MAX_DIST / max_exact))
    large = max_exact + (ratio * np.float32(half - max_exact)).astype(np.int32)
    large = np.minimum(large, half - 1)
    return (np.where(rel > 0, half, 0) + np.where(n < max_exact, n, large)).astype(np.int32)


def _bias_kernel(tab_ref, idx_ref, o_ref):
    head = pl.program_id(0)
    idx = idx_ref[...]
    acc = jnp.zeros(idx.shape, _F32)
    for b in range(NUM_BUCKETS):
        acc = jnp.where(idx == b, tab_ref[b, head], acc)
    acc = acc * _LOG2E
    kj = lax.broadcasted_iota(jnp.int32, idx.shape, 1)
    in_band = idx >= 0
    after_start = kj >= _SIDE
    before_end = kj < _KWIN - _SIDE
    o_ref[0] = jnp.where(in_band, acc, NEG_INF)
    o_ref[1] = jnp.where(jnp.logical_and(in_band, after_start), acc, NEG_INF)
    o_ref[2] = jnp.where(jnp.logical_and(in_band, before_end), acc, NEG_INF)
    o_ref[3] = jnp.where(jnp.logical_and(in_band, jnp.logical_and(after_start, before_end)),
                         acc, NEG_INF)


def _bias_tables(rel_bias):
    qi = np.arange(_QBLK, dtype=np.int32)[:, None]
    kj = np.arange(_KWIN, dtype=np.int32)[None, :]
    rel = kj - _SIDE - qi
    band = np.abs(rel) <= _SIDE
    idx = np.stack([np.where(band, _rel_bucket_np(rel * dil), -1) for _, dil in DILATED_GROUPS])
    n_heads = N_GROUPS * HEADS_PER_GROUP
    return pl.pallas_call(
        _bias_kernel,
        grid=(n_heads,),
        in_specs=[pl.BlockSpec(memory_space=pltpu.SMEM),
                  pl.BlockSpec((None, _QBLK, _KWIN), lambda h: (h // HEADS_PER_GROUP, 0, 0))],
        out_specs=pl.BlockSpec((_EDGE_VARIANTS, None, _QBLK, _KWIN), lambda h: (0, h, 0, 0)),
        out_shape=jax.ShapeDtypeStruct((_EDGE_VARIANTS, n_heads, _QBLK, _KWIN), _F32),
        name="bias_tables",
    )(rel_bias.astype(_F32), jnp.asarray(idx.astype(np.int32)))


def _cast_split_kernel(w_ref, *o_refs, bounds):
    for o_ref, (start, stop) in zip(o_refs, bounds):
        o_ref[...] = w_ref[:, start:stop].astype(o_ref.dtype)


def _cast_split(w, bounds, *, tm):
    R, C = w.shape
    assert R % tm == 0
    return pl.pallas_call(
        functools.partial(_cast_split_kernel, bounds=bounds),
        grid=(R // tm,),
        in_specs=[pl.BlockSpec((tm, C), lambda i: (i, 0))],
        out_specs=[pl.BlockSpec((tm, stop - start), lambda i: (i, 0)) for start, stop in bounds],
        out_shape=[jax.ShapeDtypeStruct((R, stop - start), _BF16) for start, stop in bounds],
        compiler_params=_params(("parallel",)),
        name="cast_split",
    )(w)


_FOLD_STRIDE = 4
_Q_SCALE = HEAD_DIM ** -0.5 * _LOG2E


def _qkv_proj_kernel(x_ref, g_ref, w_ref, *refs):
    out_refs = refs[:N_GROUPS]
    u_ref, stage_ref, stage2_ref = refs[N_GROUPS:]
    tm = x_ref.shape[0]
    n_blocks = ATTN_OUT_W // _LANES
    u_ref[...] = _rms(x_ref[...], g_ref[...]).astype(u_ref.dtype)

    def fold(o_ref, dil, part, slot):
        for c in range(n_blocks):
            lanes = _lane_block(part * n_blocks + c)
            if dil == _FOLD_STRIDE:
                for r in range(dil):
                    o_ref[r, :, lanes] = stage_ref[
                        slot, c, pl.ds(r, tm // dil, stride=dil), :].astype(o_ref.dtype)
                continue
            assert dil == _FOLD_STRIDE ** 2
            for r0 in range(_FOLD_STRIDE):
                stage2_ref[c, r0] = stage_ref[
                    slot, c, pl.ds(r0, tm // _FOLD_STRIDE, stride=_FOLD_STRIDE), :]
            for r0 in range(_FOLD_STRIDE):
                for r1 in range(_FOLD_STRIDE):
                    o_ref[r0 + _FOLD_STRIDE * r1, :, lanes] = stage2_ref[
                        c, r0, pl.ds(r1, tm // dil, stride=_FOLD_STRIDE), :].astype(o_ref.dtype)

    pending, slot = None, 0
    for g, (_, dil) in sorted(enumerate(DILATED_GROUPS), key=lambda e: -e[1][1]):
        o_ref = out_refs[g]
        for part in range(3):
            col0 = part * ATTN_W + g * ATTN_OUT_W
            res = jnp.dot(u_ref[...], w_ref[:, col0:col0 + ATTN_OUT_W],
                          preferred_element_type=_F32)
            if part == 0:
                res = res * _Q_SCALE
            if dil == 1:
                o_ref[:, part * ATTN_OUT_W:(part + 1) * ATTN_OUT_W] = res.astype(o_ref.dtype)
            else:
                for c in range(n_blocks):
                    stage_ref[slot, c] = res[:, _lane_block(c)]
            if pending is not None:
                fold(*pending)
                pending = None
            if dil > 1:
                pending = (o_ref, dil, part, slot)
                slot = 1 - slot
    if pending is not None:
        fold(*pending)


def _qkv_proj(x2d, g, w, B, S, *, tm):
    T, D = x2d.shape
    assert T == B * S and S % tm == 0
    tiles_per_seq = S // tm
    out_shape, out_specs = [], []
    for _, dil in DILATED_GROUPS:
        assert tm % (dil * 16) == 0
        if dil == 1:
            out_shape.append(jax.ShapeDtypeStruct((T, _GROUP_QKV_W), _BF16))
            out_specs.append(pl.BlockSpec((tm, _GROUP_QKV_W), lambda i: (i, 0)))
        else:
            out_shape.append(jax.ShapeDtypeStruct((B, dil, S // dil, _GROUP_QKV_W), _BF16))
            out_specs.append(pl.BlockSpec(
                (None, dil, tm // dil, _GROUP_QKV_W),
                lambda i: (i // tiles_per_seq, 0, i % tiles_per_seq, 0)))
    return pl.pallas_call(
        _qkv_proj_kernel,
        grid=(T // tm,),
        in_specs=[pl.BlockSpec((tm, D), lambda i: (i, 0)),
                  pl.BlockSpec((1, D), lambda i: (0, 0)),
                  pl.BlockSpec(w.shape, lambda i: (0, 0), pipeline_mode=pl.Buffered(1))],
        out_specs=out_specs,
        out_shape=out_shape,
        scratch_shapes=[pltpu.VMEM((tm, D), _BF16),
                        pltpu.VMEM((2, ATTN_OUT_W // _LANES, tm, _LANES), _F32),
                        pltpu.VMEM((ATTN_OUT_W // _LANES, _FOLD_STRIDE, tm // _FOLD_STRIDE, _LANES),
                                   _F32)],
        compiler_params=_params(("parallel",)),
        name="qkv_proj",
    )(x2d, g, w)


def _norm_proj_kernel(x_ref, g_ref, w_ref, o_ref, u_ref, *, segments):
    u_ref[...] = _rms(x_ref[...], g_ref[...]).astype(u_ref.dtype)
    for src, dst, width in segments:
        o_ref[:, dst:dst + width] = jnp.dot(u_ref[...], w_ref[:, src:src + width],
                                            preferred_element_type=_F32).astype(o_ref.dtype)


def _norm_proj(x2d, g, w, *, tm, segments, out_dtype):
    T, D = x2d.shape
    N = w.shape[1]
    assert T % tm == 0 and sum(s[2] for s in segments) == N
    return pl.pallas_call(
        functools.partial(_norm_proj_kernel, segments=segments),
        grid=(T // tm,),
        in_specs=[pl.BlockSpec((tm, D), lambda i: (i, 0)),
                  pl.BlockSpec((1, D), lambda i: (0, 0)),
                  pl.BlockSpec(w.shape, lambda i: (0, 0), pipeline_mode=pl.Buffered(1))],
        out_specs=pl.BlockSpec((tm, N), lambda i: (i, 0)),
        out_shape=jax.ShapeDtypeStruct((T, N), out_dtype),
        scratch_shapes=[pltpu.VMEM((tm, D), _BF16)],
        compiler_params=_params(("parallel",)),
        name="norm_proj",
    )(x2d, g, w)


_ML_SHIFT = _LANES // 2


def _attn_kernel(z_ref, b_ref, o_ref, ml_ref, kpad, vpad, *, L, dil, heads, unroll):
    steps = HEADS_PER_GROUP // heads
    if steps == 1:
        _attn_heads(z_ref, b_ref, o_ref, ml_ref, kpad, vpad, 0, L=L, dil=dil, heads=heads,
                    unroll=unroll)
        return
    for hp in range(steps):
        @pl.when(pl.program_id(1) == hp)
        def _(hp=hp):
            _attn_heads(z_ref, b_ref, o_ref, ml_ref, kpad, vpad, hp * heads * HEAD_DIM,
                        L=L, dil=dil, heads=heads, unroll=unroll)


def _attn_heads(z_ref, b_ref, o_ref, ml_ref, kpad, vpad, col0, *, L, dil, heads, unroll):
    W = heads * HEAD_DIM
    zeros = jnp.zeros((_SIDE, W), _BF16)
    for r in range(dil):
        for pad, part in ((kpad, 1), (vpad, 2)):
            lo = part * ATTN_OUT_W + col0
            pad[r, 0:_SIDE, :] = zeros
            pad[r, _SIDE + L:2 * _SIDE + L, :] = zeros
            pad[r, _SIDE:_SIDE + L, :] = z_ref[r, :, lo:lo + W]

    lane = lax.broadcasted_iota(jnp.int32, (_QBLK, _LANES), 1)
    nblk = L // _QBLK
    assert nblk & (nblk - 1) == 0
    blk_shift = nblk.bit_length() - 1

    def body(n, carry):
        r = lax.shift_right_logical(n, blk_shift)
        blk = n & (nblk - 1)
        qs = pl.multiple_of(blk * _QBLK, _QBLK)
        edge = jnp.where(blk == 0, 1, 0) + jnp.where(blk == nblk - 1, 2, 0)
        ml_tile = jnp.zeros((_QBLK, _LANES), _F32)
        for h in range(heads):
            cols = _lane_block(h)
            q = z_ref[r, pl.ds(qs, _QBLK), col0 + h * HEAD_DIM:col0 + (h + 1) * HEAD_DIM]
            kw = kpad[r, pl.ds(qs, _KWIN), cols]
            vw = vpad[r, pl.ds(qs, _KWIN), cols]
            s = lax.dot_general(q, kw, (((1,), (1,)), ((), ())),
                                preferred_element_type=_F32) + b_ref[edge, h]
            m = jnp.max(s, axis=-1, keepdims=True)
            p = jnp.exp2(s - m)
            l = jnp.sum(p, axis=-1, keepdims=True)
            o_ref[r, pl.ds(qs, _QBLK), cols] = jnp.dot(p.astype(_BF16), vw,
                                                       preferred_element_type=_F32)
            ml_tile = jnp.where(lane == h, m, jnp.where(lane == _ML_SHIFT + h, l, ml_tile))
        ml_ref[r, pl.ds(qs, _QBLK), :] = ml_tile
        return carry

    lax.fori_loop(0, dil * nblk, body, 0, unroll=unroll)


def _attn_heads_per_step(S):
    return HEADS_PER_GROUP if S <= 2048 else HEADS_PER_GROUP // 2


def _attn_group(zg, bias, group, dil, B, S):
    L = S // dil
    assert L % _QBLK == 0
    heads = _attn_heads_per_step(S)
    steps = HEADS_PER_GROUP // heads
    W = heads * HEAD_DIM
    zg = zg.reshape(B, dil, L, _GROUP_QKV_W)
    return pl.pallas_call(
        functools.partial(_attn_kernel, L=L, dil=dil, heads=heads,
                          unroll=min(64 // heads, dil * L // _QBLK)),
        grid=(B, steps),
        in_specs=[pl.BlockSpec((None, dil, L, _GROUP_QKV_W), lambda b, hp: (b, 0, 0, 0)),
                  pl.BlockSpec((_EDGE_VARIANTS, heads, _QBLK, _KWIN),
                               lambda b, hp: (0, group * steps + hp, 0, 0))],
        out_specs=[pl.BlockSpec((None, dil, L, W), lambda b, hp: (b, 0, 0, hp)),
                   pl.BlockSpec((None, dil, L, _LANES), lambda b, hp: (b, 0, 0, hp))],
        out_shape=[jax.ShapeDtypeStruct((B, dil, L, ATTN_OUT_W), _F32),
                   jax.ShapeDtypeStruct((B, dil, L, steps * _LANES), _F32)],
        scratch_shapes=[pltpu.VMEM((dil, L + 2 * _SIDE, W), _BF16),
                        pltpu.VMEM((dil, L + 2 * _SIDE, W), _BF16)],
        compiler_params=_params(("parallel", "arbitrary")),
        name=f"dilated_attn_g{group}",
    )(zg, bias)


_CHUNKS = 4 * _SUBLANES
_PITCH_PAD = 4
_GATE_TAUS = 8
_SCAN_UNROLL = 16
_F32_TINY = float(np.finfo(np.float32).tiny)


def _sigmoid(x):
    return 0.5 * jnp.tanh(0.5 * x) + 0.5


def _gelu_tanh(x):
    k = math.sqrt(2.0 / math.pi)
    half = 0.5 * x
    return half * jnp.tanh(x * (k + (k * 0.044715) * (x * x))) + half


def _chunk_rows(ref, start, pitch):
    return jnp.concatenate(
        [ref[pl.ds(start + half * _SUBLANES * pitch, _SUBLANES, stride=pitch), :]
         for half in range(_CHUNKS // _SUBLANES)], axis=0)


def _rglru_kernel(rx_ref, ry_ref, cw_ref, cb_ref, wg_ref, bg_ref, lam_ref, o_ref,
                  xs, a_f, h_f, a_b, h_b, *, S):
    BW = LRU_BLOCK_W
    Lc = S // _CHUNKS
    P = Lc + _PITCH_PAD
    left = CONV_W // 2

    for c in range(_CHUNKS):
        lo = c * Lc - left
        hi = lo + P
        s_lo, s_hi = max(lo, 0), min(hi, S)
        xs[c * P + (s_lo - lo):c * P + (s_hi - lo), :] = rx_ref[s_lo:s_hi, :]
        if lo < 0:
            xs[c * P:c * P - lo, :] = jnp.zeros((-lo, BW), _F32)
        if hi > S:
            xs[c * P + (S - lo):(c + 1) * P, :] = jnp.zeros((hi - S, BW), _F32)

    cw = 0.5 * cw_ref[...]
    cb = 0.5 * cb_ref[...]
    half_decay = (-0.5 * LRU_C) * jax.nn.softplus(-lam_ref[...])
    TB = _GATE_TAUS

    def gate_body(bi, carry):
        t0 = bi * TB
        taps = [_chunk_rows(xs, t0 + j, P) for j in range(TB + CONV_W - 1)]
        rows = []
        for tau in range(TB):
            xc = cb
            for t in range(CONV_W):
                xc = xc + taps[tau + t] * cw[t:t + 1]
            rows.append(xc)
        half_x = jnp.concatenate(rows, axis=0)
        g = jnp.tanh(jnp.dot(half_x.astype(_BF16), wg_ref[...], preferred_element_type=_F32)
                     + bg_ref[...])
        r0 = pl.multiple_of(t0 * _CHUNKS, TB * _CHUNKS)
        for d, (a_ref, u_ref) in enumerate(((a_f, h_f), (a_b, h_b))):
            log_a = half_decay[d:d + 1] * g[:, (2 * d) * BW:(2 * d + 1) * BW] + half_decay[d:d + 1]
            ix = (g[:, (2 * d + 1) * BW:(2 * d + 2) * BW] + 1.0) * half_x
            a = jnp.exp(log_a)
            om = jnp.tanh(log_a) * (-1.0 - a * a)
            root = om * lax.rsqrt(jnp.maximum(om, _F32_TINY))
            a_ref[pl.ds(r0, TB * _CHUNKS), :] = a
            u_ref[pl.ds(r0, TB * _CHUNKS), :] = root * ix
        return carry

    lax.fori_loop(0, Lc // TB, gate_body, 0, unroll=4)

    def scan_body(tau, carry):
        hf, pf, hb, pb = carry
        rf = pl.multiple_of(tau * _CHUNKS, _CHUNKS)
        rb = pl.multiple_of((Lc - 1 - tau) * _CHUNKS, _CHUNKS)
        af = a_f[pl.ds(rf, _CHUNKS), :]
        hf = af * hf + h_f[pl.ds(rf, _CHUNKS), :]
        pf = af * pf
        h_f[pl.ds(rf, _CHUNKS), :] = hf
        a_f[pl.ds(rf, _CHUNKS), :] = pf
        ab = a_b[pl.ds(rb, _CHUNKS), :]
        hb = ab * hb + h_b[pl.ds(rb, _CHUNKS), :]
        pb = ab * pb
        h_b[pl.ds(rb, _CHUNKS), :] = hb
        a_b[pl.ds(rb, _CHUNKS), :] = pb
        return hf, pf, hb, pb

    zero = jnp.zeros((_CHUNKS, BW), _F32)
    one = jnp.ones((_CHUNKS, BW), _F32)
    h_end, p_end, h_beg, p_beg = lax.fori_loop(0, Lc, scan_body, (zero, one, zero, one),
                                               unroll=_SCAN_UNROLL)

    row = lax.broadcasted_iota(jnp.int32, (_CHUNKS, BW), 0)
    in_f, in_b = zero, zero
    prev_f = jnp.zeros((1, BW), _F32)
    prev_b = jnp.zeros((1, BW), _F32)
    for c in range(1, _CHUNKS):
        prev_f = h_end[c - 1:c] + p_end[c - 1:c] * prev_f
        in_f = jnp.where(row == c, prev_f, in_f)
        cb_ = _CHUNKS - 1 - c
        prev_b = h_beg[cb_ + 1:cb_ + 2] + p_beg[cb_ + 1:cb_ + 2] * prev_b
        in_b = jnp.where(row == cb_, prev_b, in_b)

    def fix_body(tau, carry):
        r = pl.multiple_of(tau * _CHUNKS, _CHUNKS)
        h = (h_f[pl.ds(r, _CHUNKS), :] + a_f[pl.ds(r, _CHUNKS), :] * in_f
             + h_b[pl.ds(r, _CHUNKS), :] + a_b[pl.ds(r, _CHUNKS), :] * in_b)
        for half in range(_CHUNKS // _SUBLANES):
            xs[pl.ds(tau + half * _SUBLANES * P, _SUBLANES, stride=P), :] = \
                h[half * _SUBLANES:(half + 1) * _SUBLANES]
        return carry

    lax.fori_loop(0, Lc, fix_body, 0, unroll=_SCAN_UNROLL)

    for c in range(_CHUNKS):
        t = slice(c * Lc, (c + 1) * Lc)
        o_ref[t, :] = (xs[c * P:c * P + Lc, :] * _gelu_tanh(ry_ref[t, :])).astype(o_ref.dtype)


def _rglru(zr, rx_col, ry_col, conv_w, conv_b, wg, bg, lam):
    B, S, _ = zr.shape
    BW = LRU_BLOCK_W
    Lc = S // _CHUNKS
    assert S % (_CHUNKS * _GATE_TAUS) == 0 and Lc % _SCAN_UNROLL == 0
    assert rx_col % BW == 0 and ry_col % BW == 0
    seq = pltpu.VMEM((S, BW), _F32)
    out = pl.pallas_call(
        functools.partial(_rglru_kernel, S=S),
        grid=(B, LRU_BLOCKS),
        in_specs=[pl.BlockSpec((None, S, BW), lambda b, n: (b, 0, rx_col // BW + n)),
                  pl.BlockSpec((None, S, BW), lambda b, n: (b, 0, ry_col // BW + n)),
                  pl.BlockSpec((CONV_W, BW), lambda b, n: (0, n)),
                  pl.BlockSpec((1, BW), lambda b, n: (0, n)),
                  pl.BlockSpec((None, BW, 4 * BW), lambda b, n: (n, 0, 0)),
                  pl.BlockSpec((None, 1, 4 * BW), lambda b, n: (n, 0, 0)),
                  pl.BlockSpec((2, BW), lambda b, n: (0, n))],
        out_specs=pl.BlockSpec((None, S, BW), lambda b, n: (b, 0, n)),
        out_shape=jax.ShapeDtypeStruct((B, S, LRU_W), _BF16),
        scratch_shapes=[pltpu.VMEM((_CHUNKS * (Lc + _PITCH_PAD), BW), _F32), seq, seq, seq, seq],
        compiler_params=_params(("parallel", "parallel")),
        name="conv_rglru",
    )(zr, zr, conv_w, conv_b, wg, bg, lam)
    return out.reshape(B * S, LRU_W)


def _unfold(src_ref, dst_ref, dil):
    rows = src_ref.shape[1]
    for r in range(dil):
        for c in range(dst_ref.shape[0]):
            dst_ref[c, pl.ds(r, rows, stride=dil), :] = src_ref[r, :, _lane_block(c)]


def _merge_kernel(*refs, heads):
    o_refs = refs[0:N_GROUPS]
    l_refs = refs[N_GROUPS:2 * N_GROUPS]
    rnn_ref, gate_ref, x_ref, wa_ref, wr_ref, wo_ref, out_ref = refs[2 * N_GROUPS:2 * N_GROUPS + 7]
    scratch = refs[2 * N_GROUPS + 7:]
    D = x_ref.shape[1]

    pr = jnp.dot(rnn_ref[...], wr_ref[...], preferred_element_type=_F32)

    o_get, l_get, k = [], [], 0
    for g, (_, dil) in enumerate(DILATED_GROUPS):
        if dil == 1:
            o_get.append(lambda c, ref=o_refs[g]: ref[:, _lane_block(c)])
            l_get.append(lambda c, ref=l_refs[g]: ref[:, _lane_block(c)])
        else:
            o_scr, l_scr = scratch[k], scratch[k + 1]
            k += 2
            _unfold(o_refs[g], o_scr, dil)
            _unfold(l_refs[g], l_scr, dil)
            o_get.append(lambda c, ref=o_scr: ref[c])
            l_get.append(lambda c, ref=l_scr: ref[c])

    wts = []
    for step in range(HEADS_PER_GROUP // heads):
        ml = [get(step) for get in l_get]
        mx = functools.reduce(jnp.maximum, ml)
        es = [jnp.exp2(t - mx) for t in ml]
        den = functools.reduce(jnp.add, [e * pltpu.roll(t, _ML_SHIFT, axis=1)
                                         for e, t in zip(es, ml)])
        inv = 1.0 / den
        wts.append([e * inv for e in es])
    attn = []
    for h in range(HEADS_PER_GROUP):
        w = wts[h // heads]
        c = h % heads
        attn.append(functools.reduce(
            jnp.add, [w[g][:, c:c + 1] * o_get[g](h) for g in range(N_GROUPS)]))
    attn = jnp.concatenate(attn, axis=-1).astype(_BF16)
    pa = jnp.dot(attn, wa_ref[...], preferred_element_type=_F32)
    merged = _sigmoid(gate_ref[:, :D]) * pa + _sigmoid(gate_ref[:, D:]) * pr
    out_ref[...] = x_ref[...] + jnp.dot(merged.astype(_BF16), wo_ref[...],
                                        preferred_element_type=_F32)


def _merge(os_, lses, rnn, zr, ga_col, gr_col, x2d, wa, wr, wo, B, S, *, tm):
    T, D = x2d.shape
    assert S % tm == 0 and ga_col % (2 * D) == 0 and gr_col == ga_col + D
    heads = _attn_heads_per_step(S)
    LW = (HEADS_PER_GROUP // heads) * _LANES
    tiles_per_seq = S // tm
    row = lambda w: pl.BlockSpec((tm, w), lambda i: (i, 0))
    const = lambda a: pl.BlockSpec(a.shape, lambda i: (0, 0), pipeline_mode=pl.Buffered(1))

    def group_spec(dil, w):
        if dil == 1:
            return row(w)
        assert tm % (dil * _SUBLANES) == 0
        return pl.BlockSpec((None, dil, tm // dil, w),
                            lambda i: (i // tiles_per_seq, 0, i % tiles_per_seq, 0))

    scratch = []
    for _, dil in DILATED_GROUPS:
        if dil > 1:
            scratch += [pltpu.VMEM((ATTN_OUT_W // _LANES, tm, _LANES), _F32),
                        pltpu.VMEM((LW // _LANES, tm, _LANES), _F32)]
    os_ = [o.reshape(T, ATTN_OUT_W) if dil == 1 else o for o, (_, dil) in zip(os_, DILATED_GROUPS)]
    lses = [l.reshape(T, LW) if dil == 1 else l for l, (_, dil) in zip(lses, DILATED_GROUPS)]
    return pl.pallas_call(
        functools.partial(_merge_kernel, heads=heads),
        grid=(T // tm,),
        in_specs=[group_spec(dil, ATTN_OUT_W) for _, dil in DILATED_GROUPS]
        + [group_spec(dil, LW) for _, dil in DILATED_GROUPS]
        + [row(LRU_W),
           pl.BlockSpec((tm, 2 * D), lambda i: (i, ga_col // (2 * D))),
           row(D), const(wa), const(wr), const(wo)],
        out_specs=row(D),
        out_shape=jax.ShapeDtypeStruct((T, D), _F32),
        scratch_shapes=scratch,
        compiler_params=_params(("parallel",)),
        name="merge_proj",
    )(*os_, *lses, rnn, zr, x2d, wa, wr, wo)


def _mlp_kernel(x_ref, g_ref, w1_ref, w2_ref, gf_ref, o_ref, u_ref):
    j = pl.program_id(1)

    def ff_chunk():
        h = jnp.dot(u_ref[...], w1_ref[...], preferred_element_type=_F32)
        h = jnp.square(jnp.maximum(h, 0.0)).astype(_BF16)
        return jnp.dot(h, w2_ref[...], preferred_element_type=_F32)

    @pl.when(j == 0)
    def _():
        x = x_ref[...]
        u_ref[...] = _rms(x, g_ref[...]).astype(u_ref.dtype)
        o_ref[...] = x + ff_chunk()

    @pl.when(j > 0)
    def _():
        o_ref[...] += ff_chunk()

    @pl.when(j == pl.num_programs(1) - 1)
    def _():
        o_ref[...] = _rms(o_ref[...], gf_ref[...])


def _mlp(x2d, g, w1, w2, gf, *, tm, tf):
    T, D = x2d.shape
    F = w1.shape[1]
    assert T % tm == 0 and F % tf == 0
    return pl.pallas_call(
        _mlp_kernel,
        grid=(T // tm, F // tf),
        in_specs=[pl.BlockSpec((tm, D), lambda i, j: (i, 0)),
                  pl.BlockSpec((1, D), lambda i, j: (0, 0)),
                  pl.BlockSpec((D, tf), lambda i, j: (0, j)),
                  pl.BlockSpec((tf, D), lambda i, j: (j, 0)),
                  pl.BlockSpec((1, D), lambda i, j: (0, 0))],
        out_specs=pl.BlockSpec((tm, D), lambda i, j: (i, 0)),
        out_shape=jax.ShapeDtypeStruct((T, D), _F32),
        scratch_shapes=[pltpu.VMEM((tm, D), _BF16)],
        compiler_params=_params(("parallel", "arbitrary")),
        name="mlp_final_norm",
    )(x2d, g, w1, w2, gf)


_REST_GA = 0
_REST_GR = D_MODEL
_REST_RX = 2 * D_MODEL
_REST_RY = 2 * D_MODEL + LRU_W
_REST_SEGMENTS = ((2 * LRU_W, _REST_GA, D_MODEL), (2 * LRU_W + D_MODEL, _REST_GR, D_MODEL),
                  (0, _REST_RX, LRU_W), (LRU_W, _REST_RY, LRU_W))


def _layer(x, bias_tabs, p):
    B, S, D = x.shape
    T = B * S
    x2d = x.reshape(T, D)
    zgs = _qkv_proj(x2d, p["g_mix"], p["w_qkv"], B, S, tm=512)
    zr = _norm_proj(x2d, p["g_mix"], p["w_rest"], tm=256, segments=_REST_SEGMENTS, out_dtype=_F32)
    os_, lses = [], []
    for g, (_, dil) in enumerate(DILATED_GROUPS):
        o, lse = _attn_group(zgs[g], bias_tabs, g, dil, B, S)
        os_.append(o)
        lses.append(lse)
    rnn = _rglru(zr.reshape(B, S, -1), _REST_RX, _REST_RY, p["conv_w"], p["conv_b"],
                 p["w_gate"], p["b_gate"], p["lam"])
    x2 = _merge(os_, lses, rnn, zr, _REST_GA, _REST_GR, x2d,
                p["w_attn_o"], p["w_rnn_o"], p["w_out"], B, S, tm=256)
    y = _mlp(x2, p["g_mlp"], p["w_mlp_in"], p["w_mlp_out"], p["g_final"], tm=512, tf=2048)
    return y.reshape(B, S, D)


def kernel(x_prompt, x_sample, rel_bias, norm_mix_g, w_in, conv_w, conv_b, lru_wa, lru_ba,
           lru_wx, lru_bx, lru_lambda, w_attn_o, w_rnn_o, w_out, norm_mlp_g, w_mlp_in,
           w_mlp_out, norm_final_g):
    depth = w_in.shape[0]
    assert depth == 1, "the final norm is fused into the single layer's MLP kernel"
    bias_tabs = _bias_tables(rel_bias)
    w_qkv, w_rest = _cast_split(w_in[0], ((0, 3 * ATTN_W), (3 * ATTN_W, w_in.shape[2])), tm=256)
    w_gate = jnp.concatenate([lru_wa[0, 0], lru_wx[0, 0], lru_wa[0, 1], lru_wx[0, 1]],
                             axis=-1).astype(_BF16)
    b_gate = jnp.concatenate(
        [b.reshape(LRU_BLOCKS, 1, LRU_BLOCK_W)
         for b in (lru_ba[0, 0], lru_bx[0, 0], lru_ba[0, 1], lru_bx[0, 1])], axis=-1)
    p = dict(
        g_mix=norm_mix_g[0].reshape(1, D_MODEL),
        w_qkv=w_qkv, w_rest=w_rest,
        conv_w=conv_w[0], conv_b=conv_b[0].reshape(1, LRU_W),
        w_gate=w_gate, b_gate=0.5 * b_gate.astype(_F32), lam=lru_lambda[0],
        w_attn_o=w_attn_o[0].astype(_BF16), w_rnn_o=w_rnn_o[0].astype(_BF16),
        w_out=w_out[0].astype(_BF16),
        g_mlp=norm_mlp_g[0].reshape(1, D_MODEL),
        w_mlp_in=w_mlp_in[0].astype(_BF16), w_mlp_out=w_mlp_out[0].astype(_BF16),
        g_final=norm_final_g.reshape(1, D_MODEL),
    )
    return (_layer(x_prompt, bias_tabs, p), _layer(x_sample, bias_tabs, p))
```

```python
import functools
import math

import jax
import jax.numpy as jnp
import numpy as np
from jax import lax
from jax.experimental import pallas as pl
from jax.experimental.pallas import tpu as pltpu

D_MODEL = 2048
HEAD_DIM = 128
HEADS_PER_GROUP = 4
DILATED_GROUPS = ((128, 1), (512, 4), (2048, 16))
N_GROUPS = len(DILATED_GROUPS)
ATTN_W = N_GROUPS * HEADS_PER_GROUP * HEAD_DIM
ATTN_OUT_W = HEADS_PER_GROUP * HEAD_DIM
NUM_BUCKETS = 32
REL_MAX_DIST = 1024
LRU_W = 1536
LRU_BLOCKS = 12
LRU_BLOCK_W = LRU_W // LRU_BLOCKS
LRU_C = 8.0
CONV_W = 4
D_FF = 4 * D_MODEL
NORM_EPS = 1e-6
NEG_INF = -1e30

_F32 = jnp.float32
_BF16 = jnp.bfloat16
_V7X_VMEM_LIMIT_BYTES = 56 * 1024 * 1024
_LANES = 128
_SUBLANES = 8
_BF16_SUBLANES = 2 * _SUBLANES

_QKV_TM = 512
_REST_TM = 256
_MERGE_TM = 256
_MLP_TM = 512
_MLP_TF = 2048
_CAST_TM = 256

_SIDE = DILATED_GROUPS[0][0] // (2 * DILATED_GROUPS[0][1])
assert all(w // (2 * d) == _SIDE for w, d in DILATED_GROUPS)
_QBLK = 2 * _SIDE
_KWIN = 4 * _SIDE
assert _QBLK == _LANES
_GROUP_QKV_W = 3 * ATTN_OUT_W


def _params(semantics):
    return pltpu.CompilerParams(dimension_semantics=semantics,
                                vmem_limit_bytes=_V7X_VMEM_LIMIT_BYTES)


def _rms(x, g):
    ms = jnp.mean(x * x, axis=-1, keepdims=True)
    return x * lax.rsqrt(ms + NORM_EPS) * g


def _lane_block(c):
    return slice(c * _LANES, (c + 1) * _LANES)


_LOG2E = math.log2(math.e)
_EDGE_VARIANTS = 4


def _rel_bucket_np(rel):
    half = NUM_BUCKETS // 2
    max_exact = half // 2
    n = np.abs(rel)
    nf = np.maximum(n, 1).astype(np.float32)
    ratio = np.log(nf / np.float32(max_exact)) / np.float32(math.log(REL_MAX_DIST / max_exact))
    large = max_exact + (ratio * np.float32(half - max_exact)).astype(np.int32)
    large = np.minimum(large, half - 1)
    return (np.where(rel > 0, half, 0) + np.where(n < max_exact, n, large)).astype(np.int32)


def _bias_kernel(tab_ref, idx_ref, o_ref):
    head = pl.program_id(0)
    idx = idx_ref[...]
    acc = jnp.zeros(idx.shape, _F32)
    for b in range(NUM_BUCKETS):
        acc = jnp.where(idx == b, tab_ref[b, head], acc)
    acc = acc * _LOG2E
    kj = lax.broadcasted_iota(jnp.int32, idx.shape, 1)
    in_band = idx >= 0
    after_start = kj >= _SIDE
    before_end = kj < _KWIN - _SIDE
    o_ref[0] = jnp.where(in_band, acc, NEG_INF)
    o_ref[1] = jnp.where(jnp.logical_and(in_band, after_start), acc, NEG_INF)
    o_ref[2] = jnp.where(jnp.logical_and(in_band, before_end), acc, NEG_INF)
    o_ref[3] = jnp.where(jnp.logical_and(in_band, jnp.logical_and(after_start, before_end)),
                         acc, NEG_INF)


def _bias_tables(rel_bias):
    qi = np.arange(_QBLK, dtype=np.int32)[:, None]
    kj = np.arange(_KWIN, dtype=np.int32)[None, :]
    rel = kj - _SIDE - qi
    band = np.abs(rel) <= _SIDE
    idx = np.stack([np.where(band, _rel_bucket_np(rel * dil), -1) for _, dil in DILATED_GROUPS])
    n_heads = N_GROUPS * HEADS_PER_GROUP
    return pl.pallas_call(
        _bias_kernel,
        grid=(n_heads,),
        in_specs=[pl.BlockSpec(memory_space=pltpu.SMEM),
                  pl.BlockSpec((None, _QBLK, _KWIN), lambda h: (h // HEADS_PER_GROUP, 0, 0))],
        out_specs=pl.BlockSpec((_EDGE_VARIANTS, None, _QBLK, _KWIN), lambda h: (0, h, 0, 0)),
        out_shape=jax.ShapeDtypeStruct((_EDGE_VARIANTS, n_heads, _QBLK, _KWIN), _F32),
        name="bias_tables",
    )(rel_bias.astype(_F32), jnp.asarray(idx.astype(np.int32)))


def _cast_split_kernel(w_ref, *o_refs, bounds):
    for o_ref, (start, stop) in zip(o_refs, bounds):
        o_ref[...] = w_ref[:, start:stop].astype(o_ref.dtype)


def _cast_split(w, bounds, *, tm):
    R, C = w.shape
    assert R % tm == 0
    return pl.pallas_call(
        functools.partial(_cast_split_kernel, bounds=bounds),
        grid=(R // tm,),
        in_specs=[pl.BlockSpec((tm, C), lambda i: (i, 0))],
        out_specs=[pl.BlockSpec((tm, stop - start), lambda i: (i, 0)) for start, stop in bounds],
        out_shape=[jax.ShapeDtypeStruct((R, stop - start), _BF16) for start, stop in bounds],
        compiler_params=_params(("parallel",)),
        name="cast_split",
    )(w)


_FOLD_STRIDE = 4
_Q_SCALE = HEAD_DIM ** -0.5 * _LOG2E


def _qkv_proj_kernel(x_ref, g_ref, w_ref, *refs):
    out_refs = refs[:N_GROUPS]
    u_ref, stage_ref, stage2_ref = refs[N_GROUPS:]
    tm = x_ref.shape[0]
    n_blocks = ATTN_OUT_W // _LANES
    u_ref[...] = _rms(x_ref[...], g_ref[...]).astype(u_ref.dtype)

    def fold(o_ref, dil, part, slot):
        for c in range(n_blocks):
            lanes = _lane_block(part * n_blocks + c)
            if dil == _FOLD_STRIDE:
                for r in range(dil):
                    o_ref[r, :, lanes] = stage_ref[
                        slot, c, pl.ds(r, tm // dil, stride=dil), :].astype(o_ref.dtype)
                continue
            assert dil == _FOLD_STRIDE ** 2
            for r0 in range(_FOLD_STRIDE):
                stage2_ref[c, r0] = stage_ref[
                    slot, c, pl.ds(r0, tm // _FOLD_STRIDE, stride=_FOLD_STRIDE), :]
            for r0 in range(_FOLD_STRIDE):
                for r1 in range(_FOLD_STRIDE):
                    o_ref[r0 + _FOLD_STRIDE * r1, :, lanes] = stage2_ref[
                        c, r0, pl.ds(r1, tm // dil, stride=_FOLD_STRIDE), :].astype(o_ref.dtype)

    pending, slot = None, 0
    for g, (_, dil) in sorted(enumerate(DILATED_GROUPS), key=lambda e: -e[1][1]):
        o_ref = out_refs[g]
        for part in range(3):
            col0 = part * ATTN_W + g * ATTN_OUT_W
            res = jnp.dot(u_ref[...], w_ref[:, col0:col0 + ATTN_OUT_W],
                          preferred_element_type=_F32)
            if part == 0:
                res = res * _Q_SCALE
            if dil == 1:
                o_ref[:, part * ATTN_OUT_W:(part + 1) * ATTN_OUT_W] = res.astype(o_ref.dtype)
            else:
                for c in range(n_blocks):
                    stage_ref[slot, c] = res[:, _lane_block(c)]
            if pending is not None:
                fold(*pending)
                pending = None
            if dil > 1:
                pending = (o_ref, dil, part, slot)
                slot = 1 - slot
    if pending is not None:
        fold(*pending)


def _qkv_proj(x2d, g, w, B, S, *, tm):
    T, D = x2d.shape
    assert T == B * S and S % tm == 0
    tiles_per_seq = S // tm
    out_shape, out_specs = [], []
    for _, dil in DILATED_GROUPS:
        assert tm % (dil * _BF16_SUBLANES) == 0
        if dil == 1:
            out_shape.append(jax.ShapeDtypeStruct((T, _GROUP_QKV_W), _BF16))
            out_specs.append(pl.BlockSpec((tm, _GROUP_QKV_W), lambda i: (i, 0)))
        else:
            out_shape.append(jax.ShapeDtypeStruct((B, dil, S // dil, _GROUP_QKV_W), _BF16))
            out_specs.append(pl.BlockSpec(
                (None, dil, tm // dil, _GROUP_QKV_W),
                lambda i: (i // tiles_per_seq, 0, i % tiles_per_seq, 0)))
    return pl.pallas_call(
        _qkv_proj_kernel,
        grid=(T // tm,),
        in_specs=[pl.BlockSpec((tm, D), lambda i: (i, 0)),
                  pl.BlockSpec((1, D), lambda i: (0, 0)),
                  pl.BlockSpec(w.shape, lambda i: (0, 0), pipeline_mode=pl.Buffered(1))],
        out_specs=out_specs,
        out_shape=out_shape,
        scratch_shapes=[pltpu.VMEM((tm, D), _BF16),
                        pltpu.VMEM((2, ATTN_OUT_W // _LANES, tm, _LANES), _F32),
                        pltpu.VMEM((ATTN_OUT_W // _LANES, _FOLD_STRIDE, tm // _FOLD_STRIDE, _LANES),
                                   _F32)],
        compiler_params=_params(("parallel",)),
        name="qkv_proj",
    )(x2d, g, w)


def _norm_proj_kernel(x_ref, g_ref, w_ref, o_ref, u_ref, *, segments):
    u_ref[...] = _rms(x_ref[...], g_ref[...]).astype(u_ref.dtype)
    for src, dst, width in segments:
        o_ref[:, dst:dst + width] = jnp.dot(u_ref[...], w_ref[:, src:src + width],
                                            preferred_element_type=_F32).astype(o_ref.dtype)


def _norm_proj(x2d, g, w, *, tm, segments, out_dtype):
    T, D = x2d.shape
    N = w.shape[1]
    assert T % tm == 0 and sum(s[2] for s in segments) == N
    return pl.pallas_call(
        functools.partial(_norm_proj_kernel, segments=segments),
        grid=(T // tm,),
        in_specs=[pl.BlockSpec((tm, D), lambda i: (i, 0)),
                  pl.BlockSpec((1, D), lambda i: (0, 0)),
                  pl.BlockSpec(w.shape, lambda i: (0, 0), pipeline_mode=pl.Buffered(1))],
        out_specs=pl.BlockSpec((tm, N), lambda i: (i, 0)),
        out_shape=jax.ShapeDtypeStruct((T, N), out_dtype),
        scratch_shapes=[pltpu.VMEM((tm, D), _BF16)],
        compiler_params=_params(("parallel",)),
        name="norm_proj",
    )(x2d, g, w)


_ML_SHIFT = _LANES // 2
_ATTN_CHAINS_PER_TRIP = 64


def _attn_kernel(q_ref, k_ref, v_ref, b_ref, o_ref, ml_ref, kpad, vpad, *, L, dil, heads, unroll):
    W = heads * HEAD_DIM
    zeros = jnp.zeros((_SIDE, W), _BF16)
    for r in range(dil):
        for pad, src in ((kpad, k_ref), (vpad, v_ref)):
            pad[r, 0:_SIDE, :] = zeros
            pad[r, _SIDE + L:2 * _SIDE + L, :] = zeros
            pad[r, _SIDE:_SIDE + L, :] = src[r]

    lane = lax.broadcasted_iota(jnp.int32, (_QBLK, _LANES), 1)
    nblk = L // _QBLK
    assert nblk & (nblk - 1) == 0
    blk_shift = nblk.bit_length() - 1

    def body(n, carry):
        r = lax.shift_right_logical(n, blk_shift)
        blk = n & (nblk - 1)
        qs = pl.multiple_of(blk * _QBLK, _QBLK)
        edge = jnp.where(blk == 0, 1, 0) + jnp.where(blk == nblk - 1, 2, 0)
        ml_tile = jnp.zeros((_QBLK, _LANES), _F32)
        for h in range(heads):
            cols = _lane_block(h)
            q = q_ref[r, pl.ds(qs, _QBLK), cols]
            kw = kpad[r, pl.ds(qs, _KWIN), cols]
            vw = vpad[r, pl.ds(qs, _KWIN), cols]
            s = lax.dot_general(q, kw, (((1,), (1,)), ((), ())),
                                preferred_element_type=_F32) + b_ref[edge, h]
            m = jnp.max(s, axis=-1, keepdims=True)
            p = jnp.exp2(s - m)
            l = jnp.sum(p, axis=-1, keepdims=True)
            o_ref[r, pl.ds(qs, _QBLK), cols] = jnp.dot(p.astype(_BF16), vw,
                                                       preferred_element_type=_F32)
            ml_tile = jnp.where(lane == h, m, jnp.where(lane == _ML_SHIFT + h, l, ml_tile))
        ml_ref[r, pl.ds(qs, _QBLK), :] = ml_tile
        return carry

    lax.fori_loop(0, dil * nblk, body, 0, unroll=unroll)


def _attn_heads_per_step(S):
    return HEADS_PER_GROUP if S <= 2048 else HEADS_PER_GROUP // 2


def _attn_group(zg, bias, group, dil, B, S):
    L = S // dil
    assert L % _QBLK == 0
    heads = _attn_heads_per_step(S)
    steps = HEADS_PER_GROUP // heads
    W = heads * HEAD_DIM
    zg = zg.reshape(B, dil, L, _GROUP_QKV_W)
    col = lambda part: (lambda b, hp: (b, 0, 0, part * steps + hp))
    return pl.pallas_call(
        functools.partial(_attn_kernel, L=L, dil=dil, heads=heads,
                          unroll=min(_ATTN_CHAINS_PER_TRIP // heads, dil * L // _QBLK)),
        grid=(B, steps),
        in_specs=[pl.BlockSpec((None, dil, L, W), col(0)),
                  pl.BlockSpec((None, dil, L, W), col(1)),
                  pl.BlockSpec((None, dil, L, W), col(2)),
                  pl.BlockSpec((_EDGE_VARIANTS, heads, _QBLK, _KWIN),
                               lambda b, hp: (0, group * steps + hp, 0, 0))],
        out_specs=[pl.BlockSpec((None, dil, L, W), lambda b, hp: (b, 0, 0, hp)),
                   pl.BlockSpec((None, dil, L, _LANES), lambda b, hp: (b, 0, 0, hp))],
        out_shape=[jax.ShapeDtypeStruct((B, dil, L, ATTN_OUT_W), _F32),
                   jax.ShapeDtypeStruct((B, dil, L, steps * _LANES), _F32)],
        scratch_shapes=[pltpu.VMEM((dil, L + 2 * _SIDE, W), _BF16),
                        pltpu.VMEM((dil, L + 2 * _SIDE, W), _BF16)],
        compiler_params=_params(("parallel", "parallel")),
        name=f"dilated_attn_g{group}",
    )(zg, zg, zg, bias)


_CHUNKS = 4 * _SUBLANES
_PITCH_PAD = 4
_GATE_TAUS = 8
_SCAN_UNROLL = 16
_F32_TINY = float(np.finfo(np.float32).tiny)


def _sigmoid(x):
    return 0.5 * jnp.tanh(0.5 * x) + 0.5


def _gelu_tanh(x):
    k = math.sqrt(2.0 / math.pi)
    half = 0.5 * x
    return half * jnp.tanh(x * (k + (k * 0.044715) * (x * x))) + half


def _chunk_rows(ref, start, pitch):
    return jnp.concatenate(
        [ref[pl.ds(start + half * _SUBLANES * pitch, _SUBLANES, stride=pitch), :]
         for half in range(_CHUNKS // _SUBLANES)], axis=0)


def _rglru_kernel(rx_ref, ry_ref, cw_ref, cb_ref, wg_ref, bg_ref, lam_ref, o_ref,
                  xs, a_f, h_f, a_b, h_b, *, S):
    BW = LRU_BLOCK_W
    Lc = S // _CHUNKS
    P = Lc + _PITCH_PAD
    left = CONV_W // 2

    for c in range(_CHUNKS):
        lo = c * Lc - left
        hi = lo + P
        s_lo, s_hi = max(lo, 0), min(hi, S)
        xs[c * P + (s_lo - lo):c * P + (s_hi - lo), :] = rx_ref[s_lo:s_hi, :]
        if lo < 0:
            xs[c * P:c * P - lo, :] = jnp.zeros((-lo, BW), _F32)
        if hi > S:
            xs[c * P + (S - lo):(c + 1) * P, :] = jnp.zeros((hi - S, BW), _F32)

    cw = 0.5 * cw_ref[...]
    cb = 0.5 * cb_ref[...]
    half_decay = (-0.5 * LRU_C) * jax.nn.softplus(-lam_ref[...])
    TB = _GATE_TAUS

    def gate_body(bi, carry):
        t0 = bi * TB
        taps = [_chunk_rows(xs, t0 + j, P) for j in range(TB + CONV_W - 1)]
        rows = []
        for tau in range(TB):
            xc = cb
            for t in range(CONV_W):
                xc = xc + taps[tau + t] * cw[t:t + 1]
            rows.append(xc)
        half_x = jnp.concatenate(rows, axis=0)
        g = jnp.tanh(jnp.dot(half_x.astype(_BF16), wg_ref[...], preferred_element_type=_F32)
                     + bg_ref[...])
        r0 = pl.multiple_of(t0 * _CHUNKS, TB * _CHUNKS)
        for d, (a_ref, u_ref) in enumerate(((a_f, h_f), (a_b, h_b))):
            log_a = half_decay[d:d + 1] * g[:, (2 * d) * BW:(2 * d + 1) * BW] + half_decay[d:d + 1]
            ix = (g[:, (2 * d + 1) * BW:(2 * d + 2) * BW] + 1.0) * half_x
            a = jnp.exp(log_a)
            om = jnp.tanh(log_a) * (-1.0 - a * a)
            root = om * lax.rsqrt(jnp.maximum(om, _F32_TINY))
            a_ref[pl.ds(r0, TB * _CHUNKS), :] = a
            u_ref[pl.ds(r0, TB * _CHUNKS), :] = root * ix
        return carry

    lax.fori_loop(0, Lc // TB, gate_body, 0, unroll=4)

    def scan_body(tau, carry):
        hf, pf, hb, pb = carry
        rf = pl.multiple_of(tau * _CHUNKS, _CHUNKS)
        rb = pl.multiple_of((Lc - 1 - tau) * _CHUNKS, _CHUNKS)
        af = a_f[pl.ds(rf, _CHUNKS), :]
        hf = af * hf + h_f[pl.ds(rf, _CHUNKS), :]
        pf = af * pf
        h_f[pl.ds(rf, _CHUNKS), :] = hf
        a_f[pl.ds(rf, _CHUNKS), :] = pf
        ab = a_b[pl.ds(rb, _CHUNKS), :]
        hb = ab * hb + h_b[pl.ds(rb, _CHUNKS), :]
        pb = ab * pb
        h_b[pl.ds(rb, _CHUNKS), :] = hb
        a_b[pl.ds(rb, _CHUNKS), :] = pb
        return hf, pf, hb, pb

    zero = jnp.zeros((_CHUNKS, BW), _F32)
    one = jnp.ones((_CHUNKS, BW), _F32)
    h_end, p_end, h_beg, p_beg = lax.fori_loop(0, Lc, scan_body, (zero, one, zero, one),
                                               unroll=_SCAN_UNROLL)

    row = lax.broadcasted_iota(jnp.int32, (_CHUNKS, BW), 0)
    in_f, in_b = zero, zero
    prev_f = jnp.zeros((1, BW), _F32)
    prev_b = jnp.zeros((1, BW), _F32)
    for c in range(1, _CHUNKS):
        prev_f = h_end[c - 1:c] + p_end[c - 1:c] * prev_f
        in_f = jnp.where(row == c, prev_f, in_f)
        cb_ = _CHUNKS - 1 - c
        prev_b = h_beg[cb_ + 1:cb_ + 2] + p_beg[cb_ + 1:cb_ + 2] * prev_b
        in_b = jnp.where(row == cb_, prev_b, in_b)

    def fix_body(tau, carry):
        r = pl.multiple_of(tau * _CHUNKS, _CHUNKS)
        h = (h_f[pl.ds(r, _CHUNKS), :] + a_f[pl.ds(r, _CHUNKS), :] * in_f
             + h_b[pl.ds(r, _CHUNKS), :] + a_b[pl.ds(r, _CHUNKS), :] * in_b)
        for half in range(_CHUNKS // _SUBLANES):
            xs[pl.ds(tau + half * _SUBLANES * P, _SUBLANES, stride=P), :] = \
                h[half * _SUBLANES:(half + 1) * _SUBLANES]
        return carry

    lax.fori_loop(0, Lc, fix_body, 0, unroll=_SCAN_UNROLL)

    for c in range(_CHUNKS):
        t = slice(c * Lc, (c + 1) * Lc)
        o_ref[t, :] = (xs[c * P:c * P + Lc, :] * _gelu_tanh(ry_ref[t, :])).astype(o_ref.dtype)


def _rglru(zr, rx_col, ry_col, conv_w, conv_b, wg, bg, lam):
    B, S, _ = zr.shape
    BW = LRU_BLOCK_W
    Lc = S // _CHUNKS
    assert S % (_CHUNKS * _GATE_TAUS) == 0 and Lc % _SCAN_UNROLL == 0
    assert rx_col % BW == 0 and ry_col % BW == 0
    seq = pltpu.VMEM((S, BW), _F32)
    out = pl.pallas_call(
        functools.partial(_rglru_kernel, S=S),
        grid=(B, LRU_BLOCKS),
        in_specs=[pl.BlockSpec((None, S, BW), lambda b, n: (b, 0, rx_col // BW + n)),
                  pl.BlockSpec((None, S, BW), lambda b, n: (b, 0, ry_col // BW + n)),
                  pl.BlockSpec((CONV_W, BW), lambda b, n: (0, n)),
                  pl.BlockSpec((1, BW), lambda b, n: (0, n)),
                  pl.BlockSpec((None, BW, 4 * BW), lambda b, n: (n, 0, 0)),
                  pl.BlockSpec((None, 1, 4 * BW), lambda b, n: (n, 0, 0)),
                  pl.BlockSpec((2, BW), lambda b, n: (0, n))],
        out_specs=pl.BlockSpec((None, S, BW), lambda b, n: (b, 0, n)),
        out_shape=jax.ShapeDtypeStruct((B, S, LRU_W), _BF16),
        scratch_shapes=[pltpu.VMEM((_CHUNKS * (Lc + _PITCH_PAD), BW), _F32), seq, seq, seq, seq],
        compiler_params=_params(("parallel", "parallel")),
        name="conv_rglru",
    )(zr, zr, conv_w, conv_b, wg, bg, lam)
    return out.reshape(B * S, LRU_W)


def _unfold(src_ref, dst_ref, dil):
    rows = src_ref.shape[1]
    for r in range(dil):
        for c in range(dst_ref.shape[0]):
            dst_ref[c, pl.ds(r, rows, stride=dil), :] = src_ref[r, :, _lane_block(c)]


def _merge_kernel(*refs, heads):
    o_refs = refs[0:N_GROUPS]
    l_refs = refs[N_GROUPS:2 * N_GROUPS]
    rnn_ref, ga_ref, gr_ref, x_ref, wa_ref, wr_ref, wo_ref, out_ref = refs[2 * N_GROUPS:2 * N_GROUPS + 8]
    scratch = refs[2 * N_GROUPS + 8:]

    pr = jnp.dot(rnn_ref[...], wr_ref[...], preferred_element_type=_F32)

    o_get, l_get, k = [], [], 0
    for g, (_, dil) in enumerate(DILATED_GROUPS):
        if dil == 1:
            o_get.append(lambda c, ref=o_refs[g]: ref[:, _lane_block(c)])
            l_get.append(lambda c, ref=l_refs[g]: ref[:, _lane_block(c)])
        else:
            o_scr, l_scr = scratch[k], scratch[k + 1]
            k += 2
            _unfold(o_refs[g], o_scr, dil)
            _unfold(l_refs[g], l_scr, dil)
            o_get.append(lambda c, ref=o_scr: ref[c])
            l_get.append(lambda c, ref=l_scr: ref[c])

    wts = []
    for step in range(HEADS_PER_GROUP // heads):
        ml = [get(step) for get in l_get]
        mx = functools.reduce(jnp.maximum, ml)
        es = [jnp.exp2(t - mx) for t in ml]
        den = functools.reduce(jnp.add, [e * pltpu.roll(t, _ML_SHIFT, axis=1)
                                         for e, t in zip(es, ml)])
        inv = 1.0 / den
        wts.append([e * inv for e in es])
    attn = []
    for h in range(HEADS_PER_GROUP):
        w = wts[h // heads]
        c = h % heads
        attn.append(functools.reduce(
            jnp.add, [w[g][:, c:c + 1] * o_get[g](h) for g in range(N_GROUPS)]))
    attn = jnp.concatenate(attn, axis=-1).astype(_BF16)
    pa = jnp.dot(attn, wa_ref[...], preferred_element_type=_F32)
    merged = _sigmoid(ga_ref[...]) * pa + _sigmoid(gr_ref[...]) * pr
    out_ref[...] = x_ref[...] + jnp.dot(merged.astype(_BF16), wo_ref[...],
                                        preferred_element_type=_F32)


def _merge(os_, lses, rnn, zr, ga_col, gr_col, x2d, wa, wr, wo, B, S, *, tm):
    T, D = x2d.shape
    assert S % tm == 0 and ga_col % D == 0 and gr_col % D == 0
    heads = _attn_heads_per_step(S)
    LW = (HEADS_PER_GROUP // heads) * _LANES
    tiles_per_seq = S // tm
    row = lambda w: pl.BlockSpec((tm, w), lambda i: (i, 0))
    const = lambda a: pl.BlockSpec(a.shape, lambda i: (0, 0), pipeline_mode=pl.Buffered(1))

    def group_spec(dil, w):
        if dil == 1:
            return row(w)
        assert tm % (dil * _SUBLANES) == 0
        return pl.BlockSpec((None, dil, tm // dil, w),
                            lambda i: (i // tiles_per_seq, 0, i % tiles_per_seq, 0))

    scratch = []
    for _, dil in DILATED_GROUPS:
        if dil > 1:
            scratch += [pltpu.VMEM((ATTN_OUT_W // _LANES, tm, _LANES), _F32),
                        pltpu.VMEM((LW // _LANES, tm, _LANES), _F32)]
    os_ = [o.reshape(T, ATTN_OUT_W) if dil == 1 else o for o, (_, dil) in zip(os_, DILATED_GROUPS)]
    lses = [l.reshape(T, LW) if dil == 1 else l for l, (_, dil) in zip(lses, DILATED_GROUPS)]
    return pl.pallas_call(
        functools.partial(_merge_kernel, heads=heads),
        grid=(T // tm,),
        in_specs=[group_spec(dil, ATTN_OUT_W) for _, dil in DILATED_GROUPS]
        + [group_spec(dil, LW) for _, dil in DILATED_GROUPS]
        + [row(LRU_W),
           pl.BlockSpec((tm, D), lambda i: (i, ga_col // D)),
           pl.BlockSpec((tm, D), lambda i: (i, gr_col // D)),
           row(D), const(wa), const(wr), const(wo)],
        out_specs=row(D),
        out_shape=jax.ShapeDtypeStruct((T, D), _F32),
        scratch_shapes=scratch,
        compiler_params=_params(("parallel",)),
        name="merge_proj",
    )(*os_, *lses, rnn, zr, zr, x2d, wa, wr, wo)


def _mlp_kernel(x_ref, g_ref, w1_ref, w2_ref, gf_ref, o_ref, u_ref):
    j = pl.program_id(1)

    def ff_chunk():
        h = jnp.dot(u_ref[...], w1_ref[...], preferred_element_type=_F32)
        h = jnp.square(jnp.maximum(h, 0.0)).astype(_BF16)
        return jnp.dot(h, w2_ref[...], preferred_element_type=_F32)

    @pl.when(j == 0)
    def _():
        x = x_ref[...]
        u_ref[...] = _rms(x, g_ref[...]).astype(u_ref.dtype)
        o_ref[...] = x + ff_chunk()

    @pl.when(j > 0)
    def _():
        o_ref[...] += ff_chunk()

    @pl.when(j == pl.num_programs(1) - 1)
    def _():
        o_ref[...] = _rms(o_ref[...], gf_ref[...])


def _mlp(x2d, g, w1, w2, gf, *, tm, tf):
    T, D = x2d.shape
    F = w1.shape[1]
    assert T % tm == 0 and F % tf == 0
    return pl.pallas_call(
        _mlp_kernel,
        grid=(T // tm, F // tf),
        in_specs=[pl.BlockSpec((tm, D), lambda i, j: (i, 0)),
                  pl.BlockSpec((1, D), lambda i, j: (0, 0)),
                  pl.BlockSpec((D, tf), lambda i, j: (0, j)),
                  pl.BlockSpec((tf, D), lambda i, j: (j, 0)),
                  pl.BlockSpec((1, D), lambda i, j: (0, 0))],
        out_specs=pl.BlockSpec((tm, D), lambda i, j: (i, 0)),
        out_shape=jax.ShapeDtypeStruct((T, D), _F32),
        scratch_shapes=[pltpu.VMEM((tm, D), _BF16)],
        compiler_params=_params(("parallel", "arbitrary")),
        name="mlp_final_norm",
    )(x2d, g, w1, w2, gf)


_REST_GA = 0
_REST_GR = D_MODEL
_REST_RX = 2 * D_MODEL
_REST_RY = 2 * D_MODEL + LRU_W
_REST_SEGMENTS = ((2 * LRU_W, _REST_GA, D_MODEL), (2 * LRU_W + D_MODEL, _REST_GR, D_MODEL),
                  (0, _REST_RX, LRU_W), (LRU_W, _REST_RY, LRU_W))


def _layer(x, bias_tabs, p):
    B, S, D = x.shape
    T = B * S
    x2d = x.reshape(T, D)
    zgs = _qkv_proj(x2d, p["g_mix"], p["w_qkv"], B, S, tm=_QKV_TM)
    zr = _norm_proj(x2d, p["g_mix"], p["w_rest"], tm=_REST_TM, segments=_REST_SEGMENTS,
                    out_dtype=_F32)
    os_, lses = [], []
    for g, (_, dil) in enumerate(DILATED_GROUPS):
        o, lse = _attn_group(zgs[g], bias_tabs, g, dil, B, S)
        os_.append(o)
        lses.append(lse)
    rnn = _rglru(zr.reshape(B, S, -1), _REST_RX, _REST_RY, p["conv_w"], p["conv_b"],
                 p["w_gate"], p["b_gate"], p["lam"])
    x2 = _merge(os_, lses, rnn, zr, _REST_GA, _REST_GR, x2d,
                p["w_attn_o"], p["w_rnn_o"], p["w_out"], B, S, tm=_MERGE_TM)
    y = _mlp(x2, p["g_mlp"], p["w_mlp_in"], p["w_mlp_out"], p["g_final"], tm=_MLP_TM, tf=_MLP_TF)
    return y.reshape(B, S, D)


def kernel(x_prompt, x_sample, rel_bias, norm_mix_g, w_in, conv_w, conv_b, lru_wa, lru_ba,
           lru_wx, lru_bx, lru_lambda, w_attn_o, w_rnn_o, w_out, norm_mlp_g, w_mlp_in,
           w_mlp_out, norm_final_g):
    depth = w_in.shape[0]
    assert depth == 1, "the final norm is fused into the single layer's MLP kernel"
    bias_tabs = _bias_tables(rel_bias)
    w_qkv, w_rest = _cast_split(w_in[0], ((0, 3 * ATTN_W), (3 * ATTN_W, w_in.shape[2])),
                                tm=_CAST_TM)
    w_gate = jnp.concatenate([lru_wa[0, 0], lru_wx[0, 0], lru_wa[0, 1], lru_wx[0, 1]],
                             axis=-1).astype(_BF16)
    b_gate = jnp.concatenate(
        [b.reshape(LRU_BLOCKS, 1, LRU_BLOCK_W)
         for b in (lru_ba[0, 0], lru_bx[0, 0], lru_ba[0, 1], lru_bx[0, 1])], axis=-1)
    p = dict(
        g_mix=norm_mix_g[0].reshape(1, D_MODEL),
        w_qkv=w_qkv, w_rest=w_rest,
        conv_w=conv_w[0], conv_b=conv_b[0].reshape(1, LRU_W),
        w_gate=w_gate, b_gate=0.5 * b_gate.astype(_F32), lam=lru_lambda[0],
        w_attn_o=w_attn_o[0].astype(_BF16), w_rnn_o=w_rnn_o[0].astype(_BF16),
        w_out=w_out[0].astype(_BF16),
        g_mlp=norm_mlp_g[0].reshape(1, D_MODEL),
        w_mlp_in=w_mlp_in[0].astype(_BF16), w_mlp_out=w_mlp_out[0].astype(_BF16),
        g_final=norm_final_g.reshape(1, D_MODEL),
    )
    return (_layer(x_prompt, bias_tabs, p), _layer(x_sample, bias_tabs, p))
```

```python
import functools
import math

import jax
import jax.numpy as jnp
import numpy as np
from jax import lax
from jax.experimental import pallas as pl
from jax.experimental.pallas import tpu as pltpu

D_MODEL = 2048
HEAD_DIM = 128
HEADS_PER_GROUP = 4
DILATED_GROUPS = ((128, 1), (512, 4), (2048, 16))
N_GROUPS = len(DILATED_GROUPS)
ATTN_W = N_GROUPS * HEADS_PER_GROUP * HEAD_DIM
ATTN_OUT_W = HEADS_PER_GROUP * HEAD_DIM
NUM_BUCKETS = 32
REL_MAX_DIST = 1024
LRU_W = 1536
LRU_BLOCKS = 12
LRU_BLOCK_W = LRU_W // LRU_BLOCKS
LRU_C = 8.0
CONV_W = 4
D_FF = 4 * D_MODEL
NORM_EPS = 1e-6
NEG_INF = -1e30

_F32 = jnp.float32
_BF16 = jnp.bfloat16
_V7X_VMEM_LIMIT_BYTES = 56 * 1024 * 1024
_LANES = 128
_SUBLANES = 8
_BF16_SUBLANES = 2 * _SUBLANES

_QKV_TM = 512
_REST_TM = 256
_MERGE_TM = 256
_GATE_RING_SLOTS = 3
_MLP_TM = 512
_MLP_TF = 2048
_CAST_TM = 256

_SIDE = DILATED_GROUPS[0][0] // (2 * DILATED_GROUPS[0][1])
assert all(w // (2 * d) == _SIDE for w, d in DILATED_GROUPS)
_QBLK = 2 * _SIDE
_KWIN = 4 * _SIDE
assert _QBLK == _LANES
_GROUP_QKV_W = 3 * ATTN_OUT_W


def _params(semantics):
    return pltpu.CompilerParams(dimension_semantics=semantics,
                                vmem_limit_bytes=_V7X_VMEM_LIMIT_BYTES)


def _rms(x, g):
    ms = jnp.mean(x * x, axis=-1, keepdims=True)
    return x * lax.rsqrt(ms + NORM_EPS) * g


def _lane_block(c):
    return slice(c * _LANES, (c + 1) * _LANES)


_LOG2E = math.log2(math.e)
_EDGE_VARIANTS = 4


def _rel_bucket_np(rel):
    half = NUM_BUCKETS // 2
    max_exact = half // 2
    n = np.abs(rel)
    nf = np.maximum(n, 1).astype(np.float32)
    ratio = np.log(nf / np.float32(max_exact)) / np.float32(math.log(REL_MAX_DIST / max_exact))
    large = max_exact + (ratio * np.float32(half - max_exact)).astype(np.int32)
    large = np.minimum(large, half - 1)
    return (np.where(rel > 0, half, 0) + np.where(n < max_exact, n, large)).astype(np.int32)


def _bias_kernel(tab_ref, idx_ref, o_ref):
    head = pl.program_id(0)
    idx = idx_ref[...]
    acc = jnp.zeros(idx.shape, _F32)
    for b in range(NUM_BUCKETS):
        acc = jnp.where(idx == b, tab_ref[b, head], acc)
    acc = acc * _LOG2E
    kj = lax.broadcasted_iota(jnp.int32, idx.shape, 1)
    in_band = idx >= 0
    after_start = kj >= _SIDE
    before_end = kj < _KWIN - _SIDE
    o_ref[0] = jnp.where(in_band, acc, NEG_INF)
    o_ref[1] = jnp.where(jnp.logical_and(in_band, after_start), acc, NEG_INF)
    o_ref[2] = jnp.where(jnp.logical_and(in_band, before_end), acc, NEG_INF)
    o_ref[3] = jnp.where(jnp.logical_and(in_band, jnp.logical_and(after_start, before_end)),
                         acc, NEG_INF)


def _bias_tables(rel_bias):
    qi = np.arange(_QBLK, dtype=np.int32)[:, None]
    kj = np.arange(_KWIN, dtype=np.int32)[None, :]
    rel = kj - _SIDE - qi
    band = np.abs(rel) <= _SIDE
    idx = np.stack([np.where(band, _rel_bucket_np(rel * dil), -1) for _, dil in DILATED_GROUPS])
    n_heads = N_GROUPS * HEADS_PER_GROUP
    return pl.pallas_call(
        _bias_kernel,
        grid=(n_heads,),
        in_specs=[pl.BlockSpec(memory_space=pltpu.SMEM),
                  pl.BlockSpec((None, _QBLK, _KWIN), lambda h: (h // HEADS_PER_GROUP, 0, 0))],
        out_specs=pl.BlockSpec((_EDGE_VARIANTS, None, _QBLK, _KWIN), lambda h: (0, h, 0, 0)),
        out_shape=jax.ShapeDtypeStruct((_EDGE_VARIANTS, n_heads, _QBLK, _KWIN), _F32),
        name="bias_tables",
    )(rel_bias.astype(_F32), jnp.asarray(idx.astype(np.int32)))


def _cast_split_kernel(w_ref, *o_refs, bounds):
    for o_ref, (start, stop) in zip(o_refs, bounds):
        o_ref[...] = w_ref[:, start:stop].astype(o_ref.dtype)


def _cast_split(w, bounds, *, tm):
    R, C = w.shape
    assert R % tm == 0
    return pl.pallas_call(
        functools.partial(_cast_split_kernel, bounds=bounds),
        grid=(R // tm,),
        in_specs=[pl.BlockSpec((tm, C), lambda i: (i, 0))],
        out_specs=[pl.BlockSpec((tm, stop - start), lambda i: (i, 0)) for start, stop in bounds],
        out_shape=[jax.ShapeDtypeStruct((R, stop - start), _BF16) for start, stop in bounds],
        compiler_params=_params(("parallel",)),
        name="cast_split",
    )(w)


_FOLD_STRIDE = 4
_Q_SCALE = HEAD_DIM ** -0.5 * _LOG2E


def _qkv_proj_kernel(x_ref, g_ref, w_ref, *refs):
    out_refs = refs[:N_GROUPS]
    u_ref, stage_ref, stage2_ref = refs[N_GROUPS:]
    tm = x_ref.shape[0]
    n_blocks = ATTN_OUT_W // _LANES
    u_ref[...] = _rms(x_ref[...], g_ref[...]).astype(u_ref.dtype)

    def fold(o_ref, dil, part, slot):
        for c in range(n_blocks):
            lanes = _lane_block(part * n_blocks + c)
            if dil == _FOLD_STRIDE:
                for r in range(dil):
                    o_ref[r, :, lanes] = stage_ref[
                        slot, c, pl.ds(r, tm // dil, stride=dil), :].astype(o_ref.dtype)
                continue
            assert dil == _FOLD_STRIDE ** 2
            for r0 in range(_FOLD_STRIDE):
                stage2_ref[c, r0] = stage_ref[
                    slot, c, pl.ds(r0, tm // _FOLD_STRIDE, stride=_FOLD_STRIDE), :]
            for r0 in range(_FOLD_STRIDE):
                for r1 in range(_FOLD_STRIDE):
                    o_ref[r0 + _FOLD_STRIDE * r1, :, lanes] = stage2_ref[
                        c, r0, pl.ds(r1, tm // dil, stride=_FOLD_STRIDE), :].astype(o_ref.dtype)

    pending, slot = None, 0
    for g, (_, dil) in sorted(enumerate(DILATED_GROUPS), key=lambda e: -e[1][1]):
        o_ref = out_refs[g]
        for part in range(3):
            col0 = part * ATTN_W + g * ATTN_OUT_W
            res = jnp.dot(u_ref[...], w_ref[:, col0:col0 + ATTN_OUT_W],
                          preferred_element_type=_F32)
            if part == 0:
                res = res * _Q_SCALE
            if dil == 1:
                o_ref[:, part * ATTN_OUT_W:(part + 1) * ATTN_OUT_W] = res.astype(o_ref.dtype)
            else:
                for c in range(n_blocks):
                    stage_ref[slot, c] = res[:, _lane_block(c)]
            if pending is not None:
                fold(*pending)
                pending = None
            if dil > 1:
                pending = (o_ref, dil, part, slot)
                slot = 1 - slot
    if pending is not None:
        fold(*pending)


def _qkv_proj(x2d, g, w, B, S, *, tm):
    T, D = x2d.shape
    assert T == B * S and S % tm == 0
    tiles_per_seq = S // tm
    out_shape, out_specs = [], []
    for _, dil in DILATED_GROUPS:
        assert tm % (dil * _BF16_SUBLANES) == 0
        if dil == 1:
            out_shape.append(jax.ShapeDtypeStruct((T, _GROUP_QKV_W), _BF16))
            out_specs.append(pl.BlockSpec((tm, _GROUP_QKV_W), lambda i: (i, 0)))
        else:
            out_shape.append(jax.ShapeDtypeStruct((B, dil, S // dil, _GROUP_QKV_W), _BF16))
            out_specs.append(pl.BlockSpec(
                (None, dil, tm // dil, _GROUP_QKV_W),
                lambda i: (i // tiles_per_seq, 0, i % tiles_per_seq, 0)))
    return pl.pallas_call(
        _qkv_proj_kernel,
        grid=(T // tm,),
        in_specs=[pl.BlockSpec((tm, D), lambda i: (i, 0)),
                  pl.BlockSpec((1, D), lambda i: (0, 0)),
                  pl.BlockSpec(w.shape, lambda i: (0, 0), pipeline_mode=pl.Buffered(1))],
        out_specs=out_specs,
        out_shape=out_shape,
        scratch_shapes=[pltpu.VMEM((tm, D), _BF16),
                        pltpu.VMEM((2, ATTN_OUT_W // _LANES, tm, _LANES), _F32),
                        pltpu.VMEM((ATTN_OUT_W // _LANES, _FOLD_STRIDE, tm // _FOLD_STRIDE, _LANES),
                                   _F32)],
        compiler_params=_params(("parallel",)),
        name="qkv_proj",
    )(x2d, g, w)


def _norm_proj_kernel(x_ref, g_ref, w_ref, o_ref, u_ref, *, segments):
    u_ref[...] = _rms(x_ref[...], g_ref[...]).astype(u_ref.dtype)
    for src, dst, width in segments:
        o_ref[:, dst:dst + width] = jnp.dot(u_ref[...], w_ref[:, src:src + width],
                                            preferred_element_type=_F32).astype(o_ref.dtype)


def _norm_proj(x2d, g, w, *, tm, segments, out_dtype):
    T, D = x2d.shape
    N = w.shape[1]
    assert T % tm == 0 and sum(s[2] for s in segments) == N
    return pl.pallas_call(
        functools.partial(_norm_proj_kernel, segments=segments),
        grid=(T // tm,),
        in_specs=[pl.BlockSpec((tm, D), lambda i: (i, 0)),
                  pl.BlockSpec((1, D), lambda i: (0, 0)),
                  pl.BlockSpec(w.shape, lambda i: (0, 0), pipeline_mode=pl.Buffered(1))],
        out_specs=pl.BlockSpec((tm, N), lambda i: (i, 0)),
        out_shape=jax.ShapeDtypeStruct((T, N), out_dtype),
        scratch_shapes=[pltpu.VMEM((tm, D), _BF16)],
        compiler_params=_params(("parallel",)),
        name="norm_proj",
    )(x2d, g, w)


_ML_SHIFT = _LANES // 2
_ATTN_CHAINS_PER_TRIP = 64


def _attn_kernel(q_ref, k_ref, v_ref, b_ref, o_ref, ml_ref, kpad, vpad, *, L, dil, heads, unroll):
    W = heads * HEAD_DIM
    zeros = jnp.zeros((_SIDE, W), _BF16)
    for r in range(dil):
        for pad, src in ((kpad, k_ref), (vpad, v_ref)):
            pad[r, 0:_SIDE, :] = zeros
            pad[r, _SIDE + L:2 * _SIDE + L, :] = zeros
            pad[r, _SIDE:_SIDE + L, :] = src[r]

    lane = lax.broadcasted_iota(jnp.int32, (_QBLK, _LANES), 1)
    nblk = L // _QBLK
    assert nblk & (nblk - 1) == 0
    blk_shift = nblk.bit_length() - 1

    def body(n, carry):
        r = lax.shift_right_logical(n, blk_shift)
        blk = n & (nblk - 1)
        qs = pl.multiple_of(blk * _QBLK, _QBLK)
        edge = jnp.where(blk == 0, 1, 0) + jnp.where(blk == nblk - 1, 2, 0)
        ml_tile = jnp.zeros((_QBLK, _LANES), _F32)
        for h in range(heads):
            cols = _lane_block(h)
            q = q_ref[r, pl.ds(qs, _QBLK), cols]
            kw = kpad[r, pl.ds(qs, _KWIN), cols]
            vw = vpad[r, pl.ds(qs, _KWIN), cols]
            s = lax.dot_general(q, kw, (((1,), (1,)), ((), ())),
                                preferred_element_type=_F32) + b_ref[edge, h]
            m = jnp.max(s, axis=-1, keepdims=True)
            p = jnp.exp2(s - m)
            l = jnp.sum(p, axis=-1, keepdims=True)
            o_ref[r, pl.ds(qs, _QBLK), cols] = jnp.dot(p.astype(_BF16), vw,
                                                       preferred_element_type=_F32)
            ml_tile = jnp.where(lane == h, m, jnp.where(lane == _ML_SHIFT + h, l, ml_tile))
        ml_ref[r, pl.ds(qs, _QBLK), :] = ml_tile
        return carry

    lax.fori_loop(0, dil * nblk, body, 0, unroll=unroll)


def _attn_heads_per_step(S):
    return HEADS_PER_GROUP if S <= 2048 else HEADS_PER_GROUP // 2


def _attn_group(zg, bias, group, dil, B, S):
    L = S // dil
    assert L % _QBLK == 0
    heads = _attn_heads_per_step(S)
    steps = HEADS_PER_GROUP // heads
    W = heads * HEAD_DIM
    zg = zg.reshape(B, dil, L, _GROUP_QKV_W)
    col = lambda part: (lambda b, hp: (b, 0, 0, part * steps + hp))
    return pl.pallas_call(
        functools.partial(_attn_kernel, L=L, dil=dil, heads=heads,
                          unroll=min(_ATTN_CHAINS_PER_TRIP // heads, dil * L // _QBLK)),
        grid=(B, steps),
        in_specs=[pl.BlockSpec((None, dil, L, W), col(0)),
                  pl.BlockSpec((None, dil, L, W), col(1)),
                  pl.BlockSpec((None, dil, L, W), col(2)),
                  pl.BlockSpec((_EDGE_VARIANTS, heads, _QBLK, _KWIN),
                               lambda b, hp: (0, group * steps + hp, 0, 0))],
        out_specs=[pl.BlockSpec((None, dil, L, W), lambda b, hp: (b, 0, 0, hp)),
                   pl.BlockSpec((None, dil, L, _LANES), lambda b, hp: (b, 0, 0, hp))],
        out_shape=[jax.ShapeDtypeStruct((B, dil, L, ATTN_OUT_W), _F32),
                   jax.ShapeDtypeStruct((B, dil, L, steps * _LANES), _F32)],
        scratch_shapes=[pltpu.VMEM((dil, L + 2 * _SIDE, W), _BF16),
                        pltpu.VMEM((dil, L + 2 * _SIDE, W), _BF16)],
        compiler_params=_params(("parallel", "parallel")),
        name=f"dilated_attn_g{group}",
    )(zg, zg, zg, bias)


_CHUNKS = 4 * _SUBLANES
_PITCH_PAD = 4
_GATE_TAUS = 8
_SCAN_UNROLL = 16
_F32_TINY = float(np.finfo(np.float32).tiny)


def _sigmoid(x):
    return 0.5 * jnp.tanh(0.5 * x) + 0.5


def _gelu_tanh(x):
    k = math.sqrt(2.0 / math.pi)
    half = 0.5 * x
    return half * jnp.tanh(x * (k + (k * 0.044715) * (x * x))) + half


def _chunk_rows(ref, start, pitch):
    return jnp.concatenate(
        [ref[pl.ds(start + half * _SUBLANES * pitch, _SUBLANES, stride=pitch), :]
         for half in range(_CHUNKS // _SUBLANES)], axis=0)


def _rglru_kernel(rx_ref, ry_ref, cw_ref, cb_ref, wg_ref, bg_ref, lam_ref, o_ref,
                  xs, a_f, h_f, a_b, h_b, *, S):
    BW = LRU_BLOCK_W
    Lc = S // _CHUNKS
    P = Lc + _PITCH_PAD
    left = CONV_W // 2

    for c in range(_CHUNKS):
        lo = c * Lc - left
        hi = lo + P
        s_lo, s_hi = max(lo, 0), min(hi, S)
        xs[c * P + (s_lo - lo):c * P + (s_hi - lo), :] = rx_ref[s_lo:s_hi, :]
        if lo < 0:
            xs[c * P:c * P - lo, :] = jnp.zeros((-lo, BW), _F32)
        if hi > S:
            xs[c * P + (S - lo):(c + 1) * P, :] = jnp.zeros((hi - S, BW), _F32)

    cw = 0.5 * cw_ref[...]
    cb = 0.5 * cb_ref[...]
    half_decay = (-0.5 * LRU_C) * jax.nn.softplus(-lam_ref[...])
    TB = _GATE_TAUS

    def gate_body(bi, carry):
        t0 = bi * TB
        taps = [_chunk_rows(xs, t0 + j, P) for j in range(TB + CONV_W - 1)]
        rows = []
        for tau in range(TB):
            xc = cb
            for t in range(CONV_W):
                xc = xc + taps[tau + t] * cw[t:t + 1]
            rows.append(xc)
        half_x = jnp.concatenate(rows, axis=0)
        g = jnp.tanh(jnp.dot(half_x.astype(_BF16), wg_ref[...], preferred_element_type=_F32)
                     + bg_ref[...])
        r0 = pl.multiple_of(t0 * _CHUNKS, TB * _CHUNKS)
        for d, (a_ref, u_ref) in enumerate(((a_f, h_f), (a_b, h_b))):
            log_a = half_decay[d:d + 1] * g[:, (2 * d) * BW:(2 * d + 1) * BW] + half_decay[d:d + 1]
            ix = (g[:, (2 * d + 1) * BW:(2 * d + 2) * BW] + 1.0) * half_x
            a = jnp.exp(log_a)
            om = jnp.tanh(log_a) * (-1.0 - a * a)
            root = om * lax.rsqrt(jnp.maximum(om, _F32_TINY))
            a_ref[pl.ds(r0, TB * _CHUNKS), :] = a
            u_ref[pl.ds(r0, TB * _CHUNKS), :] = root * ix
        return carry

    lax.fori_loop(0, Lc // TB, gate_body, 0, unroll=4)

    def scan_body(tau, carry):
        hf, pf, hb, pb = carry
        rf = pl.multiple_of(tau * _CHUNKS, _CHUNKS)
        rb = pl.multiple_of((Lc - 1 - tau) * _CHUNKS, _CHUNKS)
        af = a_f[pl.ds(rf, _CHUNKS), :]
        hf = af * hf + h_f[pl.ds(rf, _CHUNKS), :]
        pf = af * pf
        h_f[pl.ds(rf, _CHUNKS), :] = hf
        a_f[pl.ds(rf, _CHUNKS), :] = pf
        ab = a_b[pl.ds(rb, _CHUNKS), :]
        hb = ab * hb + h_b[pl.ds(rb, _CHUNKS), :]
        pb = ab * pb
        h_b[pl.ds(rb, _CHUNKS), :] = hb
        a_b[pl.ds(rb, _CHUNKS), :] = pb
        return hf, pf, hb, pb

    zero = jnp.zeros((_CHUNKS, BW), _F32)
    one = jnp.ones((_CHUNKS, BW), _F32)
    h_end, p_end, h_beg, p_beg = lax.fori_loop(0, Lc, scan_body, (zero, one, zero, one),
                                               unroll=_SCAN_UNROLL)

    row = lax.broadcasted_iota(jnp.int32, (_CHUNKS, BW), 0)
    in_f, in_b = zero, zero
    prev_f = jnp.zeros((1, BW), _F32)
    prev_b = jnp.zeros((1, BW), _F32)
    for c in range(1, _CHUNKS):
        prev_f = h_end[c - 1:c] + p_end[c - 1:c] * prev_f
        in_f = jnp.where(row == c, prev_f, in_f)
        cb_ = _CHUNKS - 1 - c
        prev_b = h_beg[cb_ + 1:cb_ + 2] + p_beg[cb_ + 1:cb_ + 2] * prev_b
        in_b = jnp.where(row == cb_, prev_b, in_b)

    def fix_body(tau, carry):
        r = pl.multiple_of(tau * _CHUNKS, _CHUNKS)
        h = (h_f[pl.ds(r, _CHUNKS), :] + a_f[pl.ds(r, _CHUNKS), :] * in_f
             + h_b[pl.ds(r, _CHUNKS), :] + a_b[pl.ds(r, _CHUNKS), :] * in_b)
        for half in range(_CHUNKS // _SUBLANES):
            xs[pl.ds(tau + half * _SUBLANES * P, _SUBLANES, stride=P), :] = \
                h[half * _SUBLANES:(half + 1) * _SUBLANES]
        return carry

    lax.fori_loop(0, Lc, fix_body, 0, unroll=_SCAN_UNROLL)

    for c in range(_CHUNKS):
        t = slice(c * Lc, (c + 1) * Lc)
        o_ref[t, :] = (xs[c * P:c * P + Lc, :] * _gelu_tanh(ry_ref[t, :])).astype(o_ref.dtype)


def _rglru(zr, rx_col, ry_col, conv_w, conv_b, wg, bg, lam):
    B, S, _ = zr.shape
    BW = LRU_BLOCK_W
    Lc = S // _CHUNKS
    assert S % (_CHUNKS * _GATE_TAUS) == 0 and Lc % _SCAN_UNROLL == 0
    assert rx_col % BW == 0 and ry_col % BW == 0
    seq = pltpu.VMEM((S, BW), _F32)
    out = pl.pallas_call(
        functools.partial(_rglru_kernel, S=S),
        grid=(B, LRU_BLOCKS),
        in_specs=[pl.BlockSpec((None, S, BW), lambda b, n: (b, 0, rx_col // BW + n)),
                  pl.BlockSpec((None, S, BW), lambda b, n: (b, 0, ry_col // BW + n)),
                  pl.BlockSpec((CONV_W, BW), lambda b, n: (0, n)),
                  pl.BlockSpec((1, BW), lambda b, n: (0, n)),
                  pl.BlockSpec((None, BW, 4 * BW), lambda b, n: (n, 0, 0)),
                  pl.BlockSpec((None, 1, 4 * BW), lambda b, n: (n, 0, 0)),
                  pl.BlockSpec((2, BW), lambda b, n: (0, n))],
        out_specs=pl.BlockSpec((None, S, BW), lambda b, n: (b, 0, n)),
        out_shape=jax.ShapeDtypeStruct((B, S, LRU_W), _BF16),
        scratch_shapes=[pltpu.VMEM((_CHUNKS * (Lc + _PITCH_PAD), BW), _F32), seq, seq, seq, seq],
        compiler_params=_params(("parallel", "parallel")),
        name="conv_rglru",
    )(zr, zr, conv_w, conv_b, wg, bg, lam)
    return out.reshape(B * S, LRU_W)


def _unfold(src_ref, dst_ref, dil):
    rows = src_ref.shape[1]
    for r in range(dil):
        for c in range(dst_ref.shape[0]):
            dst_ref[c, pl.ds(r, rows, stride=dil), :] = src_ref[r, :, _lane_block(c)]


def _merge_kernel(*refs, heads, n_steps, gate_col):
    o_refs = refs[0:N_GROUPS]
    l_refs = refs[N_GROUPS:2 * N_GROUPS]
    rnn_ref, zr_hbm, x_ref, wa_ref, wr_ref, wo_ref, out_ref = refs[2 * N_GROUPS:2 * N_GROUPS + 7]
    gate_ring, gate_sem = refs[2 * N_GROUPS + 7:2 * N_GROUPS + 9]
    scratch = refs[2 * N_GROUPS + 9:]
    tm, D = x_ref.shape
    i = pl.program_id(0)

    def gate_copy(step, slot):
        return pltpu.make_async_copy(
            zr_hbm.at[pl.ds(step * tm, tm), pl.ds(gate_col, 2 * D)],
            gate_ring.at[slot], gate_sem.at[slot])

    @pl.when(i == 0)
    def _():
        for k in range(min(_GATE_RING_SLOTS - 1, n_steps)):
            gate_copy(k, k).start()

    ahead = i + (_GATE_RING_SLOTS - 1)

    @pl.when(ahead < n_steps)
    def _():
        gate_copy(ahead, lax.rem(ahead, _GATE_RING_SLOTS)).start()

    slot = lax.rem(i, _GATE_RING_SLOTS)
    gate_copy(i, slot).wait()

    pr = jnp.dot(rnn_ref[...], wr_ref[...], preferred_element_type=_F32)

    o_get, l_get, k = [], [], 0
    for g, (_, dil) in enumerate(DILATED_GROUPS):
        if dil == 1:
            o_get.append(lambda c, ref=o_refs[g]: ref[:, _lane_block(c)])
            l_get.append(lambda c, ref=l_refs[g]: ref[:, _lane_block(c)])
        else:
            o_scr, l_scr = scratch[k], scratch[k + 1]
            k += 2
            _unfold(o_refs[g], o_scr, dil)
            _unfold(l_refs[g], l_scr, dil)
            o_get.append(lambda c, ref=o_scr: ref[c])
            l_get.append(lambda c, ref=l_scr: ref[c])

    wts = []
    for step in range(HEADS_PER_GROUP // heads):
        ml = [get(step) for get in l_get]
        mx = functools.reduce(jnp.maximum, ml)
        es = [jnp.exp2(t - mx) for t in ml]
        den = functools.reduce(jnp.add, [e * pltpu.roll(t, _ML_SHIFT, axis=1)
                                         for e, t in zip(es, ml)])
        inv = 1.0 / den
        wts.append([e * inv for e in es])
    attn = []
    for h in range(HEADS_PER_GROUP):
        w = wts[h // heads]
        c = h % heads
        attn.append(functools.reduce(
            jnp.add, [w[g][:, c:c + 1] * o_get[g](h) for g in range(N_GROUPS)]))
    attn = jnp.concatenate(attn, axis=-1).astype(_BF16)
    pa = jnp.dot(attn, wa_ref[...], preferred_element_type=_F32)
    merged = (_sigmoid(gate_ring[slot, :, :D]) * pa + _sigmoid(gate_ring[slot, :, D:]) * pr)
    out_ref[...] = x_ref[...] + jnp.dot(merged.astype(_BF16), wo_ref[...],
                                        preferred_element_type=_F32)


def _merge(os_, lses, rnn, zr, ga_col, gr_col, x2d, wa, wr, wo, B, S, *, tm):
    T, D = x2d.shape
    assert S % tm == 0 and gr_col == ga_col + D
    heads = _attn_heads_per_step(S)
    LW = (HEADS_PER_GROUP // heads) * _LANES
    tiles_per_seq = S // tm
    row = lambda w: pl.BlockSpec((tm, w), lambda i: (i, 0))
    const = lambda a: pl.BlockSpec(a.shape, lambda i: (0, 0), pipeline_mode=pl.Buffered(1))

    def group_spec(dil, w):
        if dil == 1:
            return row(w)
        assert tm % (dil * _SUBLANES) == 0
        return pl.BlockSpec((None, dil, tm // dil, w),
                            lambda i: (i // tiles_per_seq, 0, i % tiles_per_seq, 0))

    scratch = [pltpu.VMEM((_GATE_RING_SLOTS, tm, 2 * D), _F32),
               pltpu.SemaphoreType.DMA((_GATE_RING_SLOTS,))]
    for _, dil in DILATED_GROUPS:
        if dil > 1:
            scratch += [pltpu.VMEM((ATTN_OUT_W // _LANES, tm, _LANES), _F32),
                        pltpu.VMEM((LW // _LANES, tm, _LANES), _F32)]
    os_ = [o.reshape(T, ATTN_OUT_W) if dil == 1 else o for o, (_, dil) in zip(os_, DILATED_GROUPS)]
    lses = [l.reshape(T, LW) if dil == 1 else l for l, (_, dil) in zip(lses, DILATED_GROUPS)]
    return pl.pallas_call(
        functools.partial(_merge_kernel, heads=heads, n_steps=T // tm, gate_col=ga_col),
        grid=(T // tm,),
        in_specs=[group_spec(dil, ATTN_OUT_W) for _, dil in DILATED_GROUPS]
        + [group_spec(dil, LW) for _, dil in DILATED_GROUPS]
        + [row(LRU_W), pl.BlockSpec(memory_space=pl.ANY),
           row(D), const(wa), const(wr), const(wo)],
        out_specs=row(D),
        out_shape=jax.ShapeDtypeStruct((T, D), _F32),
        scratch_shapes=scratch,
        compiler_params=_params(("arbitrary",)),
        name="merge_proj",
    )(*os_, *lses, rnn, zr, x2d, wa, wr, wo)


def _mlp_kernel(x_ref, g_ref, w1_ref, w2_ref, gf_ref, o_ref, u_ref):
    j = pl.program_id(1)

    def ff_chunk():
        h = jnp.dot(u_ref[...], w1_ref[...], preferred_element_type=_F32)
        h = jnp.square(jnp.maximum(h, 0.0)).astype(_BF16)
        return jnp.dot(h, w2_ref[...], preferred_element_type=_F32)

    @pl.when(j == 0)
    def _():
        x = x_ref[...]
        u_ref[...] = _rms(x, g_ref[...]).astype(u_ref.dtype)
        o_ref[...] = x + ff_chunk()

    @pl.when(j > 0)
    def _():
        o_ref[...] += ff_chunk()

    @pl.when(j == pl.num_programs(1) - 1)
    def _():
        o_ref[...] = _rms(o_ref[...], gf_ref[...])


def _mlp(x2d, g, w1, w2, gf, *, tm, tf):
    T, D = x2d.shape
    F = w1.shape[1]
    assert T % tm == 0 and F % tf == 0
    return pl.pallas_call(
        _mlp_kernel,
        grid=(T // tm, F // tf),
        in_specs=[pl.BlockSpec((tm, D), lambda i, j: (i, 0)),
                  pl.BlockSpec((1, D), lambda i, j: (0, 0)),
                  pl.BlockSpec((D, tf), lambda i, j: (0, j)),
                  pl.BlockSpec((tf, D), lambda i, j: (j, 0)),
                  pl.BlockSpec((1, D), lambda i, j: (0, 0))],
        out_specs=pl.BlockSpec((tm, D), lambda i, j: (i, 0)),
        out_shape=jax.ShapeDtypeStruct((T, D), _F32),
        scratch_shapes=[pltpu.VMEM((tm, D), _BF16)],
        compiler_params=_params(("parallel", "arbitrary")),
        name="mlp_final_norm",
    )(x2d, g, w1, w2, gf)


_REST_GA = 0
_REST_GR = D_MODEL
_REST_RX = 2 * D_MODEL
_REST_RY = 2 * D_MODEL + LRU_W
_REST_SEGMENTS = ((2 * LRU_W, _REST_GA, D_MODEL), (2 * LRU_W + D_MODEL, _REST_GR, D_MODEL),
                  (0, _REST_RX, LRU_W), (LRU_W, _REST_RY, LRU_W))


def _layer(x, bias_tabs, p):
    B, S, D = x.shape
    T = B * S
    x2d = x.reshape(T, D)
    zgs = _qkv_proj(x2d, p["g_mix"], p["w_qkv"], B, S, tm=_QKV_TM)
    zr = _norm_proj(x2d, p["g_mix"], p["w_rest"], tm=_REST_TM, segments=_REST_SEGMENTS,
                    out_dtype=_F32)
    os_, lses = [], []
    for g, (_, dil) in enumerate(DILATED_GROUPS):
        o, lse = _attn_group(zgs[g], bias_tabs, g, dil, B, S)
        os_.append(o)
        lses.append(lse)
    rnn = _rglru(zr.reshape(B, S, -1), _REST_RX, _REST_RY, p["conv_w"], p["conv_b"],
                 p["w_gate"], p["b_gate"], p["lam"])
    x2 = _merge(os_, lses, rnn, zr, _REST_GA, _REST_GR, x2d,
                p["w_attn_o"], p["w_rnn_o"], p["w_out"], B, S, tm=_MERGE_TM)
    y = _mlp(x2, p["g_mlp"], p["w_mlp_in"], p["w_mlp_out"], p["g_final"], tm=_MLP_TM, tf=_MLP_TF)
    return y.reshape(B, S, D)


def kernel(x_prompt, x_sample, rel_bias, norm_mix_g, w_in, conv_w, conv_b, lru_wa, lru_ba,
           lru_wx, lru_bx, lru_lambda, w_attn_o, w_rnn_o, w_out, norm_mlp_g, w_mlp_in,
           w_mlp_out, norm_final_g):
    depth = w_in.shape[0]
    assert depth == 1, "the final norm is fused into the single layer's MLP kernel"
    bias_tabs = _bias_tables(rel_bias)
    w_qkv, w_rest = _cast_split(w_in[0], ((0, 3 * ATTN_W), (3 * ATTN_W, w_in.shape[2])),
                                tm=_CAST_TM)
    w_gate = jnp.concatenate([lru_wa[0, 0], lru_wx[0, 0], lru_wa[0, 1], lru_wx[0, 1]],
                             axis=-1).astype(_BF16)
    b_gate = jnp.concatenate(
        [b.reshape(LRU_BLOCKS, 1, LRU_BLOCK_W)
         for b in (lru_ba[0, 0], lru_bx[0, 0], lru_ba[0, 1], lru_bx[0, 1])], axis=-1)
    p = dict(
        g_mix=norm_mix_g[0].reshape(1, D_MODEL),
        w_qkv=w_qkv, w_rest=w_rest,
        conv_w=conv_w[0], conv_b=conv_b[0].reshape(1, LRU_W),
        w_gate=w_gate, b_gate=0.5 * b_gate.astype(_F32), lam=lru_lambda[0],
        w_attn_o=w_attn_o[0].astype(_BF16), w_rnn_o=w_rnn_o[0].astype(_BF16),
        w_out=w_out[0].astype(_BF16),
        g_mlp=norm_mlp_g[0].reshape(1, D_MODEL),
        w_mlp_in=w_mlp_in[0].astype(_BF16), w_mlp_out=w_mlp_out[0].astype(_BF16),
        g_final=norm_final_g.reshape(1, D_MODEL),
    )
    return (_layer(x_prompt, bias_tabs, p), _layer(x_sample, bias_tabs, p))
```
